```python
import math
import jax, jax.numpy as jnp
from jax import lax
import numpy as np

D_MODEL = 1024
BATCH = 16
SEQ = 4096
DEPTH = 4

HEAD_DIM = 64
A_WIDTH = D_MODEL // 2
A_HEADS = A_WIDTH // HEAD_DIM
DILATED_BRANCHES = ((128, 1), (512, 4), (2048, 16))
BLOCK = 128
B_WIDTH = D_MODEL // 2
B_GROUPS = 4
B_GROUP_CH = B_WIDTH // B_GROUPS
CHUNK = 128
C_WIDTH = D_MODEL // 2
CONV_WIDTH = 31
D_WIDTH = D_MODEL // 2
D_HEADS = D_WIDTH // HEAD_DIM
N_BUCKETS = 32
MAX_DISTANCE = 2048
N_GROUPS = 4
EXPERTS_PER_GROUP = 4
N_EXPERTS = N_GROUPS * EXPERTS_PER_GROUP
EXPERT_TOP_K = 2
D_EXPERT = D_MODEL // 2
EPS = 1e-6
NEG = -1e30
N_EVEN = (DEPTH + 1) // 2
N_ODD = DEPTH // 2

kernel_name = "hybrid_dilated_gmlp_conformer_stickbreak_hmoe"


def rms_norm(x, g):
    xf = x.astype(jnp.float32)
    y = xf * lax.rsqrt(jnp.mean(xf * xf, -1, keepdims=True) + EPS)
    return (y * g.astype(jnp.float32)).astype(x.dtype)


def layer_norm(x, g, b):
    xf = x.astype(jnp.float32)
    mu = jnp.mean(xf, -1, keepdims=True)
    xc = xf - mu
    var = jnp.mean(xc * xc, -1, keepdims=True)
    y = xc * lax.rsqrt(var + EPS) * g.astype(jnp.float32) + b.astype(jnp.float32)
    return y.astype(x.dtype)


def t5_bucket(dist):
    max_exact = N_BUCKETS // 2
    d = np.maximum(dist, 1).astype(np.float32)
    large = max_exact + (np.log(d / max_exact) / np.log(MAX_DISTANCE / max_exact)
                         * (N_BUCKETS - max_exact)).astype(np.int32)
    large = np.minimum(large, N_BUCKETS - 1)
    return np.where(dist < max_exact, dist, large).astype(np.int32)


def dilated_branch(q, k, v, rel_table, window, dilation):
    b, h, s, hd = q.shape
    n_keys = window // dilation
    sub_len = -(-s // dilation)
    lp = -(-sub_len // BLOCK) * BLOCK
    sp = lp * dilation
    nb = lp // BLOCK
    pad = ((0, 0), (0, 0), (0, sp - s), (0, 0))

    def to_blocks(t):
        t = jnp.pad(t, pad).reshape(b, h, lp, dilation, hd)
        t = jnp.swapaxes(t, 2, 3)
        return t.reshape(b, h, dilation, nb, BLOCK, hd)

    def with_prev(t):
        prev = jnp.pad(t, ((0, 0), (0, 0), (0, 0), (1, 0), (0, 0), (0, 0)))[:, :, :, :-1]
        return jnp.concatenate([prev, t], axis=4)

    qb = to_blocks(q)
    kb = with_prev(to_blocks(k))
    vb = with_prev(to_blocks(v))
    scores = jnp.einsum('bhrnqd,bhrnkd->bhrnqk', qb, kb)

    qi = np.arange(BLOCK)[:, None]
    kj = np.arange(2 * BLOCK)[None, :]
    sub_dist = qi + BLOCK - kj
    in_window = (sub_dist >= 0) & (sub_dist <= n_keys)
    bucket = t5_bucket(np.clip(sub_dist, 0, None) * dilation)
    bias = jnp.transpose(rel_table.astype(jnp.float32)[bucket], (2, 0, 1))
    first_block = (np.arange(nb) == 0)[:, None, None]
    valid = in_window[None] & ~(first_block & (kj < BLOCK)[None])
    logits = jnp.where(valid, scores + bias[None, :, None, None], NEG)

    mx = jnp.max(logits, -1)
    p = jnp.exp(logits - mx[..., None])
    den = jnp.sum(p, -1)
    num = jnp.einsum('bhrnqk,bhrnkd->bhrnqd', p, vb)

    def from_blocks(t):
        rest = t.shape[5:]
        t = t.reshape(b, h, dilation, lp, *rest)
        t = jnp.swapaxes(t, 2, 3).reshape(b, h, sp, *rest)
        return t[:, :, :s]

    return from_blocks(num), from_blocks(den), from_blocks(mx)


def dilated_attention(q, k, v, rel_table):
    b, s, h, hd = q.shape
    qf = jnp.swapaxes(q.astype(jnp.float32), 1, 2) * (hd ** -0.5)
    kf = jnp.swapaxes(k.astype(jnp.float32), 1, 2)
    vf = jnp.swapaxes(v.astype(jnp.float32), 1, 2)
    outs = [dilated_branch(qf, kf, vf, rel_table, w, d) for (w, d) in DILATED_BRANCHES]
    m_all = jnp.max(jnp.stack([m for (_, _, m) in outs]), 0)
    num = sum(n * jnp.exp(m - m_all)[..., None] for (n, _, m) in outs)
    den = sum(l * jnp.exp(m - m_all) for (_, l, m) in outs)
    o = num / den[..., None]
    return jnp.swapaxes(o, 1, 2).reshape(b, s, h * hd)


def spatial_gating(u, v, ln_g, ln_b, w_s, b_s):
    b, s, _ = v.shape
    u = jax.nn.gelu(u)
    v = layer_norm(jax.nn.gelu(v), ln_g, ln_b)
    vc = v.reshape(b, s // CHUNK, CHUNK, B_GROUPS, B_GROUP_CH)
    tril = np.tril(np.ones((CHUNK, CHUNK), dtype=bool))
    w = jnp.where(tril[None], w_s, 0.0)
    mixed = jnp.einsum('gts,bnsgc->bntgc', w, vc) + jnp.swapaxes(b_s, 0, 1)[None, None, :, :, None]
    return u * mixed.reshape(b, s, B_WIDTH).astype(u.dtype)


def conv_module(a, gate, w_dw, b_dw, ln_g, ln_b):
    hcur = a * jax.nn.sigmoid(gate)
    hcur = lax.conv_general_dilated(
        hcur, w_dw[:, None, :].astype(hcur.dtype), window_strides=(1,),
        padding=[(CONV_WIDTH - 1, 0)], dimension_numbers=('NWC', 'WIO', 'NWC'),
        feature_group_count=C_WIDTH) + b_dw
    hcur = layer_norm(hcur, ln_g, ln_b)
    return jax.nn.silu(hcur)


def stick_breaking_attention(q, k, v):
    b, s, h, hd = q.shape
    qf = jnp.transpose(q.astype(jnp.float32), (0, 2, 1, 3)) * (hd ** -0.5)
    kf = jnp.transpose(k.astype(jnp.float32), (0, 2, 1, 3))
    vf = jnp.transpose(v.astype(jnp.float32), (0, 2, 1, 3))
    nb = s // BLOCK
    q_blocks = jnp.moveaxis(qf.reshape(b, h, nb, BLOCK, hd), 2, 0)
    key_pos = jnp.arange(s)

    def block(args):
        qb, i = args
        z = jnp.einsum('bhqd,bhkd->bhqk', qb, kf)
        t = i * BLOCK + jnp.arange(BLOCK)
        causal = key_pos[None, :] < t[:, None]
        log_keep = jnp.where(causal, jax.nn.log_sigmoid(-z), 0.0)
        between = lax.cumsum(log_keep, axis=3, reverse=True) - log_keep
        att = jnp.where(causal, jnp.exp(jax.nn.log_sigmoid(z) + between), 0.0)
        return jnp.einsum('bhqk,bhkd->bhqd', att, vf)

    out = lax.map(block, (q_blocks, jnp.arange(nb)))
    out = jnp.moveaxis(out, 0, 2).reshape(b, h, s, hd)
    return jnp.transpose(out, (0, 2, 1, 3)).reshape(b, s, h * hd)


def hierarchical_moe(hin, w_group, b_group, w_router, b_router, w_gate, w_up, w_down):
    b, s, d = hin.shape
    xt = hin.reshape(b * s, d)
    xf = xt.astype(jnp.float32)
    g_prob = jax.nn.softmax(xf @ w_group.astype(jnp.float32) + b_group.astype(jnp.float32), -1)
    g_w, g_idx = lax.top_k(g_prob, 1)
    e_logits = jnp.einsum('td,gde->tge', xf, w_router.astype(jnp.float32)) + b_router.astype(jnp.float32)
    e_logits = jnp.take_along_axis(e_logits, g_idx[:, :, None], axis=1)[:, 0]
    e_val, e_idx = lax.top_k(e_logits, EXPERT_TOP_K)
    e_w = jax.nn.softmax(e_val, -1)
    flat_idx = g_idx * EXPERTS_PER_GROUP + e_idx
    gates = jnp.sum(jax.nn.one_hot(flat_idx, N_EXPERTS, dtype=jnp.float32)
                    * (g_w * e_w)[..., None], axis=1)
    y = jnp.zeros((b * s, d), jnp.float32)
    for e in range(N_EXPERTS):
        hid = jax.nn.silu(xt @ w_gate[e]) * (xt @ w_up[e])
        y = y + gates[:, e:e + 1] * (hid @ w_down[e]).astype(jnp.float32)
    return y.astype(hin.dtype).reshape(b, s, d)


def setup_inputs(seed: int = 0) -> dict:
    key = jax.random.key(seed)
    ks = iter(jax.random.split(key, 32))

    def nrm(shape, scale):
        return scale * jax.random.normal(next(ks), shape, jnp.float32)

    d = D_MODEL
    ev_in = 3 * A_WIDTH + 2 * B_WIDTH
    od_in = 2 * C_WIDTH + 3 * D_WIDTH
    return {
        "x": nrm((BATCH, SEQ, d), 1.0),
        "c": nrm((BATCH, d), 1.0),
        "rel_bias": nrm((N_BUCKETS, A_HEADS), 0.5),
        "norm1_g": 1.0 + nrm((DEPTH, d), 0.1),
        "norm2_g": 1.0 + nrm((DEPTH, d), 0.1),
        "w_ada": nrm((DEPTH, d, 6 * d), 0.5 * d ** -0.5),
        "b_ada": nrm((DEPTH, 6 * d), 0.02),
        "ev_w_in": nrm((N_EVEN, d, ev_in), d ** -0.5),
        "ev_w_out": nrm((N_EVEN, A_WIDTH + B_WIDTH, d), (A_WIDTH + B_WIDTH) ** -0.5),
        "ev_gmlp_ln_g": 1.0 + nrm((N_EVEN, B_WIDTH), 0.1),
        "ev_gmlp_ln_b": nrm((N_EVEN, B_WIDTH), 0.02),
        "ev_w_s": nrm((N_EVEN, B_GROUPS, CHUNK, CHUNK), CHUNK ** -0.5),
        "ev_b_s": 1.0 + nrm((N_EVEN, B_GROUPS, CHUNK), 0.1),
        "od_w_in": nrm((N_ODD, d, od_in), d ** -0.5),
        "od_w_out": nrm((N_ODD, C_WIDTH + D_WIDTH, d), (C_WIDTH + D_WIDTH) ** -0.5),
        "od_w_dw": nrm((N_ODD, CONV_WIDTH, C_WIDTH), CONV_WIDTH ** -0.5),
        "od_b_dw": nrm((N_ODD, C_WIDTH), 0.02),
        "od_conv_ln_g": 1.0 + nrm((N_ODD, C_WIDTH), 0.1),
        "od_conv_ln_b": nrm((N_ODD, C_WIDTH), 0.02),
        "moe_w_group": nrm((DEPTH, d, N_GROUPS), d ** -0.5),
        "moe_b_group": nrm((DEPTH, N_GROUPS), 0.01),
        "moe_w_router": nrm((DEPTH, N_GROUPS, d, EXPERTS_PER_GROUP), d ** -0.5),
        "moe_b_router": nrm((DEPTH, N_GROUPS, EXPERTS_PER_GROUP), 0.01),
        "moe_w_gate": nrm((DEPTH, N_EXPERTS, d, D_EXPERT), d ** -0.5),
        "moe_w_up": nrm((DEPTH, N_EXPERTS, d, D_EXPERT), d ** -0.5),
        "moe_w_down": nrm((DEPTH, N_EXPERTS, D_EXPERT, d), D_EXPERT ** -0.5),
        "final_norm_g": 1.0 + nrm((d,), 0.1),
    }


def reference(x, c, rel_bias, norm1_g, norm2_g, w_ada, b_ada, ev_w_in, ev_w_out,
              ev_gmlp_ln_g, ev_gmlp_ln_b, ev_w_s, ev_b_s, od_w_in, od_w_out, od_w_dw,
              od_b_dw, od_conv_ln_g, od_conv_ln_b, moe_w_group, moe_b_group, moe_w_router,
              moe_b_router, moe_w_gate, moe_w_up, moe_w_down, final_norm_g):
    b, s, _ = x.shape
    for layer in range(DEPTH):
        j = layer // 2
        mod = jax.nn.silu(c) @ w_ada[layer] + b_ada[layer]
        sh1, sc1, g1, sh2, sc2, g2 = jnp.split(mod[:, None, :], 6, axis=-1)
        hcur = rms_norm(x, norm1_g[layer]) * (1 + sc1) + sh1
        if layer % 2 == 0:
            proj = hcur @ ev_w_in[j]
            q, k, v, u, gv = jnp.split(
                proj, [A_WIDTH, 2 * A_WIDTH, 3 * A_WIDTH, 3 * A_WIDTH + B_WIDTH], axis=-1)
            o_a = dilated_attention(q.reshape(b, s, A_HEADS, HEAD_DIM),
                                    k.reshape(b, s, A_HEADS, HEAD_DIM),
                                    v.reshape(b, s, A_HEADS, HEAD_DIM), rel_bias)
            o_b = spatial_gating(u, gv, ev_gmlp_ln_g[j], ev_gmlp_ln_b[j], ev_w_s[j], ev_b_s[j])
            mix = jnp.concatenate([o_a.astype(x.dtype), o_b.astype(x.dtype)], -1) @ ev_w_out[j]
        else:
            proj = hcur @ od_w_in[j]
            a, gate, q, k, v = jnp.split(
                proj, [C_WIDTH, 2 * C_WIDTH, 2 * C_WIDTH + D_WIDTH, 2 * C_WIDTH + 2 * D_WIDTH], axis=-1)
            o_c = conv_module(a, gate, od_w_dw[j], od_b_dw[j], od_conv_ln_g[j], od_conv_ln_b[j])
            o_d = stick_breaking_attention(q.reshape(b, s, D_HEADS, HEAD_DIM),
                                           k.reshape(b, s, D_HEADS, HEAD_DIM),
                                           v.reshape(b, s, D_HEADS, HEAD_DIM))
            mix = jnp.concatenate([o_c.astype(x.dtype), o_d.astype(x.dtype)], -1) @ od_w_out[j]
        x = x + g1 * mix
        hcur = rms_norm(x, norm2_g[layer]) * (1 + sc2) + sh2
        x = x + g2 * hierarchical_moe(hcur, moe_w_group[layer], moe_b_group[layer],
                                      moe_w_router[layer], moe_b_router[layer],
                                      moe_w_gate[layer], moe_w_up[layer], moe_w_down[layer])
    return rms_norm(x, final_norm_g)
```

```python
import functools
import math

import numpy as np
import jax
import jax.numpy as jnp
from jax import lax
from jax.experimental import pallas as pl
from jax.experimental.pallas import tpu as pltpu

F32 = jnp.float32
BF16 = jnp.bfloat16

HEAD_DIM = 64
HEADS = 8
WIDTH = HEADS * HEAD_DIM
PAIR = 2 * HEAD_DIM
BLK = 128
DILATED_BRANCHES = ((128, 1), (512, 4), (2048, 16))
N_BUCKETS = 32
MAX_DISTANCE = 2048
CONV_WIDTH = 31
CONV_HALO = 32
B_GROUPS = 4
N_GROUPS = 4
EXPERTS_PER_GROUP = 4
N_EXPERTS = N_GROUPS * EXPERTS_PER_GROUP
EPS = 1e-6
NEG = -1e30
LANES = 128
VMEM_LIMIT = 48 * 1024 * 1024


def _params(sem):
    return pltpu.CompilerParams(dimension_semantics=sem, vmem_limit_bytes=VMEM_LIMIT)


def _bdot(a, b):
    return jnp.dot(a, b, preferred_element_type=F32)


def _bdot_nt(a, b):
    return lax.dot_general(a, b, (((1,), (1,)), ((), ())), preferred_element_type=F32)


def _split(a):
    hi = a.astype(BF16)
    lo = (a - hi.astype(F32)).astype(BF16)
    return hi, lo


def _dot3(a, w_hi, w_lo):
    a_hi, a_lo = _split(a)
    return _bdot(a_hi, w_hi) + (_bdot(a_hi, w_lo) + _bdot(a_lo, w_hi))


def _ada_kernel(c_ref, w_ref, b_ref, o_ref):
    c = c_ref[...]
    sc = c * jax.nn.sigmoid(c)
    w_hi, w_lo = _split(w_ref[...])
    o_ref[...] = _dot3(sc, w_hi, w_lo) + b_ref[...]


def _ada(c, w_ada, b_ada):
    depth, d, n = w_ada.shape
    b = c.shape[0]
    tn = 1024
    return pl.pallas_call(
        _ada_kernel,
        grid=(depth, n // tn),
        in_specs=[
            pl.BlockSpec((b, d), lambda l, j: (0, 0)),
            pl.BlockSpec((None, d, tn), lambda l, j: (l, 0, j)),
            pl.BlockSpec((None, 1, tn), lambda l, j: (l, 0, j)),
        ],
        out_specs=pl.BlockSpec((None, b, tn), lambda l, j: (l, 0, j)),
        out_shape=jax.ShapeDtypeStruct((depth, b, n), F32),
        compiler_params=_params(("arbitrary", "arbitrary")),
        name="ada_mod",
    )(c, w_ada, b_ada.reshape(depth, 1, n))


def _rms_mod(x, g, scale_row, shift_row):
    y = x * lax.rsqrt(jnp.mean(x * x, -1, keepdims=True) + EPS) * g
    return y * (1.0 + scale_row) + shift_row


def _in_kernel(x_ref, g_ref, mod_ref, w_ref, o_ref):
    m = mod_ref[...]
    h = _rms_mod(x_ref[...], g_ref[...], m[1:2], m[0:1]).astype(BF16)
    n = w_ref.shape[1]
    for j in range(n // WIDTH):
        cols = slice(j * WIDTH, (j + 1) * WIDTH)
        o_ref[:, cols] = _bdot(h, w_ref[:, cols]).astype(o_ref.dtype)


def _in_proj(x, g, mod, w_in):
    b, s, d = x.shape
    n = w_in.shape[1]
    tm = 512
    return pl.pallas_call(
        _in_kernel,
        grid=(b, s // tm),
        in_specs=[
            pl.BlockSpec((None, tm, d), lambda i, t: (i, t, 0)),
            pl.BlockSpec((1, d), lambda i, t: (0, 0)),
            pl.BlockSpec((None, 6, d), lambda i, t: (i, 0, 0)),
            pl.BlockSpec((d, n), lambda i, t: (0, 0)),
        ],
        out_specs=pl.BlockSpec((None, tm, n), lambda i, t: (i, t, 0)),
        out_shape=jax.ShapeDtypeStruct((b, s, n), BF16),
        compiler_params=_params(("arbitrary", "arbitrary")),
        name="in_proj",
    )(x, g.reshape(1, d), mod, w_in)


def _t5_bucket(dist):
    max_exact = N_BUCKETS // 2
    d = np.maximum(dist, 1).astype(np.float32)
    large = max_exact + (np.log(d / max_exact) / np.log(MAX_DISTANCE / max_exact)
                         * (N_BUCKETS - max_exact)).astype(np.int32)
    large = np.minimum(large, N_BUCKETS - 1)
    return np.where(dist < max_exact, dist, large).astype(np.int32)


def _branch_bias(rel_table, dilation):
    qi = np.arange(BLK)[:, None]
    kj = np.arange(2 * BLK)[None, :]
    bucket = _t5_bucket(np.clip(qi + BLK - kj, 0, None) * dilation)
    return jnp.transpose(rel_table.astype(F32)[bucket], (2, 0, 1))


def _dil_kernel(q_ref, kp_ref, kc_ref, vp_ref, vc_ref, bias_ref, o_ref, m_ref, l_ref,
                kbuf, vbuf, *, tq):
    n = pl.program_id(2)
    nsub = tq // BLK
    kbuf[0:BLK] = kp_ref[...]
    kbuf[BLK:] = kc_ref[...]
    vbuf[0:BLK] = vp_ref[...]
    vbuf[BLK:] = vc_ref[...]
    qi = lax.broadcasted_iota(jnp.int32, (BLK, 2 * BLK), 0)
    kj = lax.broadcasted_iota(jnp.int32, (BLK, 2 * BLK), 1)
    dist = qi + BLK - kj
    in_window = (dist >= 0) & (dist <= BLK)
    lane = lax.broadcasted_iota(jnp.int32, (BLK, PAIR), 1)
    scale = HEAD_DIM ** -0.5
    for i in range(nsub):
        first_key = jnp.where(n * nsub + i == 0, BLK, 0)
        valid = in_window & (kj >= first_key)
        rows = slice(i * BLK, (i + 1) * BLK)
        krows = slice(i * BLK, (i + 2) * BLK)
        m_tile = jnp.zeros((BLK, LANES), F32)
        l_tile = jnp.ones((BLK, LANES), F32)
        for p in range(HEADS // 2):
            cols = slice(p * PAIR, (p + 1) * PAIR)
            q_pair = q_ref[rows, cols]
            k_pair = kbuf[krows, cols]
            v_pair = vbuf[krows, cols]
            outs = []
            for hh in range(2):
                h = 2 * p + hh
                head_lanes = (lane >= HEAD_DIM) if hh else (lane < HEAD_DIM)
                qm = jnp.where(head_lanes, q_pair, jnp.zeros_like(q_pair))
                s = _bdot_nt(qm, k_pair) * scale
                logit = jnp.where(valid, s + bias_ref[h], NEG)
                mx = jnp.max(logit, -1, keepdims=True)
                pr = jnp.exp(logit - mx)
                den = jnp.sum(pr, -1, keepdims=True)
                outs.append(_bdot(pr.astype(BF16), v_pair) / den)
                m_tile = jnp.where(lane == h, mx, m_tile)
                l_tile = jnp.where(lane == h, den, l_tile)
            o_ref[rows, cols] = jnp.where(lane < HEAD_DIM, outs[0], outs[1]).astype(o_ref.dtype)
        m_ref[rows, :] = m_tile
        l_ref[rows, :] = l_tile


def _dilated_branch(proj, bias, dilation):
    b, s, n = proj.shape
    sub = s // dilation
    tq = min(512, sub)
    nsub = tq // BLK
    ncol = n // WIDTH
    view = proj.reshape(b, sub, dilation * n)
    cur = lambda c: pl.BlockSpec((None, tq, WIDTH), lambda i, r, t: (i, t, r * ncol + c))
    prev = lambda c: pl.BlockSpec(
        (None, BLK, WIDTH), lambda i, r, t: (i, jnp.maximum(t * nsub - 1, 0), r * ncol + c))
    o, m, l = pl.pallas_call(
        functools.partial(_dil_kernel, tq=tq),
        grid=(b, dilation, sub // tq),
        in_specs=[cur(0), prev(1), cur(1), prev(2), cur(2),
                  pl.BlockSpec((HEADS, BLK, 2 * BLK), lambda i, r, t: (0, 0, 0))],
        out_specs=[
            pl.BlockSpec((None, tq, WIDTH), lambda i, r, t: (i, t, r)),
            pl.BlockSpec((None, tq, LANES), lambda i, r, t: (i, t, r)),
            pl.BlockSpec((None, tq, LANES), lambda i, r, t: (i, t, r)),
        ],
        out_shape=[
            jax.ShapeDtypeStruct((b, sub, dilation * WIDTH), BF16),
            jax.ShapeDtypeStruct((b, sub, dilation * LANES), F32),
            jax.ShapeDtypeStruct((b, sub, dilation * LANES), F32),
        ],
        scratch_shapes=[pltpu.VMEM((BLK + tq, WIDTH), BF16), pltpu.VMEM((BLK + tq, WIDTH), BF16)],
        compiler_params=_params(("arbitrary", "arbitrary", "arbitrary")),
        name=f"dilated_d{dilation}",
    )(view, view, view, view, view, bias)
    return (o.reshape(b, s, WIDTH), m.reshape(b, s, LANES), l.reshape(b, s, LANES))


def _comb_kernel(o1, o2, o3, m1, m2, m3, l1, l2, l3, e_ref, out_ref):
    ms = [m1[...], m2[...], m3[...]]
    m_all = jnp.maximum(jnp.maximum(ms[0], ms[1]), ms[2])
    es = [l[...] * jnp.exp(m - m_all) for l, m in zip((l1, l2, l3), ms)]
    tot = es[0] + es[1] + es[2]
    acc = None
    for e, o in zip(es, (o1, o2, o3)):
        w_hi, w_lo = _split(e / tot)
        w = _bdot(w_hi, e_ref[...]) + _bdot(w_lo, e_ref[...])
        term = w * o[...].astype(F32)
        acc = term if acc is None else acc + term
    out_ref[...] = acc.astype(out_ref.dtype)


def _combine_branches(outs):
    (o1, m1, l1), (o2, m2, l2), (o3, m3, l3) = outs
    b, s, _ = o1.shape
    tm = 512
    expand = np.zeros((LANES, WIDTH), np.float32)
    for h in range(HEADS):
        expand[h, h * HEAD_DIM:(h + 1) * HEAD_DIM] = 1.0
    wide = pl.BlockSpec((None, tm, WIDTH), lambda i, t: (i, t, 0))
    stat = pl.BlockSpec((None, tm, LANES), lambda i, t: (i, t, 0))
    return pl.pallas_call(
        _comb_kernel,
        grid=(b, s // tm),
        in_specs=[wide] * 3 + [stat] * 6 + [pl.BlockSpec((LANES, WIDTH), lambda i, t: (0, 0))],
        out_specs=wide,
        out_shape=jax.ShapeDtypeStruct((b, s, WIDTH), BF16),
        compiler_params=_params(("arbitrary", "arbitrary")),
        name="dilated_combine",
    )(o1, o2, o3, m1, m2, m3, l1, l2, l3, jnp.asarray(expand, BF16))


def _gelu(x):
    return 0.5 * x * (1.0 + jnp.tanh(math.sqrt(2.0 / math.pi) * (x + 0.044715 * (x * x * x))))


def _gmlp_kernel(u_ref, v_ref, g_ref, b_ref, ws_ref, bs_ref, o_ref):
    tm = u_ref.shape[0]
    u = _gelu(u_ref[...].astype(F32))
    v = _gelu(v_ref[...].astype(F32))
    mu = jnp.mean(v, -1, keepdims=True)
    vc = v - mu
    var = jnp.mean(vc * vc, -1, keepdims=True)
    v = (vc * lax.rsqrt(var + EPS) * g_ref[...] + b_ref[...]).astype(BF16)
    r = lax.broadcasted_iota(jnp.int32, (BLK, BLK), 0)
    c = lax.broadcasted_iota(jnp.int32, (BLK, BLK), 1)
    for g in range(B_GROUPS):
        w = jnp.where(c <= r, ws_ref[g], 0.0).astype(BF16)
        cols = slice(g * BLK, (g + 1) * BLK)
        for ch in range(tm // BLK):
            rows = slice(ch * BLK, (ch + 1) * BLK)
            mixed = _bdot(w, v[rows, cols]) + bs_ref[g]
            o_ref[rows, cols] = (u[rows, cols] * mixed).astype(o_ref.dtype)


def _gmlp(proj, ln_g, ln_b, w_s, b_s):
    b, s, n = proj.shape
    tm = 512
    return pl.pallas_call(
        _gmlp_kernel,
        grid=(b, s // tm),
        in_specs=[
            pl.BlockSpec((None, tm, WIDTH), lambda i, t: (i, t, 3)),
            pl.BlockSpec((None, tm, WIDTH), lambda i, t: (i, t, 4)),
            pl.BlockSpec((1, WIDTH), lambda i, t: (0, 0)),
            pl.BlockSpec((1, WIDTH), lambda i, t: (0, 0)),
            pl.BlockSpec((B_GROUPS, BLK, BLK), lambda i, t: (0, 0, 0)),
            pl.BlockSpec((B_GROUPS, BLK, 1), lambda i, t: (0, 0, 0)),
        ],
        out_specs=pl.BlockSpec((None, tm, WIDTH), lambda i, t: (i, t, 0)),
        out_shape=jax.ShapeDtypeStruct((b, s, WIDTH), BF16),
        compiler_params=_params(("arbitrary", "arbitrary")),
        name="gmlp_gate",
    )(proj, proj, ln_g.reshape(1, WIDTH), ln_b.reshape(1, WIDTH), w_s,
      b_s.reshape(B_GROUPS, BLK, 1))


def _conv_kernel(a_ref, g_ref, ah_ref, gh_ref, w_ref, bdw_ref, lng_ref, lnb_ref, o_ref, hbuf):
    tm = a_ref.shape[0]
    t = pl.program_id(1)
    halo = ah_ref[...].astype(F32) * jax.nn.sigmoid(gh_ref[...].astype(F32))
    hbuf[0:CONV_HALO] = jnp.where(t == 0, 0.0, halo)
    hbuf[CONV_HALO:] = a_ref[...].astype(F32) * jax.nn.sigmoid(g_ref[...].astype(F32))
    rc = 32
    off = CONV_HALO - (CONV_WIDTH - 1)

    def chunk(ci, carry):
        base = pl.multiple_of(ci * rc, rc)
        win = hbuf[pl.ds(base, rc + CONV_HALO), :]
        acc = jnp.zeros((rc, WIDTH), F32) + bdw_ref[...]
        for k in range(CONV_WIDTH):
            acc = acc + w_ref[k:k + 1, :] * win[off + k:off + k + rc, :]
        mu = jnp.mean(acc, -1, keepdims=True)
        xc = acc - mu
        var = jnp.mean(xc * xc, -1, keepdims=True)
        y = xc * lax.rsqrt(var + EPS) * lng_ref[...] + lnb_ref[...]
        o_ref[pl.ds(base, rc), :] = (y * jax.nn.sigmoid(y)).astype(o_ref.dtype)
        return carry

    lax.fori_loop(0, tm // rc, chunk, 0)


def _conv_module(proj, w_dw, b_dw, ln_g, ln_b):
    b, s, n = proj.shape
    tm = 512
    hb = tm // CONV_HALO
    row = lambda v: v.reshape(1, WIDTH)
    halo = lambda c: pl.BlockSpec(
        (None, CONV_HALO, WIDTH), lambda i, t: (i, jnp.maximum(t * hb - 1, 0), c))
    return pl.pallas_call(
        _conv_kernel,
        grid=(b, s // tm),
        in_specs=[
            pl.BlockSpec((None, tm, WIDTH), lambda i, t: (i, t, 0)),
            pl.BlockSpec((None, tm, WIDTH), lambda i, t: (i, t, 1)),
            halo(0), halo(1),
            pl.BlockSpec((CONV_WIDTH, WIDTH), lambda i, t: (0, 0)),
            pl.BlockSpec((1, WIDTH), lambda i, t: (0, 0)),
            pl.BlockSpec((1, WIDTH), lambda i, t: (0, 0)),
            pl.BlockSpec((1, WIDTH), lambda i, t: (0, 0)),
        ],
        out_specs=pl.BlockSpec((None, tm, WIDTH), lambda i, t: (i, t, 0)),
        out_shape=jax.ShapeDtypeStruct((b, s, WIDTH), BF16),
        scratch_shapes=[pltpu.VMEM((CONV_HALO + tm, WIDTH), F32)],
        compiler_params=_params(("arbitrary", "arbitrary")),
        name="conv_module",
    )(proj, proj, proj, proj, w_dw, row(b_dw), row(ln_g), row(ln_b))


def _stick_kernel(q_ref, k_ref, v_ref, mm_ref, o_ref):
    i = pl.program_id(2)
    q_pair = q_ref[...]
    lane = lax.broadcasted_iota(jnp.int32, (BLK, PAIR), 1)
    row = lax.broadcasted_iota(jnp.int32, (BLK, BLK), 0)
    col = lax.broadcasted_iota(jnp.int32, (BLK, BLK), 1)
    causal = col < row
    scale = HEAD_DIM ** -0.5
    outs = []
    for hh in range(2):
        head_lanes = (lane >= HEAD_DIM) if hh else (lane < HEAD_DIM)
        qm = jnp.where(head_lanes, q_pair, jnp.zeros_like(q_pair))

        def block(j, carry, acc, diag):
            start = pl.multiple_of(j * BLK, BLK)
            kb = k_ref[pl.ds(start, BLK), :]
            vb = v_ref[pl.ds(start, BLK), :]
            z = _bdot_nt(qm, kb) * scale
            soft = jnp.log(1.0 + jnp.exp(-jnp.abs(z)))
            log_keep = -(jnp.maximum(z, 0.0) + soft)
            if diag:
                log_keep = jnp.where(causal, log_keep, 0.0)
            hi, lo = _split(log_keep)
            sums = _bdot(jnp.concatenate([hi, lo], axis=1), mm_ref[...])
            between = sums[:, :BLK] + carry
            att = jnp.exp(jnp.minimum(z, 0.0) - soft + between)
            if diag:
                att = jnp.where(causal, att, 0.0)
            acc = acc + _bdot(att.astype(BF16), vb)
            return carry + sums[:, BLK:], acc

        zero = jnp.zeros((BLK, BLK), F32)
        carry, acc = block(i, zero, zero, True)
        carry, acc = lax.fori_loop(
            0, i, lambda t, c: block(i - 1 - t, c[0], c[1], False), (carry, acc))
        outs.append(acc)
    o_ref[...] = jnp.where(lane < HEAD_DIM, outs[0], outs[1]).astype(o_ref.dtype)


def _suffix_sum_matrix():
    sp = np.arange(BLK)[:, None]
    sc = np.arange(BLK)[None, :]
    half = np.concatenate([(sp > sc).astype(np.float32), np.ones((BLK, BLK), np.float32)], axis=1)
    return jnp.asarray(np.concatenate([half, half], axis=0), BF16)


def _stick_breaking(proj):
    b, s, n = proj.shape
    per = WIDTH // PAIR
    return pl.pallas_call(
        _stick_kernel,
        grid=(b, per, s // BLK),
        in_specs=[
            pl.BlockSpec((None, BLK, PAIR), lambda i, p, t: (i, t, 2 * per + p)),
            pl.BlockSpec((None, s, PAIR), lambda i, p, t: (i, 0, 3 * per + p)),
            pl.BlockSpec((None, s, PAIR), lambda i, p, t: (i, 0, 4 * per + p)),
            pl.BlockSpec((2 * BLK, 2 * BLK), lambda i, p, t: (0, 0)),
        ],
        out_specs=pl.BlockSpec((None, BLK, PAIR), lambda i, p, t: (i, t, p)),
        out_shape=jax.ShapeDtypeStruct((b, s, WIDTH), BF16),
        compiler_params=_params(("arbitrary", "arbitrary", "arbitrary")),
        name="stick_breaking",
    )(proj, proj, proj, _suffix_sum_matrix())


def _lane_min_index(mask, lane):
    return jnp.min(jnp.where(mask, lane, float(LANES)), -1, keepdims=True)


def _out_kernel(l_ref, r_ref, w_ref, x_ref, mod_ref, g_ref, rh_ref, rl_ref, rb_ref,
                xo_ref, h_ref, gate_ref):
    m = mod_ref[...]
    mix = _bdot(l_ref[...], w_ref[0:WIDTH, :]) + _bdot(r_ref[...], w_ref[WIDTH:, :])
    x = x_ref[...] + m[2:3] * mix
    xo_ref[...] = x
    h = _rms_mod(x, g_ref[...], m[4:5], m[3:4])
    h_ref[...] = h.astype(h_ref.dtype)
    logits = _dot3(h, rh_ref[...], rl_ref[...]) + rb_ref[...]
    lane = lax.broadcasted_iota(jnp.int32, logits.shape, 1).astype(F32)
    is_group = (lane >= N_EXPERTS) & (lane < N_EXPERTS + N_GROUPS)
    glog = jnp.where(is_group, logits, -jnp.inf)
    gmax = jnp.max(glog, -1, keepdims=True)
    gsum = jnp.sum(jnp.where(is_group, jnp.exp(logits - gmax), 0.0), -1, keepdims=True)
    g_w = 1.0 / gsum
    g_idx = _lane_min_index(glog == gmax, lane) - N_EXPERTS
    in_group = (lane >= g_idx * EXPERTS_PER_GROUP) & (lane < (g_idx + 1) * EXPERTS_PER_GROUP)
    ev = jnp.where(in_group, logits, -jnp.inf)
    v1 = jnp.max(ev, -1, keepdims=True)
    i1 = _lane_min_index(ev == v1, lane)
    ev2 = jnp.where(lane == i1, -jnp.inf, ev)
    v2 = jnp.max(ev2, -1, keepdims=True)
    i2 = _lane_min_index(ev2 == v2, lane)
    e2 = jnp.exp(v2 - v1)
    w1 = 1.0 / (1.0 + e2)
    w2 = e2 / (1.0 + e2)
    gate_ref[...] = jnp.where(lane == i1, g_w * w1, jnp.where(lane == i2, g_w * w2, 0.0))


def _out_proj(left, right, w_out, x, mod, g, r_hi, r_lo, r_b):
    b, s, d = x.shape
    tm = 512
    tile = lambda w: pl.BlockSpec((None, tm, w), lambda i, t: (i, t, 0))
    const = lambda shp: pl.BlockSpec(shp, lambda i, t: tuple(0 for _ in shp))
    return pl.pallas_call(
        _out_kernel,
        grid=(b, s // tm),
        in_specs=[tile(WIDTH), tile(WIDTH), const((2 * WIDTH, d)), tile(d),
                  pl.BlockSpec((None, 6, d), lambda i, t: (i, 0, 0)),
                  const((1, d)), const((d, LANES)), const((d, LANES)), const((1, LANES))],
        out_specs=[tile(d), tile(d), tile(LANES)],
        out_shape=[jax.ShapeDtypeStruct((b, s, d), F32),
                   jax.ShapeDtypeStruct((b, s, d), BF16),
                   jax.ShapeDtypeStruct((b, s, LANES), F32)],
        compiler_params=_params(("arbitrary", "arbitrary")),
        name="out_proj_router",
    )(left, right, w_out, x, mod, g.reshape(1, d), r_hi, r_lo, r_b)


def _router_weights(w_group, b_group, w_router, b_router):
    d = w_group.shape[0]
    w = jnp.concatenate([jnp.transpose(w_router, (1, 0, 2)).reshape(d, N_EXPERTS), w_group], axis=1)
    w = jnp.pad(w.astype(F32), ((0, 0), (0, LANES - w.shape[1])))
    bias = jnp.concatenate([b_router.reshape(N_EXPERTS), b_group]).astype(F32)
    bias = jnp.pad(bias, (0, LANES - bias.shape[0])).reshape(1, LANES)
    hi, lo = _split(w)
    return hi, lo, bias


def _moe_kernel(h_ref, gate_ref, wg_ref, wu_ref, wd_ref, x_ref, mod_ref, fg_ref, o_ref, acc,
                *, final):
    e = pl.program_id(2)

    @pl.when(e == 0)
    def _():
        acc[...] = jnp.zeros_like(acc)

    h = h_ref[...]
    gates = gate_ref[...]
    lane = lax.broadcasted_iota(jnp.int32, gates.shape, 1)
    gate = jnp.sum(jnp.where(lane == e, gates, 0.0), -1, keepdims=True)
    a = _bdot(h, wg_ref[...])
    hid = (a * jax.nn.sigmoid(a)) * _bdot(h, wu_ref[...])
    acc[...] += gate * _bdot(hid.astype(BF16), wd_ref[...])

    @pl.when(e == pl.num_programs(2) - 1)
    def _():
        x = x_ref[...] + mod_ref[5:6, :] * acc[...]
        if final:
            x = x * lax.rsqrt(jnp.mean(x * x, -1, keepdims=True) + EPS) * fg_ref[...]
        o_ref[...] = x


def _moe(h, gates, w_gate, w_up, w_down, x, mod, final_g, final):
    b, s, d = x.shape
    ne, _, de = w_gate.shape
    tm = 512
    tile = lambda w: pl.BlockSpec((None, tm, w), lambda i, t, e: (i, t, 0))
    return pl.pallas_call(
        functools.partial(_moe_kernel, final=final),
        grid=(b, s // tm, ne),
        in_specs=[tile(d), tile(LANES),
                  pl.BlockSpec((None, d, de), lambda i, t, e: (e, 0, 0)),
                  pl.BlockSpec((None, d, de), lambda i, t, e: (e, 0, 0)),
                  pl.BlockSpec((None, de, d), lambda i, t, e: (e, 0, 0)),
                  tile(d),
                  pl.BlockSpec((None, 6, d), lambda i, t, e: (i, 0, 0)),
                  pl.BlockSpec((1, d), lambda i, t, e: (0, 0))],
        out_specs=tile(d),
        out_shape=jax.ShapeDtypeStruct((b, s, d), F32),
        scratch_shapes=[pltpu.VMEM((tm, d), F32)],
        compiler_params=_params(("arbitrary", "arbitrary", "arbitrary")),
        name="moe_experts",
    )(h, gates, w_gate, w_up, w_down, x, mod, final_g.reshape(1, d))


@jax.jit
def kernel(x, c, rel_bias, norm1_g, norm2_g, w_ada, b_ada, ev_w_in, ev_w_out, ev_gmlp_ln_g, ev_gmlp_ln_b, ev_w_s, ev_b_s, od_w_in, od_w_out, od_w_dw, od_b_dw, od_conv_ln_g, od_conv_ln_b, moe_w_group, moe_b_group, moe_w_router, moe_b_router, moe_w_gate, moe_w_up, moe_w_down, final_norm_g):
    b, s, d = x.shape
    depth = w_ada.shape[0]
    mods = _ada(c, w_ada, b_ada).reshape(depth, b, 6, d)
    biases = [_branch_bias(rel_bias, dil) for _, dil in DILATED_BRANCHES]
    for layer in range(depth):
        j = layer // 2
        mod = mods[layer]
        if layer % 2 == 0:
            proj = _in_proj(x, norm1_g[layer], mod, ev_w_in[j].astype(BF16))
            outs = [_dilated_branch(proj, bias, dil)
                    for bias, (_, dil) in zip(biases, DILATED_BRANCHES)]
            left = _combine_branches(outs)
            right = _gmlp(proj, ev_gmlp_ln_g[j], ev_gmlp_ln_b[j], ev_w_s[j], ev_b_s[j])
            w_out = ev_w_out[j]
        else:
            proj = _in_proj(x, norm1_g[layer], mod, od_w_in[j].astype(BF16))
            left = _conv_module(proj, od_w_dw[j], od_b_dw[j], od_conv_ln_g[j], od_conv_ln_b[j])
            right = _stick_breaking(proj)
            w_out = od_w_out[j]
        r_hi, r_lo, r_b = _router_weights(moe_w_group[layer], moe_b_group[layer],
                                          moe_w_router[layer], moe_b_router[layer])
        x, h, gates = _out_proj(left, right, w_out.astype(BF16), x, mod, norm2_g[layer],
                                r_hi, r_lo, r_b)
        x = _moe(h, gates, moe_w_gate[layer].astype(BF16), moe_w_up[layer].astype(BF16),
                 moe_w_down[layer].astype(BF16), x, mod, final_norm_g, layer == depth - 1)
    return x
```

```python
import functools
import math

import numpy as np
import jax
import jax.numpy as jnp
from jax import lax
from jax.experimental import pallas as pl
from jax.experimental.pallas import tpu as pltpu

F32 = jnp.float32
BF16 = jnp.bfloat16

HEAD_DIM = 64
HEADS = 8
WIDTH = HEADS * HEAD_DIM
PAIR = 2 * HEAD_DIM
BLK = 128
DILATED_BRANCHES = ((128, 1), (512, 4), (2048, 16))
N_BUCKETS = 32
MAX_DISTANCE = 2048
CONV_WIDTH = 31
CONV_HALO = 32
B_GROUPS = 4
N_GROUPS = 4
EXPERTS_PER_GROUP = 4
N_EXPERTS = N_GROUPS * EXPERTS_PER_GROUP
EPS = 1e-6
NEG = -1e30
LOG2E = 1.4426950408889634
LANES = 128
VMEM_LIMIT = 48 * 1024 * 1024


def _params(sem):
    return pltpu.CompilerParams(dimension_semantics=sem, vmem_limit_bytes=VMEM_LIMIT)


def _bdot(a, b):
    return jnp.dot(a, b, preferred_element_type=F32)


def _bdot_nt(a, b):
    return lax.dot_general(a, b, (((1,), (1,)), ((), ())), preferred_element_type=F32)


def _split(a):
    hi = a.astype(BF16)
    lo = (a - hi.astype(F32)).astype(BF16)
    return hi, lo


def _dot3(a, w_hi, w_lo):
    a_hi, a_lo = _split(a)
    return _bdot(a_hi, w_hi) + (_bdot(a_hi, w_lo) + _bdot(a_lo, w_hi))


def _ada_kernel(c_ref, w_ref, b_ref, o_ref):
    c = c_ref[...]
    sc = c * jax.nn.sigmoid(c)
    w_hi, w_lo = _split(w_ref[...])
    o_ref[...] = _dot3(sc, w_hi, w_lo) + b_ref[...]


def _ada(c, w_ada, b_ada):
    depth, d, n = w_ada.shape
    b = c.shape[0]
    tn = 1024
    return pl.pallas_call(
        _ada_kernel,
        grid=(depth, n // tn),
        in_specs=[
            pl.BlockSpec((b, d), lambda l, j: (0, 0)),
            pl.BlockSpec((None, d, tn), lambda l, j: (l, 0, j)),
            pl.BlockSpec((None, 1, tn), lambda l, j: (l, 0, j)),
        ],
        out_specs=pl.BlockSpec((None, b, tn), lambda l, j: (l, 0, j)),
        out_shape=jax.ShapeDtypeStruct((depth, b, n), F32),
        compiler_params=_params(("arbitrary", "arbitrary")),
        name="ada_mod",
    )(c, w_ada, b_ada.reshape(depth, 1, n))


def _rms_mod(x, g, scale_row, shift_row):
    y = x * lax.rsqrt(jnp.mean(x * x, -1, keepdims=True) + EPS) * g
    return y * (1.0 + scale_row) + shift_row


def _in_kernel(x_ref, g_ref, mod_ref, w_ref, o_ref):
    m = mod_ref[...]
    h = _rms_mod(x_ref[...], g_ref[...], m[1:2], m[0:1]).astype(BF16)
    n = w_ref.shape[1]
    for j in range(n // WIDTH):
        cols = slice(j * WIDTH, (j + 1) * WIDTH)
        o_ref[:, cols] = _bdot(h, w_ref[:, cols]).astype(o_ref.dtype)


def _in_proj(x, g, mod, w_in):
    b, s, d = x.shape
    n = w_in.shape[1]
    tm = 512
    return pl.pallas_call(
        _in_kernel,
        grid=(b, s // tm),
        in_specs=[
            pl.BlockSpec((None, tm, d), lambda i, t: (i, t, 0)),
            pl.BlockSpec((1, d), lambda i, t: (0, 0)),
            pl.BlockSpec((None, 6, d), lambda i, t: (i, 0, 0)),
            pl.BlockSpec((d, n), lambda i, t: (0, 0)),
        ],
        out_specs=pl.BlockSpec((None, tm, n), lambda i, t: (i, t, 0)),
        out_shape=jax.ShapeDtypeStruct((b, s, n), BF16),
        compiler_params=_params(("arbitrary", "arbitrary")),
        name="in_proj",
    )(x, g.reshape(1, d), mod, w_in)


def _t5_bucket(dist):
    max_exact = N_BUCKETS // 2
    d = np.maximum(dist, 1).astype(np.float32)
    large = max_exact + (np.log(d / max_exact) / np.log(MAX_DISTANCE / max_exact)
                         * (N_BUCKETS - max_exact)).astype(np.int32)
    large = np.minimum(large, N_BUCKETS - 1)
    return np.where(dist < max_exact, dist, large).astype(np.int32)


def _branch_bias(rel_table, dilation):
    qi = np.arange(BLK)[:, None]
    kj = np.arange(2 * BLK)[None, :]
    bucket = _t5_bucket(np.clip(qi + BLK - kj, 0, None) * dilation)
    return jnp.transpose(rel_table.astype(F32)[bucket], (2, 0, 1))


def _dil_kernel(q_ref, kp_ref, kc_ref, vp_ref, vc_ref, bias_ref, o_ref, m_ref, l_ref,
                kbuf, vbuf, *, tq):
    n = pl.program_id(2)
    nsub = tq // BLK
    kbuf[0:BLK] = kp_ref[...]
    kbuf[BLK:] = kc_ref[...]
    vbuf[0:BLK] = vp_ref[...]
    vbuf[BLK:] = vc_ref[...]
    qi = lax.broadcasted_iota(jnp.int32, (BLK, 2 * BLK), 0)
    kj = lax.broadcasted_iota(jnp.int32, (BLK, 2 * BLK), 1)
    dist = qi + BLK - kj
    in_window = (dist >= 0) & (dist <= BLK)
    lane = lax.broadcasted_iota(jnp.int32, (BLK, PAIR), 1)
    scale = HEAD_DIM ** -0.5
    for i in range(nsub):
        first_key = jnp.where(n * nsub + i == 0, BLK, 0)
        valid = in_window & (kj >= first_key)
        rows = slice(i * BLK, (i + 1) * BLK)
        krows = slice(i * BLK, (i + 2) * BLK)
        m_tile = jnp.zeros((BLK, LANES), F32)
        l_tile = jnp.ones((BLK, LANES), F32)
        for p in range(HEADS // 2):
            cols = slice(p * PAIR, (p + 1) * PAIR)
            q_pair = q_ref[rows, cols]
            k_pair = kbuf[krows, cols]
            v_pair = vbuf[krows, cols]
            outs = []
            for hh in range(2):
                h = 2 * p + hh
                head_lanes = (lane >= HEAD_DIM) if hh else (lane < HEAD_DIM)
                qm = jnp.where(head_lanes, q_pair, jnp.zeros_like(q_pair))
                s = _bdot_nt(qm, k_pair) * scale
                logit = jnp.where(valid, s + bias_ref[h], NEG)
                mx = jnp.max(logit, -1, keepdims=True)
                pr = jnp.exp(logit - mx)
                den = jnp.sum(pr, -1, keepdims=True)
                outs.append(_bdot(pr.astype(BF16), v_pair) / den)
                m_tile = jnp.where(lane == h, mx, m_tile)
                l_tile = jnp.where(lane == h, den, l_tile)
            o_ref[rows, cols] = jnp.where(lane < HEAD_DIM, outs[0], outs[1]).astype(o_ref.dtype)
        m_ref[rows, :] = m_tile
        l_ref[rows, :] = l_tile


def _dilated_branch(proj, bias, dilation):
    b, s, n = proj.shape
    sub = s // dilation
    tq = min(512, sub)
    nsub = tq // BLK
    ncol = n // WIDTH
    view = proj.reshape(b, sub, dilation * n)
    cur = lambda c: pl.BlockSpec((None, tq, WIDTH), lambda i, r, t: (i, t, r * ncol + c))
    prev = lambda c: pl.BlockSpec(
        (None, BLK, WIDTH), lambda i, r, t: (i, jnp.maximum(t * nsub - 1, 0), r * ncol + c))
    o, m, l = pl.pallas_call(
        functools.partial(_dil_kernel, tq=tq),
        grid=(b, dilation, sub // tq),
        in_specs=[cur(0), prev(1), cur(1), prev(2), cur(2),
                  pl.BlockSpec((HEADS, BLK, 2 * BLK), lambda i, r, t: (0, 0, 0))],
        out_specs=[
            pl.BlockSpec((None, tq, WIDTH), lambda i, r, t: (i, t, r)),
            pl.BlockSpec((None, tq, LANES), lambda i, r, t: (i, t, r)),
            pl.BlockSpec((None, tq, LANES), lambda i, r, t: (i, t, r)),
        ],
        out_shape=[
            jax.ShapeDtypeStruct((b, sub, dilation * WIDTH), BF16),
            jax.ShapeDtypeStruct((b, sub, dilation * LANES), F32),
            jax.ShapeDtypeStruct((b, sub, dilation * LANES), F32),
        ],
        scratch_shapes=[pltpu.VMEM((BLK + tq, WIDTH), BF16), pltpu.VMEM((BLK + tq, WIDTH), BF16)],
        compiler_params=_params(("arbitrary", "arbitrary", "arbitrary")),
        name=f"dilated_d{dilation}",
    )(view, view, view, view, view, bias)
    return (o.reshape(b, s, WIDTH), m.reshape(b, s, LANES), l.reshape(b, s, LANES))


def _comb_kernel(o1, o2, o3, m1, m2, m3, l1, l2, l3, e_ref, out_ref):
    ms = [m1[...], m2[...], m3[...]]
    m_all = jnp.maximum(jnp.maximum(ms[0], ms[1]), ms[2])
    es = [l[...] * jnp.exp(m - m_all) for l, m in zip((l1, l2, l3), ms)]
    tot = es[0] + es[1] + es[2]
    acc = None
    for e, o in zip(es, (o1, o2, o3)):
        w_hi, w_lo = _split(e / tot)
        w = _bdot(w_hi, e_ref[...]) + _bdot(w_lo, e_ref[...])
        term = w * o[...].astype(F32)
        acc = term if acc is None else acc + term
    out_ref[...] = acc.astype(out_ref.dtype)


def _combine_branches(outs):
    (o1, m1, l1), (o2, m2, l2), (o3, m3, l3) = outs
    b, s, _ = o1.shape
    tm = 512
    expand = np.zeros((LANES, WIDTH), np.float32)
    for h in range(HEADS):
        expand[h, h * HEAD_DIM:(h + 1) * HEAD_DIM] = 1.0
    wide = pl.BlockSpec((None, tm, WIDTH), lambda i, t: (i, t, 0))
    stat = pl.BlockSpec((None, tm, LANES), lambda i, t: (i, t, 0))
    return pl.pallas_call(
        _comb_kernel,
        grid=(b, s // tm),
        in_specs=[wide] * 3 + [stat] * 6 + [pl.BlockSpec((LANES, WIDTH), lambda i, t: (0, 0))],
        out_specs=wide,
        out_shape=jax.ShapeDtypeStruct((b, s, WIDTH), BF16),
        compiler_params=_params(("arbitrary", "arbitrary")),
        name="dilated_combine",
    )(o1, o2, o3, m1, m2, m3, l1, l2, l3, jnp.asarray(expand, BF16))


def _gelu(x):
    return 0.5 * x * (1.0 + jnp.tanh(math.sqrt(2.0 / math.pi) * (x + 0.044715 * (x * x * x))))


def _gmlp_kernel(u_ref, v_ref, g_ref, b_ref, ws_ref, bs_ref, o_ref):
    tm = u_ref.shape[0]
    u = _gelu(u_ref[...].astype(F32))
    v = _gelu(v_ref[...].astype(F32))
    mu = jnp.mean(v, -1, keepdims=True)
    vc = v - mu
    var = jnp.mean(vc * vc, -1, keepdims=True)
    v = (vc * lax.rsqrt(var + EPS) * g_ref[...] + b_ref[...]).astype(BF16)
    r = lax.broadcasted_iota(jnp.int32, (BLK, BLK), 0)
    c = lax.broadcasted_iota(jnp.int32, (BLK, BLK), 1)
    for g in range(B_GROUPS):
        w = jnp.where(c <= r, ws_ref[g], 0.0).astype(BF16)
        cols = slice(g * BLK, (g + 1) * BLK)
        for ch in range(tm // BLK):
            rows = slice(ch * BLK, (ch + 1) * BLK)
            mixed = _bdot(w, v[rows, cols]) + bs_ref[g]
            o_ref[rows, cols] = (u[rows, cols] * mixed).astype(o_ref.dtype)


def _gmlp(proj, ln_g, ln_b, w_s, b_s):
    b, s, n = proj.shape
    tm = 512
    return pl.pallas_call(
        _gmlp_kernel,
        grid=(b, s // tm),
        in_specs=[
            pl.BlockSpec((None, tm, WIDTH), lambda i, t: (i, t, 3)),
            pl.BlockSpec((None, tm, WIDTH), lambda i, t: (i, t, 4)),
            pl.BlockSpec((1, WIDTH), lambda i, t: (0, 0)),
            pl.BlockSpec((1, WIDTH), lambda i, t: (0, 0)),
            pl.BlockSpec((B_GROUPS, BLK, BLK), lambda i, t: (0, 0, 0)),
            pl.BlockSpec((B_GROUPS, BLK, 1), lambda i, t: (0, 0, 0)),
        ],
        out_specs=pl.BlockSpec((None, tm, WIDTH), lambda i, t: (i, t, 0)),
        out_shape=jax.ShapeDtypeStruct((b, s, WIDTH), BF16),
        compiler_params=_params(("arbitrary", "arbitrary")),
        name="gmlp_gate",
    )(proj, proj, ln_g.reshape(1, WIDTH), ln_b.reshape(1, WIDTH), w_s,
      b_s.reshape(B_GROUPS, BLK, 1))


def _conv_kernel(a_ref, g_ref, ah_ref, gh_ref, w_ref, bdw_ref, lng_ref, lnb_ref, o_ref, hbuf):
    tm = a_ref.shape[0]
    t = pl.program_id(1)
    halo = ah_ref[...].astype(F32) * jax.nn.sigmoid(gh_ref[...].astype(F32))
    hbuf[0:CONV_HALO] = jnp.where(t == 0, 0.0, halo)
    hbuf[CONV_HALO:] = a_ref[...].astype(F32) * jax.nn.sigmoid(g_ref[...].astype(F32))
    rc = 32
    off = CONV_HALO - (CONV_WIDTH - 1)

    def chunk(ci, carry):
        base = pl.multiple_of(ci * rc, rc)
        win = hbuf[pl.ds(base, rc + CONV_HALO), :]
        acc = jnp.zeros((rc, WIDTH), F32) + bdw_ref[...]
        for k in range(CONV_WIDTH):
            acc = acc + w_ref[k:k + 1, :] * win[off + k:off + k + rc, :]
        mu = jnp.mean(acc, -1, keepdims=True)
        xc = acc - mu
        var = jnp.mean(xc * xc, -1, keepdims=True)
        y = xc * lax.rsqrt(var + EPS) * lng_ref[...] + lnb_ref[...]
        o_ref[pl.ds(base, rc), :] = (y * jax.nn.sigmoid(y)).astype(o_ref.dtype)
        return carry

    lax.fori_loop(0, tm // rc, chunk, 0)


def _conv_module(proj, w_dw, b_dw, ln_g, ln_b):
    b, s, n = proj.shape
    tm = 512
    hb = tm // CONV_HALO
    row = lambda v: v.reshape(1, WIDTH)
    halo = lambda c: pl.BlockSpec(
        (None, CONV_HALO, WIDTH), lambda i, t: (i, jnp.maximum(t * hb - 1, 0), c))
    return pl.pallas_call(
        _conv_kernel,
        grid=(b, s // tm),
        in_specs=[
            pl.BlockSpec((None, tm, WIDTH), lambda i, t: (i, t, 0)),
            pl.BlockSpec((None, tm, WIDTH), lambda i, t: (i, t, 1)),
            halo(0), halo(1),
            pl.BlockSpec((CONV_WIDTH, WIDTH), lambda i, t: (0, 0)),
            pl.BlockSpec((1, WIDTH), lambda i, t: (0, 0)),
            pl.BlockSpec((1, WIDTH), lambda i, t: (0, 0)),
            pl.BlockSpec((1, WIDTH), lambda i, t: (0, 0)),
        ],
        out_specs=pl.BlockSpec((None, tm, WIDTH), lambda i, t: (i, t, 0)),
        out_shape=jax.ShapeDtypeStruct((b, s, WIDTH), BF16),
        scratch_shapes=[pltpu.VMEM((CONV_HALO + tm, WIDTH), F32)],
        compiler_params=_params(("arbitrary", "arbitrary")),
        name="conv_module",
    )(proj, proj, proj, proj, w_dw, row(b_dw), row(ln_g), row(ln_b))


def _stick_kernel(q_ref, k_ref, v_ref, mm_ref, o_ref, q2_ref, carry_ref, acc_ref):
    i = pl.program_id(1)
    npair = WIDTH // PAIR
    lane = lax.broadcasted_iota(jnp.int32, (BLK, PAIR), 1)
    lo_lanes = lane < HEAD_DIM
    row2 = lax.broadcasted_iota(jnp.int32, (2 * BLK, BLK), 0)
    col2 = lax.broadcasted_iota(jnp.int32, (2 * BLK, BLK), 1)
    causal2 = col2 < (row2 & (BLK - 1))
    scale = HEAD_DIM ** -0.5
    for p in range(npair):
        qp = q_ref[:, p * PAIR:(p + 1) * PAIR] * scale
        zero = jnp.zeros_like(qp)
        q2_ref[p] = jnp.concatenate(
            [jnp.where(lo_lanes, qp, zero), jnp.where(lo_lanes, zero, qp)], axis=0)
    carry_ref[...] = jnp.zeros_like(carry_ref)
    acc_ref[...] = jnp.zeros_like(acc_ref)

    def block(j, diag):
        start = pl.multiple_of(j * BLK, BLK)
        cols = lambda p: slice(p * PAIR, (p + 1) * PAIR)

        def scores(p):
            return _bdot_nt(q2_ref[p], k_ref[pl.ds(start, BLK), cols(p)])

        def keep_sums(z):
            soft = jnp.log(1.0 + jnp.exp2(jnp.abs(z) * -LOG2E))
            log_beta = jnp.minimum(z, 0.0) - soft
            log_keep = log_beta - z
            if diag:
                log_keep = jnp.where(causal2, log_keep, 0.0)
            return log_beta, _bdot(log_keep.astype(BF16), mm_ref[...])

        def weighted_values(p, log_beta, sums):
            carry = carry_ref[p]
            att = jnp.exp(log_beta + (sums[:, :BLK] + carry))
            if diag:
                att = jnp.where(causal2, att, 0.0)
            carry_ref[p] = carry + sums[:, BLK:]
            att = att.astype(BF16)
            vb = v_ref[pl.ds(start, BLK), cols(p)]
            zero = jnp.zeros_like(vb)
            v2 = jnp.concatenate([jnp.where(lo_lanes, vb, zero), jnp.where(lo_lanes, zero, vb)], axis=0)
            return _bdot(jnp.concatenate([att[:BLK], att[BLK:]], axis=1), v2)

        z, ks, pv = {}, {}, {}
        for step in range(npair + 3):
            if step < npair:
                z[step] = scores(step)
            if 0 <= step - 1 < npair:
                ks[step - 1] = keep_sums(z.pop(step - 1))
            if 0 <= step - 2 < npair:
                pv[step - 2] = weighted_values(step - 2, *ks.pop(step - 2))
            if 0 <= step - 3 < npair:
                acc_ref[step - 3] += pv.pop(step - 3)

    block(i, True)

    def body(t, c):
        block(i - 1 - t, False)
        return c

    lax.fori_loop(0, i, body, 0)
    for p in range(npair):
        o_ref[:, p * PAIR:(p + 1) * PAIR] = acc_ref[p].astype(o_ref.dtype)


def _suffix_sum_matrix():
    sp = np.arange(BLK)[:, None]
    sc = np.arange(BLK)[None, :]
    return jnp.asarray(
        np.concatenate([(sp > sc).astype(np.float32), np.ones((BLK, BLK), np.float32)], axis=1), BF16)


def _stick_breaking(proj):
    b, s, n = proj.shape
    npair = WIDTH // PAIR
    return pl.pallas_call(
        _stick_kernel,
        grid=(b, s // BLK),
        in_specs=[
            pl.BlockSpec((None, BLK, WIDTH), lambda i, t: (i, t, 2)),
            pl.BlockSpec((None, s, WIDTH), lambda i, t: (i, 0, 3)),
            pl.BlockSpec((None, s, WIDTH), lambda i, t: (i, 0, 4)),
            pl.BlockSpec((BLK, 2 * BLK), lambda i, t: (0, 0)),
        ],
        out_specs=pl.BlockSpec((None, BLK, WIDTH), lambda i, t: (i, t, 0)),
        out_shape=jax.ShapeDtypeStruct((b, s, WIDTH), BF16),
        scratch_shapes=[pltpu.VMEM((npair, 2 * BLK, PAIR), BF16),
                        pltpu.VMEM((npair, 2 * BLK, BLK), F32),
                        pltpu.VMEM((npair, BLK, PAIR), F32)],
        compiler_params=_params(("arbitrary", "arbitrary")),
        name="stick_breaking",
    )(proj, proj, proj, _suffix_sum_matrix())


def _lane_min_index(mask, lane):
    return jnp.min(jnp.where(mask, lane, float(LANES)), -1, keepdims=True)


def _out_kernel(l_ref, r_ref, w_ref, x_ref, mod_ref, g_ref, rh_ref, rl_ref, rb_ref,
                xo_ref, h_ref, gate_ref):
    m = mod_ref[...]
    mix = _bdot(l_ref[...], w_ref[0:WIDTH, :]) + _bdot(r_ref[...], w_ref[WIDTH:, :])
    x = x_ref[...] + m[2:3] * mix
    xo_ref[...] = x
    h = _rms_mod(x, g_ref[...], m[4:5], m[3:4])
    h_ref[...] = h.astype(h_ref.dtype)
    logits = _dot3(h, rh_ref[...], rl_ref[...]) + rb_ref[...]
    lane = lax.broadcasted_iota(jnp.int32, logits.shape, 1).astype(F32)
    is_group = (lane >= N_EXPERTS) & (lane < N_EXPERTS + N_GROUPS)
    glog = jnp.where(is_group, logits, -jnp.inf)
    gmax = jnp.max(glog, -1, keepdims=True)
    gsum = jnp.sum(jnp.where(is_group, jnp.exp(logits - gmax), 0.0), -1, keepdims=True)
    g_w = 1.0 / gsum
    g_idx = _lane_min_index(glog == gmax, lane) - N_EXPERTS
    in_group = (lane >= g_idx * EXPERTS_PER_GROUP) & (lane < (g_idx + 1) * EXPERTS_PER_GROUP)
    ev = jnp.where(in_group, logits, -jnp.inf)
    v1 = jnp.max(ev, -1, keepdims=True)
    i1 = _lane_min_index(ev == v1, lane)
    ev2 = jnp.where(lane == i1, -jnp.inf, ev)
    v2 = jnp.max(ev2, -1, keepdims=True)
    i2 = _lane_min_index(ev2 == v2, lane)
    e2 = jnp.exp(v2 - v1)
    w1 = 1.0 / (1.0 + e2)
    w2 = e2 / (1.0 + e2)
    gate_ref[...] = jnp.where(lane == i1, g_w * w1, jnp.where(lane == i2, g_w * w2, 0.0))


def _out_proj(left, right, w_out, x, mod, g, r_hi, r_lo, r_b):
    b, s, d = x.shape
    tm = 512
    tile = lambda w: pl.BlockSpec((None, tm, w), lambda i, t: (i, t, 0))
    const = lambda shp: pl.BlockSpec(shp, lambda i, t: tuple(0 for _ in shp))
    return pl.pallas_call(
        _out_kernel,
        grid=(b, s // tm),
        in_specs=[tile(WIDTH), tile(WIDTH), const((2 * WIDTH, d)), tile(d),
                  pl.BlockSpec((None, 6, d), lambda i, t: (i, 0, 0)),
                  const((1, d)), const((d, LANES)), const((d, LANES)), const((1, LANES))],
        out_specs=[tile(d), tile(d), tile(LANES)],
        out_shape=[jax.ShapeDtypeStruct((b, s, d), F32),
                   jax.ShapeDtypeStruct((b, s, d), BF16),
                   jax.ShapeDtypeStruct((b, s, LANES), F32)],
        compiler_params=_params(("arbitrary", "arbitrary")),
        name="out_proj_router",
    )(left, right, w_out, x, mod, g.reshape(1, d), r_hi, r_lo, r_b)


def _router_weights(w_group, b_group, w_router, b_router):
    d = w_group.shape[0]
    w = jnp.concatenate([jnp.transpose(w_router, (1, 0, 2)).reshape(d, N_EXPERTS), w_group], axis=1)
    w = jnp.pad(w.astype(F32), ((0, 0), (0, LANES - w.shape[1])))
    bias = jnp.concatenate([b_router.reshape(N_EXPERTS), b_group]).astype(F32)
    bias = jnp.pad(bias, (0, LANES - bias.shape[0])).reshape(1, LANES)
    hi, lo = _split(w)
    return hi, lo, bias


def _moe_kernel(h_ref, gate_ref, wg_ref, wu_ref, wd_ref, x_ref, mod_ref, fg_ref, o_ref, acc,
                *, final):
    e = pl.program_id(2)

    @pl.when(e == 0)
    def _():
        acc[...] = jnp.zeros_like(acc)

    h = h_ref[...]
    gates = gate_ref[...]
    lane = lax.broadcasted_iota(jnp.int32, gates.shape, 1)
    gate = jnp.sum(jnp.where(lane == e, gates, 0.0), -1, keepdims=True)
    a = _bdot(h, wg_ref[...])
    hid = (a * jax.nn.sigmoid(a)) * _bdot(h, wu_ref[...])
    acc[...] += gate * _bdot(hid.astype(BF16), wd_ref[...])

    @pl.when(e == pl.num_programs(2) - 1)
    def _():
        x = x_ref[...] + mod_ref[5:6, :] * acc[...]
        if final:
            x = x * lax.rsqrt(jnp.mean(x * x, -1, keepdims=True) + EPS) * fg_ref[...]
        o_ref[...] = x


def _moe(h, gates, w_gate, w_up, w_down, x, mod, final_g, final):
    b, s, d = x.shape
    ne, _, de = w_gate.shape
    tm = 512
    tile = lambda w: pl.BlockSpec((None, tm, w), lambda i, t, e: (i, t, 0))
    return pl.pallas_call(
        functools.partial(_moe_kernel, final=final),
        grid=(b, s // tm, ne),
        in_specs=[tile(d), tile(LANES),
                  pl.BlockSpec((None, d, de), lambda i, t, e: (e, 0, 0)),
                  pl.BlockSpec((None, d, de), lambda i, t, e: (e, 0, 0)),
                  pl.BlockSpec((None, de, d), lambda i, t, e: (e, 0, 0)),
                  tile(d),
                  pl.BlockSpec((None, 6, d), lambda i, t, e: (i, 0, 0)),
                  pl.BlockSpec((1, d), lambda i, t, e: (0, 0))],
        out_specs=tile(d),
        out_shape=jax.ShapeDtypeStruct((b, s, d), F32),
        scratch_shapes=[pltpu.VMEM((tm, d), F32)],
        compiler_params=_params(("arbitrary", "arbitrary", "arbitrary")),
        name="moe_experts",
    )(h, gates, w_gate, w_up, w_down, x, mod, final_g.reshape(1, d))


@jax.jit
def kernel(x, c, rel_bias, norm1_g, norm2_g, w_ada, b_ada, ev_w_in, ev_w_out, ev_gmlp_ln_g, ev_gmlp_ln_b, ev_w_s, ev_b_s, od_w_in, od_w_out, od_w_dw, od_b_dw, od_conv_ln_g, od_conv_ln_b, moe_w_group, moe_b_group, moe_w_router, moe_b_router, moe_w_gate, moe_w_up, moe_w_down, final_norm_g):
    b, s, d = x.shape
    depth = w_ada.shape[0]
    mods = _ada(c, w_ada, b_ada).reshape(depth, b, 6, d)
    biases = [_branch_bias(rel_bias, dil) for _, dil in DILATED_BRANCHES]
    for layer in range(depth):
        j = layer // 2
        mod = mods[layer]
        if layer % 2 == 0:
            proj = _in_proj(x, norm1_g[layer], mod, ev_w_in[j].astype(BF16))
            outs = [_dilated_branch(proj, bias, dil)
                    for bias, (_, dil) in zip(biases, DILATED_BRANCHES)]
            left = _combine_branches(outs)
            right = _gmlp(proj, ev_gmlp_ln_g[j], ev_gmlp_ln_b[j], ev_w_s[j], ev_b_s[j])
            w_out = ev_w_out[j]
        else:
            proj = _in_proj(x, norm1_g[layer], mod, od_w_in[j].astype(BF16))
            left = _conv_module(proj, od_w_dw[j], od_b_dw[j], od_conv_ln_g[j], od_conv_ln_b[j])
            right = _stick_breaking(proj)
            w_out = od_w_out[j]
        r_hi, r_lo, r_b = _router_weights(moe_w_group[layer], moe_b_group[layer],
                                          moe_w_router[layer], moe_b_router[layer])
        x, h, gates = _out_proj(left, right, w_out.astype(BF16), x, mod, norm2_g[layer],
                                r_hi, r_lo, r_b)
        x = _moe(h, gates, moe_w_gate[layer].astype(BF16), moe_w_up[layer].astype(BF16),
                 moe_w_down[layer].astype(BF16), x, mod, final_norm_g, layer == depth - 1)
    return x
```

```python
import functools
import math

import numpy as np
import jax
import jax.numpy as jnp
from jax import lax
from jax.experimental import pallas as pl
from jax.experimental.pallas import tpu as pltpu

F32 = jnp.float32
BF16 = jnp.bfloat16

HEAD_DIM = 64
HEADS = 8
WIDTH = HEADS * HEAD_DIM
PAIR = 2 * HEAD_DIM
BLK = 128
DILATED_BRANCHES = ((128, 1), (512, 4), (2048, 16))
N_BUCKETS = 32
MAX_DISTANCE = 2048
CONV_WIDTH = 31
CONV_HALO = 32
B_GROUPS = 4
N_GROUPS = 4
EXPERTS_PER_GROUP = 4
N_EXPERTS = N_GROUPS * EXPERTS_PER_GROUP
PAIRS_PER_GROUP = EXPERTS_PER_GROUP * (EXPERTS_PER_GROUP - 1) // 2
N_CLASSES = N_GROUPS * PAIRS_PER_GROUP
TOKEN_TILE = 512
SLOT_TILE = 512
EPS = 1e-6
NEG = -1e30
LOG2E = 1.4426950408889634
LANES = 128
VMEM_LIMIT = 48 * 1024 * 1024


def _params(sem):
    return pltpu.CompilerParams(dimension_semantics=sem, vmem_limit_bytes=VMEM_LIMIT)


def _bdot(a, b):
    return jnp.dot(a, b, preferred_element_type=F32)


def _bdot_nt(a, b):
    return lax.dot_general(a, b, (((1,), (1,)), ((), ())), preferred_element_type=F32)


def _split(a):
    hi = a.astype(BF16)
    lo = (a - hi.astype(F32)).astype(BF16)
    return hi, lo


def _dot3(a, w_hi, w_lo):
    a_hi, a_lo = _split(a)
    return _bdot(a_hi, w_hi) + (_bdot(a_hi, w_lo) + _bdot(a_lo, w_hi))


def _ada_kernel(c_ref, w_ref, b_ref, o_ref):
    c = c_ref[...]
    sc = c * jax.nn.sigmoid(c)
    w_hi, w_lo = _split(w_ref[...])
    o_ref[...] = _dot3(sc, w_hi, w_lo) + b_ref[...]


def _ada(c, w_ada, b_ada):
    depth, d, n = w_ada.shape
    b = c.shape[0]
    tn = 1024
    return pl.pallas_call(
        _ada_kernel,
        grid=(depth, n // tn),
        in_specs=[
            pl.BlockSpec((b, d), lambda l, j: (0, 0)),
            pl.BlockSpec((None, d, tn), lambda l, j: (l, 0, j)),
            pl.BlockSpec((None, 1, tn), lambda l, j: (l, 0, j)),
        ],
        out_specs=pl.BlockSpec((None, b, tn), lambda l, j: (l, 0, j)),
        out_shape=jax.ShapeDtypeStruct((depth, b, n), F32),
        compiler_params=_params(("arbitrary", "arbitrary")),
        name="ada_mod",
    )(c, w_ada, b_ada.reshape(depth, 1, n))


def _rms_mod(x, g, scale_row, shift_row):
    y = x * lax.rsqrt(jnp.mean(x * x, -1, keepdims=True) + EPS) * g
    return y * (1.0 + scale_row) + shift_row


def _in_kernel(x_ref, g_ref, mod_ref, w_ref, o_ref):
    m = mod_ref[...]
    h = _rms_mod(x_ref[...], g_ref[...], m[1:2], m[0:1]).astype(BF16)
    n = w_ref.shape[1]
    for j in range(n // WIDTH):
        cols = slice(j * WIDTH, (j + 1) * WIDTH)
        o_ref[:, cols] = _bdot(h, w_ref[:, cols]).astype(o_ref.dtype)


def _in_proj(x, g, mod, w_in):
    b, s, d = x.shape
    n = w_in.shape[1]
    tm = 512
    return pl.pallas_call(
        _in_kernel,
        grid=(b, s // tm),
        in_specs=[
            pl.BlockSpec((None, tm, d), lambda i, t: (i, t, 0)),
            pl.BlockSpec((1, d), lambda i, t: (0, 0)),
            pl.BlockSpec((None, 6, d), lambda i, t: (i, 0, 0)),
            pl.BlockSpec((d, n), lambda i, t: (0, 0)),
        ],
        out_specs=pl.BlockSpec((None, tm, n), lambda i, t: (i, t, 0)),
        out_shape=jax.ShapeDtypeStruct((b, s, n), BF16),
        compiler_params=_params(("arbitrary", "arbitrary")),
        name="in_proj",
    )(x, g.reshape(1, d), mod, w_in)


def _t5_bucket(dist):
    max_exact = N_BUCKETS // 2
    d = np.maximum(dist, 1).astype(np.float32)
    large = max_exact + (np.log(d / max_exact) / np.log(MAX_DISTANCE / max_exact)
                         * (N_BUCKETS - max_exact)).astype(np.int32)
    large = np.minimum(large, N_BUCKETS - 1)
    return np.where(dist < max_exact, dist, large).astype(np.int32)


def _branch_bias(rel_table, dilation):
    qi = np.arange(BLK)[:, None]
    kj = np.arange(2 * BLK)[None, :]
    bucket = _t5_bucket(np.clip(qi + BLK - kj, 0, None) * dilation)
    return jnp.transpose(rel_table.astype(F32)[bucket], (2, 0, 1))


def _dil_kernel(q_ref, kp_ref, kc_ref, vp_ref, vc_ref, bias_ref, o_ref, m_ref, l_ref,
                kbuf, vbuf, *, tq):
    n = pl.program_id(2)
    nsub = tq // BLK
    kbuf[0:BLK] = kp_ref[...]
    kbuf[BLK:] = kc_ref[...]
    vbuf[0:BLK] = vp_ref[...]
    vbuf[BLK:] = vc_ref[...]
    qi = lax.broadcasted_iota(jnp.int32, (BLK, 2 * BLK), 0)
    kj = lax.broadcasted_iota(jnp.int32, (BLK, 2 * BLK), 1)
    dist = qi + BLK - kj
    in_window = (dist >= 0) & (dist <= BLK)
    lane = lax.broadcasted_iota(jnp.int32, (BLK, PAIR), 1)
    scale = HEAD_DIM ** -0.5
    for i in range(nsub):
        first_key = jnp.where(n * nsub + i == 0, BLK, 0)
        valid = in_window & (kj >= first_key)
        rows = slice(i * BLK, (i + 1) * BLK)
        krows = slice(i * BLK, (i + 2) * BLK)
        m_tile = jnp.zeros((BLK, LANES), F32)
        l_tile = jnp.ones((BLK, LANES), F32)
        for p in range(HEADS // 2):
            cols = slice(p * PAIR, (p + 1) * PAIR)
            q_pair = q_ref[rows, cols]
            k_pair = kbuf[krows, cols]
            v_pair = vbuf[krows, cols]
            outs = []
            for hh in range(2):
                h = 2 * p + hh
                head_lanes = (lane >= HEAD_DIM) if hh else (lane < HEAD_DIM)
                qm = jnp.where(head_lanes, q_pair, jnp.zeros_like(q_pair))
                s = _bdot_nt(qm, k_pair) * scale
                logit = jnp.where(valid, s + bias_ref[h], NEG)
                mx = jnp.max(logit, -1, keepdims=True)
                pr = jnp.exp(logit - mx)
                den = jnp.sum(pr, -1, keepdims=True)
                outs.append(_bdot(pr.astype(BF16), v_pair) / den)
                m_tile = jnp.where(lane == h, mx, m_tile)
                l_tile = jnp.where(lane == h, den, l_tile)
            o_ref[rows, cols] = jnp.where(lane < HEAD_DIM, outs[0], outs[1]).astype(o_ref.dtype)
        m_ref[rows, :] = m_tile
        l_ref[rows, :] = l_tile


def _dilated_branch(proj, bias, dilation):
    b, s, n = proj.shape
    sub = s // dilation
    tq = min(512, sub)
    nsub = tq // BLK
    ncol = n // WIDTH
    view = proj.reshape(b, sub, dilation * n)
    cur = lambda c: pl.BlockSpec((None, tq, WIDTH), lambda i, r, t: (i, t, r * ncol + c))
    prev = lambda c: pl.BlockSpec(
        (None, BLK, WIDTH), lambda i, r, t: (i, jnp.maximum(t * nsub - 1, 0), r * ncol + c))
    o, m, l = pl.pallas_call(
        functools.partial(_dil_kernel, tq=tq),
        grid=(b, dilation, sub // tq),
        in_specs=[cur(0), prev(1), cur(1), prev(2), cur(2),
                  pl.BlockSpec((HEADS, BLK, 2 * BLK), lambda i, r, t: (0, 0, 0))],
        out_specs=[
            pl.BlockSpec((None, tq, WIDTH), lambda i, r, t: (i, t, r)),
            pl.BlockSpec((None, tq, LANES), lambda i, r, t: (i, t, r)),
            pl.BlockSpec((None, tq, LANES), lambda i, r, t: (i, t, r)),
        ],
        out_shape=[
            jax.ShapeDtypeStruct((b, sub, dilation * WIDTH), BF16),
            jax.ShapeDtypeStruct((b, sub, dilation * LANES), F32),
            jax.ShapeDtypeStruct((b, sub, dilation * LANES), F32),
        ],
        scratch_shapes=[pltpu.VMEM((BLK + tq, WIDTH), BF16), pltpu.VMEM((BLK + tq, WIDTH), BF16)],
        compiler_params=_params(("arbitrary", "arbitrary", "arbitrary")),
        name=f"dilated_d{dilation}",
    )(view, view, view, view, view, bias)
    return (o.reshape(b, s, WIDTH), m.reshape(b, s, LANES), l.reshape(b, s, LANES))


def _comb_kernel(o1, o2, o3, m1, m2, m3, l1, l2, l3, e_ref, out_ref):
    ms = [m1[...], m2[...], m3[...]]
    m_all = jnp.maximum(jnp.maximum(ms[0], ms[1]), ms[2])
    es = [l[...] * jnp.exp(m - m_all) for l, m in zip((l1, l2, l3), ms)]
    tot = es[0] + es[1] + es[2]
    acc = None
    for e, o in zip(es, (o1, o2, o3)):
        w_hi, w_lo = _split(e / tot)
        w = _bdot(w_hi, e_ref[...]) + _bdot(w_lo, e_ref[...])
        term = w * o[...].astype(F32)
        acc = term if acc is None else acc + term
    out_ref[...] = acc.astype(out_ref.dtype)


def _combine_branches(outs):
    (o1, m1, l1), (o2, m2, l2), (o3, m3, l3) = outs
    b, s, _ = o1.shape
    tm = 512
    expand = np.zeros((LANES, WIDTH), np.float32)
    for h in range(HEADS):
        expand[h, h * HEAD_DIM:(h + 1) * HEAD_DIM] = 1.0
    wide = pl.BlockSpec((None, tm, WIDTH), lambda i, t: (i, t, 0))
    stat = pl.BlockSpec((None, tm, LANES), lambda i, t: (i, t, 0))
    return pl.pallas_call(
        _comb_kernel,
        grid=(b, s // tm),
        in_specs=[wide] * 3 + [stat] * 6 + [pl.BlockSpec((LANES, WIDTH), lambda i, t: (0, 0))],
        out_specs=wide,
        out_shape=jax.ShapeDtypeStruct((b, s, WIDTH), BF16),
        compiler_params=_params(("arbitrary", "arbitrary")),
        name="dilated_combine",
    )(o1, o2, o3, m1, m2, m3, l1, l2, l3, jnp.asarray(expand, BF16))


def _gelu(x):
    return 0.5 * x * (1.0 + jnp.tanh(math.sqrt(2.0 / math.pi) * (x + 0.044715 * (x * x * x))))


def _gmlp_kernel(u_ref, v_ref, g_ref, b_ref, ws_ref, bs_ref, o_ref):
    tm = u_ref.shape[0]
    u = _gelu(u_ref[...].astype(F32))
    v = _gelu(v_ref[...].astype(F32))
    mu = jnp.mean(v, -1, keepdims=True)
    vc = v - mu
    var = jnp.mean(vc * vc, -1, keepdims=True)
    v = (vc * lax.rsqrt(var + EPS) * g_ref[...] + b_ref[...]).astype(BF16)
    r = lax.broadcasted_iota(jnp.int32, (BLK, BLK), 0)
    c = lax.broadcasted_iota(jnp.int32, (BLK, BLK), 1)
    for g in range(B_GROUPS):
        w = jnp.where(c <= r, ws_ref[g], 0.0).astype(BF16)
        cols = slice(g * BLK, (g + 1) * BLK)
        for ch in range(tm // BLK):
            rows = slice(ch * BLK, (ch + 1) * BLK)
            mixed = _bdot(w, v[rows, cols]) + bs_ref[g]
            o_ref[rows, cols] = (u[rows, cols] * mixed).astype(o_ref.dtype)


def _gmlp(proj, ln_g, ln_b, w_s, b_s):
    b, s, n = proj.shape
    tm = 512
    return pl.pallas_call(
        _gmlp_kernel,
        grid=(b, s // tm),
        in_specs=[
            pl.BlockSpec((None, tm, WIDTH), lambda i, t: (i, t, 3)),
            pl.BlockSpec((None, tm, WIDTH), lambda i, t: (i, t, 4)),
            pl.BlockSpec((1, WIDTH), lambda i, t: (0, 0)),
            pl.BlockSpec((1, WIDTH), lambda i, t: (0, 0)),
            pl.BlockSpec((B_GROUPS, BLK, BLK), lambda i, t: (0, 0, 0)),
            pl.BlockSpec((B_GROUPS, BLK, 1), lambda i, t: (0, 0, 0)),
        ],
        out_specs=pl.BlockSpec((None, tm, WIDTH), lambda i, t: (i, t, 0)),
        out_shape=jax.ShapeDtypeStruct((b, s, WIDTH), BF16),
        compiler_params=_params(("arbitrary", "arbitrary")),
        name="gmlp_gate",
    )(proj, proj, ln_g.reshape(1, WIDTH), ln_b.reshape(1, WIDTH), w_s,
      b_s.reshape(B_GROUPS, BLK, 1))


def _conv_kernel(a_ref, g_ref, ah_ref, gh_ref, w_ref, bdw_ref, lng_ref, lnb_ref, o_ref, hbuf):
    tm = a_ref.shape[0]
    t = pl.program_id(1)
    halo = ah_ref[...].astype(F32) * jax.nn.sigmoid(gh_ref[...].astype(F32))
    hbuf[0:CONV_HALO] = jnp.where(t == 0, 0.0, halo)
    hbuf[CONV_HALO:] = a_ref[...].astype(F32) * jax.nn.sigmoid(g_ref[...].astype(F32))
    rc = 32
    off = CONV_HALO - (CONV_WIDTH - 1)

    def chunk(ci, carry):
        base = pl.multiple_of(ci * rc, rc)
        win = hbuf[pl.ds(base, rc + CONV_HALO), :]
        acc = jnp.zeros((rc, WIDTH), F32) + bdw_ref[...]
        for k in range(CONV_WIDTH):
            acc = acc + w_ref[k:k + 1, :] * win[off + k:off + k + rc, :]
        mu = jnp.mean(acc, -1, keepdims=True)
        xc = acc - mu
        var = jnp.mean(xc * xc, -1, keepdims=True)
        y = xc * lax.rsqrt(var + EPS) * lng_ref[...] + lnb_ref[...]
        o_ref[pl.ds(base, rc), :] = (y * jax.nn.sigmoid(y)).astype(o_ref.dtype)
        return carry

    lax.fori_loop(0, tm // rc, chunk, 0)


def _conv_module(proj, w_dw, b_dw, ln_g, ln_b):
    b, s, n = proj.shape
    tm = 512
    hb = tm // CONV_HALO
    row = lambda v: v.reshape(1, WIDTH)
    halo = lambda c: pl.BlockSpec(
        (None, CONV_HALO, WIDTH), lambda i, t: (i, jnp.maximum(t * hb - 1, 0), c))
    return pl.pallas_call(
        _conv_kernel,
        grid=(b, s // tm),
        in_specs=[
            pl.BlockSpec((None, tm, WIDTH), lambda i, t: (i, t, 0)),
            pl.BlockSpec((None, tm, WIDTH), lambda i, t: (i, t, 1)),
            halo(0), halo(1),
            pl.BlockSpec((CONV_WIDTH, WIDTH), lambda i, t: (0, 0)),
            pl.BlockSpec((1, WIDTH), lambda i, t: (0, 0)),
            pl.BlockSpec((1, WIDTH), lambda i, t: (0, 0)),
            pl.BlockSpec((1, WIDTH), lambda i, t: (0, 0)),
        ],
        out_specs=pl.BlockSpec((None, tm, WIDTH), lambda i, t: (i, t, 0)),
        out_shape=jax.ShapeDtypeStruct((b, s, WIDTH), BF16),
        scratch_shapes=[pltpu.VMEM((CONV_HALO + tm, WIDTH), F32)],
        compiler_params=_params(("arbitrary", "arbitrary")),
        name="conv_module",
    )(proj, proj, proj, proj, w_dw, row(b_dw), row(ln_g), row(ln_b))


def _stick_kernel(q_ref, k_ref, v_ref, mm_ref, o_ref, q2_ref, carry_ref, acc_ref):
    i = pl.program_id(1)
    npair = WIDTH // PAIR
    lane = lax.broadcasted_iota(jnp.int32, (BLK, PAIR), 1)
    lo_lanes = lane < HEAD_DIM
    row2 = lax.broadcasted_iota(jnp.int32, (2 * BLK, BLK), 0)
    col2 = lax.broadcasted_iota(jnp.int32, (2 * BLK, BLK), 1)
    causal2 = col2 < (row2 & (BLK - 1))
    scale = HEAD_DIM ** -0.5
    for p in range(npair):
        qp = q_ref[:, p * PAIR:(p + 1) * PAIR] * scale
        zero = jnp.zeros_like(qp)
        q2_ref[p] = jnp.concatenate(
            [jnp.where(lo_lanes, qp, zero), jnp.where(lo_lanes, zero, qp)], axis=0)
    carry_ref[...] = jnp.zeros_like(carry_ref)
    acc_ref[...] = jnp.zeros_like(acc_ref)

    def blocks(js, diag):
        cols = lambda p: slice(p * PAIR, (p + 1) * PAIR)
        rows = lambda j: pl.ds(pl.multiple_of(j * BLK, BLK), BLK)

        def scores(j, p):
            return _bdot_nt(q2_ref[p], k_ref[rows(j), cols(p)])

        def keep_sums(z):
            soft = jnp.log(1.0 + jnp.exp2(jnp.abs(z) * -LOG2E))
            log_beta = jnp.minimum(z, 0.0) - soft
            log_keep = log_beta - z
            if diag:
                log_keep = jnp.where(causal2, log_keep, 0.0)
            return log_beta, _bdot(log_keep.astype(BF16), mm_ref[...])

        def weighted_values(j, p, log_beta, sums):
            carry = carry_ref[p]
            att = jnp.exp(log_beta + (sums[:, :BLK] + carry))
            if diag:
                att = jnp.where(causal2, att, 0.0)
            carry_ref[p] = carry + sums[:, BLK:]
            att = att.astype(BF16)
            vb = v_ref[rows(j), cols(p)]
            zero = jnp.zeros_like(vb)
            v2 = jnp.concatenate([jnp.where(lo_lanes, vb, zero), jnp.where(lo_lanes, zero, vb)], axis=0)
            return _bdot(jnp.concatenate([att[:BLK], att[BLK:]], axis=1), v2)

        chains = [(j, p) for j in js for p in range(npair)]
        z, ks, pv = {}, {}, {}
        for step in range(len(chains) + 3):
            if step < len(chains):
                z[step] = scores(*chains[step])
            if 0 <= step - 1 < len(chains):
                ks[step - 1] = keep_sums(z.pop(step - 1))
            if 0 <= step - 2 < len(chains):
                pv[step - 2] = weighted_values(*chains[step - 2], *ks.pop(step - 2))
            if 0 <= step - 3 < len(chains):
                acc_ref[chains[step - 3][1]] += pv.pop(step - 3)

    blocks([i], True)

    def two_blocks(t, c):
        blocks([i - 1 - 2 * t, i - 2 - 2 * t], False)
        return c

    lax.fori_loop(0, i // 2, two_blocks, 0)

    @pl.when(i % 2 == 1)
    def _():
        blocks([0], False)

    for p in range(npair):
        o_ref[:, p * PAIR:(p + 1) * PAIR] = acc_ref[p].astype(o_ref.dtype)


def _suffix_sum_matrix():
    sp = np.arange(BLK)[:, None]
    sc = np.arange(BLK)[None, :]
    return jnp.asarray(
        np.concatenate([(sp > sc).astype(np.float32), np.ones((BLK, BLK), np.float32)], axis=1), BF16)


def _stick_breaking(proj):
    b, s, n = proj.shape
    npair = WIDTH // PAIR
    return pl.pallas_call(
        _stick_kernel,
        grid=(b, s // BLK),
        in_specs=[
            pl.BlockSpec((None, BLK, WIDTH), lambda i, t: (i, t, 2)),
            pl.BlockSpec((None, s, WIDTH), lambda i, t: (i, 0, 3)),
            pl.BlockSpec((None, s, WIDTH), lambda i, t: (i, 0, 4)),
            pl.BlockSpec((BLK, 2 * BLK), lambda i, t: (0, 0)),
        ],
        out_specs=pl.BlockSpec((None, BLK, WIDTH), lambda i, t: (i, t, 0)),
        out_shape=jax.ShapeDtypeStruct((b, s, WIDTH), BF16),
        scratch_shapes=[pltpu.VMEM((npair, 2 * BLK, PAIR), BF16),
                        pltpu.VMEM((npair, 2 * BLK, BLK), F32),
                        pltpu.VMEM((npair, BLK, PAIR), F32)],
        compiler_params=_params(("arbitrary", "arbitrary")),
        name="stick_breaking",
    )(proj, proj, proj, _suffix_sum_matrix())


def _lane_min_index(mask, lane):
    return jnp.min(jnp.where(mask, lane, float(LANES)), -1, keepdims=True)


def _out_kernel(l_ref, r_ref, w_ref, x_ref, mod_ref, g_ref, rh_ref, rl_ref, rb_ref, tri_ref,
                xo_ref, h_ref, cls_ref, rank_ref, cnt_ref, seen):
    tm = x_ref.shape[0]

    @pl.when((pl.program_id(0) == 0) & (pl.program_id(1) == 0))
    def _():
        seen[...] = jnp.zeros_like(seen)

    m = mod_ref[...]
    mix = _bdot(l_ref[...], w_ref[0:WIDTH, :]) + _bdot(r_ref[...], w_ref[WIDTH:, :])
    x = x_ref[...] + m[2:3] * mix
    xo_ref[...] = x
    h = _rms_mod(x, g_ref[...], m[4:5], m[3:4])
    h_ref[...] = h
    logits = _dot3(h, rh_ref[...], rl_ref[...]) + rb_ref[...]
    lane = lax.broadcasted_iota(jnp.int32, logits.shape, 1).astype(F32)
    is_group = (lane >= N_EXPERTS) & (lane < N_EXPERTS + N_GROUPS)
    glog = jnp.where(is_group, logits, -jnp.inf)
    gmax = jnp.max(glog, -1, keepdims=True)
    g_idx = _lane_min_index(glog == gmax, lane) - N_EXPERTS
    in_group = (lane >= g_idx * EXPERTS_PER_GROUP) & (lane < (g_idx + 1) * EXPERTS_PER_GROUP)
    ev = jnp.where(in_group, logits, -jnp.inf)
    v1 = jnp.max(ev, -1, keepdims=True)
    i1 = _lane_min_index(ev == v1, lane)
    ev2 = jnp.where(lane == i1, -jnp.inf, ev)
    v2 = jnp.max(ev2, -1, keepdims=True)
    i2 = _lane_min_index(ev2 == v2, lane)
    lo = jnp.minimum(i1, i2) - g_idx * EXPERTS_PER_GROUP
    hi = jnp.maximum(i1, i2) - g_idx * EXPERTS_PER_GROUP
    cls = g_idx * PAIRS_PER_GROUP + (lo * (7.0 - lo) * 0.5 + (hi - lo - 1.0))
    cls_t = jnp.transpose(jnp.broadcast_to(cls, (tm, LANES)))
    onehot_t = lax.broadcasted_iota(jnp.int32, (LANES, tm), 0).astype(F32) == cls_t
    ones_t = jnp.where(onehot_t, 1.0, 0.0)
    earlier = _bdot(ones_t.astype(BF16), tri_ref[...]) + seen[...]
    cls_ref[...] = cls_t[0:1, :].astype(jnp.int32)
    rank_ref[...] = jnp.sum(jnp.where(onehot_t, earlier, 0.0), 0, keepdims=True).astype(jnp.int32)
    seen[...] += jnp.sum(ones_t, 1, keepdims=True)
    cnt_ref[...] = seen[...]


def _out_proj(left, right, w_out, x, mod, g, r_hi, r_lo, r_b):
    b, s, d = x.shape
    tm = TOKEN_TILE
    nt = s // tm
    tile = lambda w: pl.BlockSpec((None, tm, w), lambda i, t: (i, t, 0))
    const = lambda shp: pl.BlockSpec(shp, lambda i, t: tuple(0 for _ in shp))
    row = pl.BlockSpec((None, 1, tm), lambda i, t: (i * nt + t, 0, 0))
    tri = np.arange(tm)[:, None] < np.arange(tm)[None, :]
    return pl.pallas_call(
        _out_kernel,
        grid=(b, nt),
        in_specs=[tile(WIDTH), tile(WIDTH), const((2 * WIDTH, d)), tile(d),
                  pl.BlockSpec((None, 6, d), lambda i, t: (i, 0, 0)),
                  const((1, d)), const((d, LANES)), const((d, LANES)), const((1, LANES)),
                  const((tm, tm))],
        out_specs=[tile(d), tile(d), row, row, const((LANES, 1))],
        out_shape=[jax.ShapeDtypeStruct((b, s, d), F32),
                   jax.ShapeDtypeStruct((b, s, d), F32),
                   jax.ShapeDtypeStruct((b * nt, 1, tm), jnp.int32),
                   jax.ShapeDtypeStruct((b * nt, 1, tm), jnp.int32),
                   jax.ShapeDtypeStruct((LANES, 1), F32)],
        scratch_shapes=[pltpu.VMEM((LANES, 1), F32)],
        compiler_params=_params(("arbitrary", "arbitrary")),
        name="out_proj_router",
    )(left, right, w_out, x, mod, g.reshape(1, d), r_hi, r_lo, r_b, jnp.asarray(tri, BF16))


def _router_weights(w_group, b_group, w_router, b_router):
    d = w_group.shape[0]
    w = jnp.concatenate([jnp.transpose(w_router, (1, 0, 2)).reshape(d, N_EXPERTS), w_group], axis=1)
    w = jnp.pad(w.astype(F32), ((0, 0), (0, LANES - w.shape[1])))
    bias = jnp.concatenate([b_router.reshape(N_EXPERTS), b_group]).astype(F32)
    bias = jnp.pad(bias, (0, LANES - bias.shape[0])).reshape(1, LANES)
    hi, lo = _split(w)
    return hi, lo, bias


def _to_token_tiles(x):
    n = x.shape[0]
    parts = [x[:, g * LANES:(g + 1) * LANES].reshape(n // 8, 8, LANES) for g in range(8)]
    return jnp.swapaxes(jnp.stack(parts, axis=1), 1, 2).reshape(n, 8, LANES)


def _from_token_tiles(v):
    n = v.shape[0]
    w = jnp.swapaxes(v.reshape(n // 8, 8, 8, LANES), 1, 2)
    return jnp.concatenate([w[:, g].reshape(n, LANES) for g in range(8)], axis=-1)


def _row_dma_loop(n, start_row):
    def body(r, carry):
        start_row(r)
        return carry
    lax.fori_loop(0, n, body, 0, unroll=8)


def _dispatch_kernel(dest_ref, h_ref, zeros_ref, hs_ref, buf, sem):
    del zeros_ref
    tm = h_ref.shape[0]
    buf[...] = _to_token_tiles(h_ref[...])
    _row_dma_loop(tm, lambda r: pltpu.make_async_copy(
        buf.at[r], hs_ref.at[dest_ref[0, r]], sem).start())
    pltpu.make_async_copy(buf, hs_ref.at[pl.ds(0, tm)], sem).wait()


def _dispatch(h, dest, n_slots):
    b, s, d = h.shape
    tm = TOKEN_TILE
    nt = s // tm
    return pl.pallas_call(
        _dispatch_kernel,
        grid=(b * nt,),
        in_specs=[pl.BlockSpec((None, 1, tm), lambda i: (i, 0, 0), memory_space=pltpu.SMEM),
                  pl.BlockSpec((None, tm, d), lambda i: (i // nt, i % nt, 0)),
                  pl.BlockSpec(memory_space=pl.ANY)],
        out_specs=pl.BlockSpec(memory_space=pl.ANY),
        out_shape=jax.ShapeDtypeStruct((n_slots, 8, LANES), F32),
        scratch_shapes=[pltpu.VMEM((tm, 8, LANES), F32), pltpu.SemaphoreType.DMA(())],
        input_output_aliases={2: 0},
        compiler_params=_params(("arbitrary",)),
        name="moe_dispatch",
    )(dest, h, jnp.zeros((n_slots, 8, LANES), F32))


def _moe_kernel(ea_ref, eb_ref, grp_ref, used_ref, hs_ref, wg_ref, wu_ref, wd_ref,
                rh_ref, rl_ref, rb_ref, o_ref, hb, wts, acc):
    n = pl.program_id(0)
    k = pl.program_id(1)

    @pl.when(used_ref[n] == 1)
    def _():
        @pl.when(k == 0)
        def _():
            h = _from_token_tiles(hs_ref[...])
            hb[...] = h.astype(BF16)
            logits = _dot3(h, rh_ref[...], rl_ref[...]) + rb_ref[...]
            lane = lax.broadcasted_iota(jnp.int32, logits.shape, 1)
            pick = lambda idx: jnp.sum(jnp.where(lane == idx, logits, 0.0), -1, keepdims=True)
            is_group = (lane >= N_EXPERTS) & (lane < N_EXPERTS + N_GROUPS)
            g_w = 1.0 / jnp.sum(
                jnp.where(is_group, jnp.exp(logits - pick(N_EXPERTS + grp_ref[n])), 0.0),
                -1, keepdims=True)
            la = pick(ea_ref[n])
            lb = pick(eb_ref[n])
            wts[0] = g_w * (1.0 / (1.0 + jnp.exp(lb - la)))
            wts[1] = g_w * (1.0 / (1.0 + jnp.exp(la - lb)))

        x = hb[...]
        a = _bdot(x, wg_ref[...])
        hid = (a * jax.nn.sigmoid(a)) * _bdot(x, wu_ref[...])
        y = _bdot(hid.astype(BF16), wd_ref[...])

        @pl.when(k == 0)
        def _():
            acc[...] = wts[0] * y

        @pl.when(k == 1)
        def _():
            o_ref[...] = _to_token_tiles(acc[...] + wts[1] * y)

    @pl.when((used_ref[n] == 0) & (k == 1))
    def _():
        o_ref[...] = jnp.zeros_like(o_ref)


def _moe(hs, plan, w_gate, w_up, w_down, r_hi, r_lo, r_b):
    n_slots = hs.shape[0]
    ne, d, de = w_gate.shape
    tm = SLOT_TILE
    n_tiles = n_slots // tm
    slot = pl.BlockSpec((tm, 8, LANES), lambda n, k, *_: (n, 0, 0))
    expert = lambda shp: pl.BlockSpec(
        (None,) + shp, lambda n, k, ea, eb, grp, used: (jnp.where(k == 0, ea[n], eb[n]), 0, 0))
    const = lambda shp: pl.BlockSpec(shp, lambda n, k, *_: tuple(0 for _ in shp))
    return pl.pallas_call(
        _moe_kernel,
        grid_spec=pltpu.PrefetchScalarGridSpec(
            num_scalar_prefetch=4,
            grid=(n_tiles, 2),
            in_specs=[slot, expert((d, de)), expert((d, de)), expert((de, d)),
                      const((d, LANES)), const((d, LANES)), const((1, LANES))],
            out_specs=slot,
            scratch_shapes=[pltpu.VMEM((tm, d), BF16), pltpu.VMEM((2, tm, 1), F32),
                            pltpu.VMEM((tm, d), F32)]),
        out_shape=jax.ShapeDtypeStruct((n_slots, 8, LANES), F32),
        compiler_params=_params(("arbitrary", "arbitrary")),
        name="moe_experts",
    )(*plan, hs, w_gate, w_up, w_down, r_hi, r_lo, r_b)


def _combine_kernel(dest_ref, ys_ref, x_ref, mod_ref, fg_ref, o_ref, buf, sem, *, final):
    tm = x_ref.shape[0]
    _row_dma_loop(tm, lambda r: pltpu.make_async_copy(
        ys_ref.at[dest_ref[0, r]], buf.at[r], sem).start())
    pltpu.make_async_copy(ys_ref.at[pl.ds(0, tm)], buf, sem).wait()
    x = x_ref[...] + mod_ref[5:6, :] * _from_token_tiles(buf[...])
    if final:
        x = x * lax.rsqrt(jnp.mean(x * x, -1, keepdims=True) + EPS) * fg_ref[...]
    o_ref[...] = x


def _combine(ys, dest, x, mod, final_g, final):
    b, s, d = x.shape
    tm = TOKEN_TILE
    nt = s // tm
    return pl.pallas_call(
        functools.partial(_combine_kernel, final=final),
        grid=(b * nt,),
        in_specs=[pl.BlockSpec((None, 1, tm), lambda i: (i, 0, 0), memory_space=pltpu.SMEM),
                  pl.BlockSpec(memory_space=pl.ANY),
                  pl.BlockSpec((None, tm, d), lambda i: (i // nt, i % nt, 0)),
                  pl.BlockSpec((None, 6, d), lambda i: (i // nt, 0, 0)),
                  pl.BlockSpec((1, d), lambda i: (0, 0))],
        out_specs=pl.BlockSpec((None, tm, d), lambda i: (i // nt, i % nt, 0)),
        out_shape=jax.ShapeDtypeStruct((b, s, d), F32),
        scratch_shapes=[pltpu.VMEM((tm, 8, LANES), F32), pltpu.SemaphoreType.DMA(())],
        compiler_params=_params(("arbitrary",)),
        name="moe_combine",
    )(dest, ys, x, mod, final_g.reshape(1, d))


def _routing_plan(cls, rank, counts, n_tokens):
    tm = SLOT_TILE
    n_tiles = n_tokens // tm + N_CLASSES
    counts = counts[:N_CLASSES, 0].astype(jnp.int32)
    tiles = (counts + tm - 1) // tm
    tile_end = jnp.cumsum(tiles)
    dest = ((tile_end - tiles) * tm)[cls] + rank
    ids = jnp.arange(n_tiles, dtype=jnp.int32)
    used = ids < tile_end[-1]
    tile_cls = jnp.searchsorted(
        tile_end, jnp.minimum(ids, tile_end[-1] - 1), side="right").astype(jnp.int32)
    grp = tile_cls // PAIRS_PER_GROUP
    pair = tile_cls % PAIRS_PER_GROUP
    pair_a = jnp.asarray([0, 0, 0, 1, 1, 2], jnp.int32)
    pair_b = jnp.asarray([1, 2, 3, 2, 3, 3], jnp.int32)
    eb = grp * EXPERTS_PER_GROUP + pair_b[pair]
    ea = jnp.where(used, grp * EXPERTS_PER_GROUP + pair_a[pair], eb)
    return dest, (ea, eb, grp, used.astype(jnp.int32))


@jax.jit
def kernel(x, c, rel_bias, norm1_g, norm2_g, w_ada, b_ada, ev_w_in, ev_w_out, ev_gmlp_ln_g, ev_gmlp_ln_b, ev_w_s, ev_b_s, od_w_in, od_w_out, od_w_dw, od_b_dw, od_conv_ln_g, od_conv_ln_b, moe_w_group, moe_b_group, moe_w_router, moe_b_router, moe_w_gate, moe_w_up, moe_w_down, final_norm_g):
    b, s, d = x.shape
    depth = w_ada.shape[0]
    mods = _ada(c, w_ada, b_ada).reshape(depth, b, 6, d)
    biases = [_branch_bias(rel_bias, dil) for _, dil in DILATED_BRANCHES]
    for layer in range(depth):
        j = layer // 2
        mod = mods[layer]
        if layer % 2 == 0:
            proj = _in_proj(x, norm1_g[layer], mod, ev_w_in[j].astype(BF16))
            outs = [_dilated_branch(proj, bias, dil)
                    for bias, (_, dil) in zip(biases, DILATED_BRANCHES)]
            left = _combine_branches(outs)
            right = _gmlp(proj, ev_gmlp_ln_g[j], ev_gmlp_ln_b[j], ev_w_s[j], ev_b_s[j])
            w_out = ev_w_out[j]
        else:
            proj = _in_proj(x, norm1_g[layer], mod, od_w_in[j].astype(BF16))
            left = _conv_module(proj, od_w_dw[j], od_b_dw[j], od_conv_ln_g[j], od_conv_ln_b[j])
            right = _stick_breaking(proj)
            w_out = od_w_out[j]
        r_hi, r_lo, r_b = _router_weights(moe_w_group[layer], moe_b_group[layer],
                                          moe_w_router[layer], moe_b_router[layer])
        x, h, cls, rank, counts = _out_proj(left, right, w_out.astype(BF16), x, mod,
                                            norm2_g[layer], r_hi, r_lo, r_b)
        dest, plan = _routing_plan(cls, rank, counts, b * s)
        hs = _dispatch(h, dest, (b * s // SLOT_TILE + N_CLASSES) * SLOT_TILE)
        ys = _moe(hs, plan, moe_w_gate[layer].astype(BF16), moe_w_up[layer].astype(BF16),
                  moe_w_down[layer].astype(BF16), r_hi, r_lo, r_b)
        x = _combine(ys, dest, x, mod, final_norm_g, layer == depth - 1)
    return x
```

```python
import functools
import math

import numpy as np
import jax
import jax.numpy as jnp
from jax import lax
from jax.experimental import pallas as pl
from jax.experimental.pallas import tpu as pltpu

F32 = jnp.float32
BF16 = jnp.bfloat16

HEAD_DIM = 64
HEADS = 8
WIDTH = HEADS * HEAD_DIM
PAIR = 2 * HEAD_DIM
BLK = 128
DILATED_BRANCHES = ((128, 1), (512, 4), (2048, 16))
N_BUCKETS = 32
MAX_DISTANCE = 2048
CONV_WIDTH = 31
CONV_HALO = 32
B_GROUPS = 4
N_GROUPS = 4
EXPERTS_PER_GROUP = 4
N_EXPERTS = N_GROUPS * EXPERTS_PER_GROUP
PAIRS_PER_GROUP = EXPERTS_PER_GROUP * (EXPERTS_PER_GROUP - 1) // 2
N_CLASSES = N_GROUPS * PAIRS_PER_GROUP
TOKEN_TILE = 512
SLOT_TILE = 512
EPS = 1e-6
NEG = -1e30
LOG2E = 1.4426950408889634
LANES = 128
SUBLANES = 8
VMEM_LIMIT = 48 * 1024 * 1024


def _params(sem):
    return pltpu.CompilerParams(dimension_semantics=sem, vmem_limit_bytes=VMEM_LIMIT)


def _bdot(a, b):
    return jnp.dot(a, b, preferred_element_type=F32)


def _bdot_nt(a, b):
    return lax.dot_general(a, b, (((1,), (1,)), ((), ())), preferred_element_type=F32)


def _split(a):
    hi = a.astype(BF16)
    lo = (a - hi.astype(F32)).astype(BF16)
    return hi, lo


def _dot3(a, w_hi, w_lo):
    a_hi, a_lo = _split(a)
    return _bdot(a_hi, w_hi) + (_bdot(a_hi, w_lo) + _bdot(a_lo, w_hi))


def _ada_kernel(c_ref, w_ref, b_ref, o_ref):
    c = c_ref[...]
    sc = c * jax.nn.sigmoid(c)
    w_hi, w_lo = _split(w_ref[...])
    o_ref[...] = _dot3(sc, w_hi, w_lo) + b_ref[...]


def _ada(c, w_ada, b_ada):
    depth, d, n = w_ada.shape
    b = c.shape[0]
    tn = 1024
    return pl.pallas_call(
        _ada_kernel,
        grid=(depth, n // tn),
        in_specs=[
            pl.BlockSpec((b, d), lambda l, j: (0, 0)),
            pl.BlockSpec((None, d, tn), lambda l, j: (l, 0, j)),
            pl.BlockSpec((None, 1, tn), lambda l, j: (l, 0, j)),
        ],
        out_specs=pl.BlockSpec((None, b, tn), lambda l, j: (l, 0, j)),
        out_shape=jax.ShapeDtypeStruct((depth, b, n), F32),
        compiler_params=_params(("arbitrary", "arbitrary")),
        name="ada_mod",
    )(c, w_ada, b_ada.reshape(depth, 1, n))


def _rms_mod(x, g, scale_row, shift_row):
    y = x * lax.rsqrt(jnp.mean(x * x, -1, keepdims=True) + EPS) * g
    return y * (1.0 + scale_row) + shift_row


def _in_kernel(x_ref, g_ref, mod_ref, w_ref, o_ref, *rest, dilations):
    m = mod_ref[...]
    h = _rms_mod(x_ref[...], g_ref[...], m[1:2], m[0:1]).astype(BF16)
    tm = x_ref.shape[0]
    n = w_ref.shape[1]
    per = WIDTH // LANES
    for j in range(n // WIDTH):
        cols = slice(j * WIDTH, (j + 1) * WIDTH)
        y = _bdot(h, w_ref[:, cols])
        o_ref[:, cols] = y.astype(o_ref.dtype)
        if dilations and j < 3:
            for c in range(per):
                rest[-1][j * per + c] = y[:, c * LANES:(c + 1) * LANES]
    for d, out in zip(dilations, rest):
        for r in range(d):
            for c in range(3 * per):
                lo = (r * 3 * per + c) * LANES
                out[:, lo:lo + LANES] = rest[-1][c, pl.ds(r, tm // d, stride=d), :].astype(out.dtype)


def _in_proj(x, g, mod, w_in, dilations=()):
    b, s, d = x.shape
    n = w_in.shape[1]
    tm = TOKEN_TILE
    qkv = 3 * WIDTH
    outs = pl.pallas_call(
        functools.partial(_in_kernel, dilations=tuple(dilations)),
        grid=(b, s // tm),
        in_specs=[
            pl.BlockSpec((None, tm, d), lambda i, t: (i, t, 0)),
            pl.BlockSpec((1, d), lambda i, t: (0, 0)),
            pl.BlockSpec((None, 6, d), lambda i, t: (i, 0, 0)),
            pl.BlockSpec((d, n), lambda i, t: (0, 0)),
        ],
        out_specs=[pl.BlockSpec((None, tm, n), lambda i, t: (i, t, 0))] + [
            pl.BlockSpec((None, tm // dil, dil * qkv), lambda i, t: (i, t, 0)) for dil in dilations],
        out_shape=[jax.ShapeDtypeStruct((b, s, n), BF16)] + [
            jax.ShapeDtypeStruct((b, s // dil, dil * qkv), BF16) for dil in dilations],
        scratch_shapes=[pltpu.VMEM((qkv // LANES, tm, LANES), F32)] if dilations else [],
        compiler_params=_params(("arbitrary", "arbitrary")),
        name="in_proj",
    )(x, g.reshape(1, d), mod, w_in)
    return outs if dilations else outs[0]


def _t5_bucket(dist):
    max_exact = N_BUCKETS // 2
    d = np.maximum(dist, 1).astype(np.float32)
    large = max_exact + (np.log(d / max_exact) / np.log(MAX_DISTANCE / max_exact)
                         * (N_BUCKETS - max_exact)).astype(np.int32)
    large = np.minimum(large, N_BUCKETS - 1)
    return np.where(dist < max_exact, dist, large).astype(np.int32)


def _branch_bias(rel_table, dilation):
    qi = np.arange(BLK)[:, None]
    kj = np.arange(2 * BLK)[None, :]
    bucket = _t5_bucket(np.clip(qi + BLK - kj, 0, None) * dilation)
    return jnp.transpose(rel_table.astype(F32)[bucket], (2, 0, 1))


def _dil_kernel(q_ref, kp_ref, kc_ref, vp_ref, vc_ref, bias_ref, o_ref, m_ref, l_ref,
                kbuf, vbuf, *, tq):
    n = pl.program_id(2)
    nsub = tq // BLK
    kbuf[0:BLK] = kp_ref[...]
    kbuf[BLK:] = kc_ref[...]
    vbuf[0:BLK] = vp_ref[...]
    vbuf[BLK:] = vc_ref[...]
    qi = lax.broadcasted_iota(jnp.int32, (BLK, 2 * BLK), 0)
    kj = lax.broadcasted_iota(jnp.int32, (BLK, 2 * BLK), 1)
    dist = qi + BLK - kj
    in_window = (dist >= 0) & (dist <= BLK)
    lane = lax.broadcasted_iota(jnp.int32, (BLK, PAIR), 1)
    scale = HEAD_DIM ** -0.5
    for i in range(nsub):
        first_key = jnp.where(n * nsub + i == 0, BLK, 0)
        valid = in_window & (kj >= first_key)
        rows = slice(i * BLK, (i + 1) * BLK)
        krows = slice(i * BLK, (i + 2) * BLK)
        m_tile = jnp.zeros((BLK, LANES), F32)
        l_tile = jnp.ones((BLK, LANES), F32)
        for p in range(HEADS // 2):
            cols = slice(p * PAIR, (p + 1) * PAIR)
            q_pair = q_ref[rows, cols]
            k_pair = kbuf[krows, cols]
            v_pair = vbuf[krows, cols]
            outs = []
            for hh in range(2):
                h = 2 * p + hh
                head_lanes = (lane >= HEAD_DIM) if hh else (lane < HEAD_DIM)
                qm = jnp.where(head_lanes, q_pair, jnp.zeros_like(q_pair))
                s = _bdot_nt(qm, k_pair) * scale
                logit = jnp.where(valid, s + bias_ref[h], NEG)
                mx = jnp.max(logit, -1, keepdims=True)
                pr = jnp.exp(logit - mx)
                den = jnp.sum(pr, -1, keepdims=True)
                outs.append(_bdot(pr.astype(BF16), v_pair) / den)
                m_tile = jnp.where(lane == h, mx, m_tile)
                l_tile = jnp.where(lane == h, den, l_tile)
            o_ref[rows, cols] = jnp.where(lane < HEAD_DIM, outs[0], outs[1]).astype(o_ref.dtype)
        m_ref[rows, :] = m_tile
        l_ref[rows, :] = l_tile


def _dilated_branch(view, bias, dilation, ncol):
    b, sub, _ = view.shape
    tq = min(512, sub)
    nsub = tq // BLK
    cur = lambda c: pl.BlockSpec((None, tq, WIDTH), lambda i, r, t: (i, t, r * ncol + c))
    prev = lambda c: pl.BlockSpec(
        (None, BLK, WIDTH), lambda i, r, t: (i, jnp.maximum(t * nsub - 1, 0), r * ncol + c))
    return pl.pallas_call(
        functools.partial(_dil_kernel, tq=tq),
        grid=(b, dilation, sub // tq),
        in_specs=[cur(0), prev(1), cur(1), prev(2), cur(2),
                  pl.BlockSpec((HEADS, BLK, 2 * BLK), lambda i, r, t: (0, 0, 0))],
        out_specs=[
            pl.BlockSpec((None, tq, WIDTH), lambda i, r, t: (i, t, r)),
            pl.BlockSpec((None, tq, LANES), lambda i, r, t: (i, t, r)),
            pl.BlockSpec((None, tq, LANES), lambda i, r, t: (i, t, r)),
        ],
        out_shape=[
            jax.ShapeDtypeStruct((b, sub, dilation * WIDTH), BF16),
            jax.ShapeDtypeStruct((b, sub, dilation * LANES), F32),
            jax.ShapeDtypeStruct((b, sub, dilation * LANES), F32),
        ],
        scratch_shapes=[pltpu.VMEM((BLK + tq, WIDTH), BF16), pltpu.VMEM((BLK + tq, WIDTH), BF16)],
        compiler_params=_params(("arbitrary", "arbitrary", "arbitrary")),
        name=f"dilated_d{dilation}",
    )(view, view, view, view, view, bias)


def _comb_kernel(*refs, dilations):
    nb = len(dilations)
    o_refs, m_refs, l_refs = refs[:nb], refs[nb:2 * nb], refs[2 * nb:3 * nb]
    e_ref, out_ref, o_scr, s_scr = refs[3 * nb:]
    tm = out_ref.shape[0]
    per = WIDTH // LANES
    os, ms, ls = [], [], []
    for bi, d in enumerate(dilations):
        if d == 1:
            os.append([o_refs[bi][:, c * LANES:(c + 1) * LANES].astype(F32) for c in range(per)])
            ms.append(m_refs[bi][...])
            ls.append(l_refs[bi][...])
            continue
        rows = lambda r: pl.ds(r, tm // d, stride=d)
        for r in range(d):
            for c in range(per):
                lo = (r * per + c) * LANES
                o_scr[bi, c, rows(r), :] = o_refs[bi][:, lo:lo + LANES].astype(F32)
            s_scr[bi, 0, rows(r), :] = m_refs[bi][:, r * LANES:(r + 1) * LANES]
            s_scr[bi, 1, rows(r), :] = l_refs[bi][:, r * LANES:(r + 1) * LANES]
        os.append([o_scr[bi, c] for c in range(per)])
        ms.append(s_scr[bi, 0])
        ls.append(s_scr[bi, 1])
    m_all = functools.reduce(jnp.maximum, ms)
    es = [l * jnp.exp(m - m_all) for l, m in zip(ls, ms)]
    tot = functools.reduce(lambda x, y: x + y, es)
    acc = [None] * per
    for e, o in zip(es, os):
        w_hi, w_lo = _split(e / tot)
        w = _bdot(w_hi, e_ref[...]) + _bdot(w_lo, e_ref[...])
        for c in range(per):
            term = w[:, c * LANES:(c + 1) * LANES] * o[c]
            acc[c] = term if acc[c] is None else acc[c] + term
    for c in range(per):
        out_ref[:, c * LANES:(c + 1) * LANES] = acc[c].astype(out_ref.dtype)


def _combine_branches(outs, dilations, s):
    b = outs[0][0].shape[0]
    tm = TOKEN_TILE
    nb = len(dilations)
    expand = np.zeros((LANES, WIDTH), np.float32)
    for h in range(HEADS):
        expand[h, h * HEAD_DIM:(h + 1) * HEAD_DIM] = 1.0
    grouped = lambda w: [pl.BlockSpec((None, tm // d, d * w), lambda i, t: (i, t, 0)) for d in dilations]
    return pl.pallas_call(
        functools.partial(_comb_kernel, dilations=tuple(dilations)),
        grid=(b, s // tm),
        in_specs=grouped(WIDTH) + grouped(LANES) + grouped(LANES)
        + [pl.BlockSpec((LANES, WIDTH), lambda i, t: (0, 0))],
        out_specs=pl.BlockSpec((None, tm, WIDTH), lambda i, t: (i, t, 0)),
        out_shape=jax.ShapeDtypeStruct((b, s, WIDTH), BF16),
        scratch_shapes=[pltpu.VMEM((nb, WIDTH // LANES, tm, LANES), F32),
                        pltpu.VMEM((nb, 2, tm, LANES), F32)],
        compiler_params=_params(("arbitrary", "arbitrary")),
        name="dilated_combine",
    )(*[o for o, _, _ in outs], *[m for _, m, _ in outs], *[l for _, _, l in outs],
      jnp.asarray(expand, BF16))


def _gelu(x):
    return 0.5 * x * (1.0 + jnp.tanh(math.sqrt(2.0 / math.pi) * (x + 0.044715 * (x * x * x))))


def _gmlp_kernel(u_ref, v_ref, g_ref, b_ref, ws_ref, bs_ref, o_ref):
    tm = u_ref.shape[0]
    u = _gelu(u_ref[...].astype(F32))
    v = _gelu(v_ref[...].astype(F32))
    mu = jnp.mean(v, -1, keepdims=True)
    vc = v - mu
    var = jnp.mean(vc * vc, -1, keepdims=True)
    v = (vc * lax.rsqrt(var + EPS) * g_ref[...] + b_ref[...]).astype(BF16)
    r = lax.broadcasted_iota(jnp.int32, (BLK, BLK), 0)
    c = lax.broadcasted_iota(jnp.int32, (BLK, BLK), 1)
    for g in range(B_GROUPS):
        w = jnp.where(c <= r, ws_ref[g], 0.0).astype(BF16)
        cols = slice(g * BLK, (g + 1) * BLK)
        for ch in range(tm // BLK):
            rows = slice(ch * BLK, (ch + 1) * BLK)
            mixed = _bdot(w, v[rows, cols]) + bs_ref[g]
            o_ref[rows, cols] = (u[rows, cols] * mixed).astype(o_ref.dtype)


def _gmlp(proj, ln_g, ln_b, w_s, b_s):
    b, s, n = proj.shape
    tm = 512
    return pl.pallas_call(
        _gmlp_kernel,
        grid=(b, s // tm),
        in_specs=[
            pl.BlockSpec((None, tm, WIDTH), lambda i, t: (i, t, 3)),
            pl.BlockSpec((None, tm, WIDTH), lambda i, t: (i, t, 4)),
            pl.BlockSpec((1, WIDTH), lambda i, t: (0, 0)),
            pl.BlockSpec((1, WIDTH), lambda i, t: (0, 0)),
            pl.BlockSpec((B_GROUPS, BLK, BLK), lambda i, t: (0, 0, 0)),
            pl.BlockSpec((B_GROUPS, BLK, 1), lambda i, t: (0, 0, 0)),
        ],
        out_specs=pl.BlockSpec((None, tm, WIDTH), lambda i, t: (i, t, 0)),
        out_shape=jax.ShapeDtypeStruct((b, s, WIDTH), BF16),
        compiler_params=_params(("arbitrary", "arbitrary")),
        name="gmlp_gate",
    )(proj, proj, ln_g.reshape(1, WIDTH), ln_b.reshape(1, WIDTH), w_s,
      b_s.reshape(B_GROUPS, BLK, 1))


def _conv_kernel(a_ref, g_ref, ah_ref, gh_ref, w_ref, bdw_ref, lng_ref, lnb_ref, o_ref, hbuf):
    tm = a_ref.shape[0]
    t = pl.program_id(1)
    halo = ah_ref[...].astype(F32) * jax.nn.sigmoid(gh_ref[...].astype(F32))
    hbuf[0, 0:CONV_HALO] = jnp.where(t == 0, 0.0, halo)
    hbuf[0, CONV_HALO:] = a_ref[...].astype(F32) * jax.nn.sigmoid(g_ref[...].astype(F32))
    n_shifted = tm + CONV_HALO - SUBLANES
    for s in range(1, SUBLANES):
        hbuf[s, 0:n_shifted] = hbuf[0, pl.ds(s, n_shifted), :]
    rc = 32
    off = CONV_HALO - (CONV_WIDTH - 1)

    def chunk(ci, carry):
        base = pl.multiple_of(ci * rc, rc)
        accs = [jnp.zeros((rc, WIDTH), F32) + bdw_ref[...]] + [None] * 3
        for k in range(CONV_WIDTH):
            s = (off + k) % SUBLANES
            term = w_ref[k:k + 1, :] * hbuf[s, pl.ds(base + (off + k - s), rc), :]
            accs[k % 4] = term if accs[k % 4] is None else accs[k % 4] + term
        acc = (accs[0] + accs[1]) + (accs[2] + accs[3])
        mu = jnp.mean(acc, -1, keepdims=True)
        xc = acc - mu
        var = jnp.mean(xc * xc, -1, keepdims=True)
        y = xc * lax.rsqrt(var + EPS) * lng_ref[...] + lnb_ref[...]
        o_ref[pl.ds(base, rc), :] = (y * jax.nn.sigmoid(y)).astype(o_ref.dtype)
        return carry

    lax.fori_loop(0, tm // rc, chunk, 0)


def _conv_module(proj, w_dw, b_dw, ln_g, ln_b):
    b, s, n = proj.shape
    tm = 512
    hb = tm // CONV_HALO
    row = lambda v: v.reshape(1, WIDTH)
    halo = lambda c: pl.BlockSpec(
        (None, CONV_HALO, WIDTH), lambda i, t: (i, jnp.maximum(t * hb - 1, 0), c))
    return pl.pallas_call(
        _conv_kernel,
        grid=(b, s // tm),
        in_specs=[
            pl.BlockSpec((None, tm, WIDTH), lambda i, t: (i, t, 0)),
            pl.BlockSpec((None, tm, WIDTH), lambda i, t: (i, t, 1)),
            halo(0), halo(1),
            pl.BlockSpec((CONV_WIDTH, WIDTH), lambda i, t: (0, 0)),
            pl.BlockSpec((1, WIDTH), lambda i, t: (0, 0)),
            pl.BlockSpec((1, WIDTH), lambda i, t: (0, 0)),
            pl.BlockSpec((1, WIDTH), lambda i, t: (0, 0)),
        ],
        out_specs=pl.BlockSpec((None, tm, WIDTH), lambda i, t: (i, t, 0)),
        out_shape=jax.ShapeDtypeStruct((b, s, WIDTH), BF16),
        scratch_shapes=[pltpu.VMEM((SUBLANES, CONV_HALO + tm, WIDTH), F32)],
        compiler_params=_params(("arbitrary", "arbitrary")),
        name="conv_module",
    )(proj, proj, proj, proj, w_dw, row(b_dw), row(ln_g), row(ln_b))


def _stick_kernel(q_ref, k_ref, v_ref, mm_ref, o_ref, q2_ref, carry_ref, acc_ref):
    i = pl.program_id(1)
    npair = WIDTH // PAIR
    lane = lax.broadcasted_iota(jnp.int32, (BLK, PAIR), 1)
    lo_lanes = lane < HEAD_DIM
    row2 = lax.broadcasted_iota(jnp.int32, (2 * BLK, BLK), 0)
    col2 = lax.broadcasted_iota(jnp.int32, (2 * BLK, BLK), 1)
    causal2 = col2 < (row2 & (BLK - 1))
    scale = HEAD_DIM ** -0.5
    for p in range(npair):
        qp = q_ref[:, p * PAIR:(p + 1) * PAIR] * scale
        zero = jnp.zeros_like(qp)
        q2_ref[p] = jnp.concatenate(
            [jnp.where(lo_lanes, qp, zero), jnp.where(lo_lanes, zero, qp)], axis=0)
    carry_ref[...] = jnp.zeros_like(carry_ref)
    acc_ref[...] = jnp.zeros_like(acc_ref)

    def blocks(js, diag):
        cols = lambda p: slice(p * PAIR, (p + 1) * PAIR)
        rows = lambda j: pl.ds(pl.multiple_of(j * BLK, BLK), BLK)

        def scores(j, p):
            return _bdot_nt(q2_ref[p], k_ref[rows(j), cols(p)])

        def keep_sums(z):
            soft = jnp.log(1.0 + jnp.exp2(jnp.abs(z) * -LOG2E))
            log_beta = jnp.minimum(z, 0.0) - soft
            log_keep = log_beta - z
            if diag:
                log_keep = jnp.where(causal2, log_keep, 0.0)
            return log_beta, _bdot(log_keep.astype(BF16), mm_ref[...])

        def weighted_values(j, p, log_beta, sums):
            carry = carry_ref[p]
            att = jnp.exp(log_beta + (sums[:, :BLK] + carry))
            if diag:
                att = jnp.where(causal2, att, 0.0)
            carry_ref[p] = carry + sums[:, BLK:]
            att = att.astype(BF16)
            vb = v_ref[rows(j), cols(p)]
            zero = jnp.zeros_like(vb)
            v2 = jnp.concatenate([jnp.where(lo_lanes, vb, zero), jnp.where(lo_lanes, zero, vb)], axis=0)
            return _bdot(jnp.concatenate([att[:BLK], att[BLK:]], axis=1), v2)

        chains = [(j, p) for j in js for p in range(npair)]
        z, ks, pv = {}, {}, {}
        for step in range(len(chains) + 3):
            if step < len(chains):
                z[step] = scores(*chains[step])
            if 0 <= step - 1 < len(chains):
                ks[step - 1] = keep_sums(z.pop(step - 1))
            if 0 <= step - 2 < len(chains):
                pv[step - 2] = weighted_values(*chains[step - 2], *ks.pop(step - 2))
            if 0 <= step - 3 < len(chains):
                acc_ref[chains[step - 3][1]] += pv.pop(step - 3)

    blocks([i], True)

    def four_blocks(t, c):
        j = i - 1 - 4 * t
        blocks([j, j - 1, j - 2, j - 3], False)
        return c

    lax.fori_loop(0, i // 4, four_blocks, 0)
    rem = i % 4

    @pl.when(rem >= 2)
    def _():
        blocks([rem - 1, rem - 2], False)

    @pl.when(rem % 2 == 1)
    def _():
        blocks([0], False)

    for p in range(npair):
        o_ref[:, p * PAIR:(p + 1) * PAIR] = acc_ref[p].astype(o_ref.dtype)


def _suffix_sum_matrix():
    sp = np.arange(BLK)[:, None]
    sc = np.arange(BLK)[None, :]
    return jnp.asarray(
        np.concatenate([(sp > sc).astype(np.float32), np.ones((BLK, BLK), np.float32)], axis=1), BF16)


def _stick_breaking(proj):
    b, s, n = proj.shape
    npair = WIDTH // PAIR
    return pl.pallas_call(
        _stick_kernel,
        grid=(b, s // BLK),
        in_specs=[
            pl.BlockSpec((None, BLK, WIDTH), lambda i, t: (i, t, 2)),
            pl.BlockSpec((None, s, WIDTH), lambda i, t: (i, 0, 3)),
            pl.BlockSpec((None, s, WIDTH), lambda i, t: (i, 0, 4)),
            pl.BlockSpec((BLK, 2 * BLK), lambda i, t: (0, 0)),
        ],
        out_specs=pl.BlockSpec((None, BLK, WIDTH), lambda i, t: (i, t, 0)),
        out_shape=jax.ShapeDtypeStruct((b, s, WIDTH), BF16),
        scratch_shapes=[pltpu.VMEM((npair, 2 * BLK, PAIR), BF16),
                        pltpu.VMEM((npair, 2 * BLK, BLK), F32),
                        pltpu.VMEM((npair, BLK, PAIR), F32)],
        compiler_params=_params(("arbitrary", "arbitrary")),
        name="stick_breaking",
    )(proj, proj, proj, _suffix_sum_matrix())


def _lane_min_index(mask, lane):
    return jnp.min(jnp.where(mask, lane, float(LANES)), -1, keepdims=True)


def _out_kernel(l_ref, r_ref, w_ref, x_ref, mod_ref, g_ref, rh_ref, rl_ref, rb_ref, tri_ref,
                xo_ref, h_ref, cls_ref, rank_ref, cnt_ref, seen):
    tm = x_ref.shape[0]

    @pl.when((pl.program_id(0) == 0) & (pl.program_id(1) == 0))
    def _():
        seen[...] = jnp.zeros_like(seen)

    m = mod_ref[...]
    mix = _bdot(l_ref[...], w_ref[0:WIDTH, :]) + _bdot(r_ref[...], w_ref[WIDTH:, :])
    x = x_ref[...] + m[2:3] * mix
    xo_ref[...] = x
    h = _rms_mod(x, g_ref[...], m[4:5], m[3:4])
    h_ref[...] = h
    logits = _dot3(h, rh_ref[...], rl_ref[...]) + rb_ref[...]
    lane = lax.broadcasted_iota(jnp.int32, logits.shape, 1).astype(F32)
    is_group = (lane >= N_EXPERTS) & (lane < N_EXPERTS + N_GROUPS)
    glog = jnp.where(is_group, logits, -jnp.inf)
    gmax = jnp.max(glog, -1, keepdims=True)
    g_idx = _lane_min_index(glog == gmax, lane) - N_EXPERTS
    in_group = (lane >= g_idx * EXPERTS_PER_GROUP) & (lane < (g_idx + 1) * EXPERTS_PER_GROUP)
    ev = jnp.where(in_group, logits, -jnp.inf)
    v1 = jnp.max(ev, -1, keepdims=True)
    i1 = _lane_min_index(ev == v1, lane)
    ev2 = jnp.where(lane == i1, -jnp.inf, ev)
    v2 = jnp.max(ev2, -1, keepdims=True)
    i2 = _lane_min_index(ev2 == v2, lane)
    lo = jnp.minimum(i1, i2) - g_idx * EXPERTS_PER_GROUP
    hi = jnp.maximum(i1, i2) - g_idx * EXPERTS_PER_GROUP
    cls = g_idx * PAIRS_PER_GROUP + (lo * (7.0 - lo) * 0.5 + (hi - lo - 1.0))
    cls_t = jnp.transpose(jnp.broadcast_to(cls, (tm, LANES)))
    onehot_t = lax.broadcasted_iota(jnp.int32, (LANES, tm), 0).astype(F32) == cls_t
    ones_t = jnp.where(onehot_t, 1.0, 0.0)
    earlier = _bdot(ones_t.astype(BF16), tri_ref[...]) + seen[...]
    cls_ref[...] = cls_t[0:1, :].astype(jnp.int32)
    rank_ref[...] = jnp.sum(jnp.where(onehot_t, earlier, 0.0), 0, keepdims=True).astype(jnp.int32)
    seen[...] += jnp.sum(ones_t, 1, keepdims=True)
    cnt_ref[...] = seen[...]


def _out_proj(left, right, w_out, x, mod, g, r_hi, r_lo, r_b):
    b, s, d = x.shape
    tm = TOKEN_TILE
    nt = s // tm
    tile = lambda w: pl.BlockSpec((None, tm, w), lambda i, t: (i, t, 0))
    const = lambda shp: pl.BlockSpec(shp, lambda i, t: tuple(0 for _ in shp))
    row = pl.BlockSpec((None, 1, tm), lambda i, t: (i * nt + t, 0, 0))
    tri = np.arange(tm)[:, None] < np.arange(tm)[None, :]
    return pl.pallas_call(
        _out_kernel,
        grid=(b, nt),
        in_specs=[tile(WIDTH), tile(WIDTH), const((2 * WIDTH, d)), tile(d),
                  pl.BlockSpec((None, 6, d), lambda i, t: (i, 0, 0)),
                  const((1, d)), const((d, LANES)), const((d, LANES)), const((1, LANES)),
                  const((tm, tm))],
        out_specs=[tile(d), tile(d), row, row, const((LANES, 1))],
        out_shape=[jax.ShapeDtypeStruct((b, s, d), F32),
                   jax.ShapeDtypeStruct((b, s, d), F32),
                   jax.ShapeDtypeStruct((b * nt, 1, tm), jnp.int32),
                   jax.ShapeDtypeStruct((b * nt, 1, tm), jnp.int32),
                   jax.ShapeDtypeStruct((LANES, 1), F32)],
        scratch_shapes=[pltpu.VMEM((LANES, 1), F32)],
        compiler_params=_params(("arbitrary", "arbitrary")),
        name="out_proj_router",
    )(left, right, w_out, x, mod, g.reshape(1, d), r_hi, r_lo, r_b, jnp.asarray(tri, BF16))


def _router_weights(w_group, b_group, w_router, b_router):
    d = w_group.shape[0]
    w = jnp.concatenate([jnp.transpose(w_router, (1, 0, 2)).reshape(d, N_EXPERTS), w_group], axis=1)
    w = jnp.pad(w.astype(F32), ((0, 0), (0, LANES - w.shape[1])))
    bias = jnp.concatenate([b_router.reshape(N_EXPERTS), b_group]).astype(F32)
    bias = jnp.pad(bias, (0, LANES - bias.shape[0])).reshape(1, LANES)
    hi, lo = _split(w)
    return hi, lo, bias


def _to_token_tiles(x):
    n = x.shape[0]
    parts = [x[:, g * LANES:(g + 1) * LANES].reshape(n // 8, 8, LANES) for g in range(8)]
    return jnp.swapaxes(jnp.stack(parts, axis=1), 1, 2).reshape(n, 8, LANES)


def _from_token_tiles(v):
    n = v.shape[0]
    w = jnp.swapaxes(v.reshape(n // 8, 8, 8, LANES), 1, 2)
    return jnp.concatenate([w[:, g].reshape(n, LANES) for g in range(8)], axis=-1)


def _row_dma_loop(n, start_row):
    def body(r, carry):
        start_row(r)
        return carry
    lax.fori_loop(0, n, body, 0, unroll=8)


def _dispatch_kernel(dest_ref, h_ref, zeros_ref, hs_ref, buf, sem):
    del zeros_ref
    tm = h_ref.shape[0]
    buf[...] = _to_token_tiles(h_ref[...])
    _row_dma_loop(tm, lambda r: pltpu.make_async_copy(
        buf.at[r], hs_ref.at[dest_ref[0, r]], sem).start())
    pltpu.make_async_copy(buf, hs_ref.at[pl.ds(0, tm)], sem).wait()


def _dispatch(h, dest, n_slots):
    b, s, d = h.shape
    tm = TOKEN_TILE
    nt = s // tm
    return pl.pallas_call(
        _dispatch_kernel,
        grid=(b * nt,),
        in_specs=[pl.BlockSpec((None, 1, tm), lambda i: (i, 0, 0), memory_space=pltpu.SMEM),
                  pl.BlockSpec((None, tm, d), lambda i: (i // nt, i % nt, 0)),
                  pl.BlockSpec(memory_space=pl.ANY)],
        out_specs=pl.BlockSpec(memory_space=pl.ANY),
        out_shape=jax.ShapeDtypeStruct((n_slots, 8, LANES), F32),
        scratch_shapes=[pltpu.VMEM((tm, 8, LANES), F32), pltpu.SemaphoreType.DMA(())],
        input_output_aliases={2: 0},
        compiler_params=_params(("arbitrary",)),
        name="moe_dispatch",
    )(dest, h, jnp.zeros((n_slots, 8, LANES), F32))


def _moe_kernel(ea_ref, eb_ref, grp_ref, used_ref, hs_ref, wg_ref, wu_ref, wd_ref,
                rh_ref, rl_ref, rb_ref, o_ref, hb, wts, acc):
    n = pl.program_id(0)
    k = pl.program_id(1)

    @pl.when(used_ref[n] == 1)
    def _():
        @pl.when(k == 0)
        def _():
            h = _from_token_tiles(hs_ref[...])
            hb[...] = h.astype(BF16)
            logits = _dot3(h, rh_ref[...], rl_ref[...]) + rb_ref[...]
            lane = lax.broadcasted_iota(jnp.int32, logits.shape, 1)
            pick = lambda idx: jnp.sum(jnp.where(lane == idx, logits, 0.0), -1, keepdims=True)
            is_group = (lane >= N_EXPERTS) & (lane < N_EXPERTS + N_GROUPS)
            g_w = 1.0 / jnp.sum(
                jnp.where(is_group, jnp.exp(logits - pick(N_EXPERTS + grp_ref[n])), 0.0),
                -1, keepdims=True)
            la = pick(ea_ref[n])
            lb = pick(eb_ref[n])
            wts[0] = g_w * (1.0 / (1.0 + jnp.exp(lb - la)))
            wts[1] = g_w * (1.0 / (1.0 + jnp.exp(la - lb)))

        x = hb[...]
        a = _bdot(x, wg_ref[...])
        hid = (a * jax.nn.sigmoid(a)) * _bdot(x, wu_ref[...])
        y = _bdot(hid.astype(BF16), wd_ref[...])

        @pl.when(k == 0)
        def _():
            acc[...] = wts[0] * y

        @pl.when(k == 1)
        def _():
            o_ref[...] = _to_token_tiles(acc[...] + wts[1] * y)

    @pl.when((used_ref[n] == 0) & (k == 1))
    def _():
        o_ref[...] = jnp.zeros_like(o_ref)


def _moe(hs, plan, w_gate, w_up, w_down, r_hi, r_lo, r_b):
    n_slots = hs.shape[0]
    ne, d, de = w_gate.shape
    tm = SLOT_TILE
    n_tiles = n_slots // tm
    slot = pl.BlockSpec((tm, 8, LANES), lambda n, k, *_: (n, 0, 0))
    expert = lambda shp: pl.BlockSpec(
        (None,) + shp, lambda n, k, ea, eb, grp, used: (jnp.where(k == 0, ea[n], eb[n]), 0, 0))
    const = lambda shp: pl.BlockSpec(shp, lambda n, k, *_: tuple(0 for _ in shp))
    return pl.pallas_call(
        _moe_kernel,
        grid_spec=pltpu.PrefetchScalarGridSpec(
            num_scalar_prefetch=4,
            grid=(n_tiles, 2),
            in_specs=[slot, expert((d, de)), expert((d, de)), expert((de, d)),
                      const((d, LANES)), const((d, LANES)), const((1, LANES))],
            out_specs=slot,
            scratch_shapes=[pltpu.VMEM((tm, d), BF16), pltpu.VMEM((2, tm, 1), F32),
                            pltpu.VMEM((tm, d), F32)]),
        out_shape=jax.ShapeDtypeStruct((n_slots, 8, LANES), F32),
        compiler_params=_params(("arbitrary", "arbitrary")),
        name="moe_experts",
    )(*plan, hs, w_gate, w_up, w_down, r_hi, r_lo, r_b)


def _combine_kernel(dest_ref, ys_ref, x_ref, mod_ref, fg_ref, o_ref, buf, sem, *, final):
    tm = x_ref.shape[0]
    _row_dma_loop(tm, lambda r: pltpu.make_async_copy(
        ys_ref.at[dest_ref[0, r]], buf.at[r], sem).start())
    pltpu.make_async_copy(ys_ref.at[pl.ds(0, tm)], buf, sem).wait()
    x = x_ref[...] + mod_ref[5:6, :] * _from_token_tiles(buf[...])
    if final:
        x = x * lax.rsqrt(jnp.mean(x * x, -1, keepdims=True) + EPS) * fg_ref[...]
    o_ref[...] = x


def _combine(ys, dest, x, mod, final_g, final):
    b, s, d = x.shape
    tm = TOKEN_TILE
    nt = s // tm
    return pl.pallas_call(
        functools.partial(_combine_kernel, final=final),
        grid=(b * nt,),
        in_specs=[pl.BlockSpec((None, 1, tm), lambda i: (i, 0, 0), memory_space=pltpu.SMEM),
                  pl.BlockSpec(memory_space=pl.ANY),
                  pl.BlockSpec((None, tm, d), lambda i: (i // nt, i % nt, 0)),
                  pl.BlockSpec((None, 6, d), lambda i: (i // nt, 0, 0)),
                  pl.BlockSpec((1, d), lambda i: (0, 0))],
        out_specs=pl.BlockSpec((None, tm, d), lambda i: (i // nt, i % nt, 0)),
        out_shape=jax.ShapeDtypeStruct((b, s, d), F32),
        scratch_shapes=[pltpu.VMEM((tm, 8, LANES), F32), pltpu.SemaphoreType.DMA(())],
        compiler_params=_params(("arbitrary",)),
        name="moe_combine",
    )(dest, ys, x, mod, final_g.reshape(1, d))


def _routing_plan(cls, rank, counts, n_tokens):
    tm = SLOT_TILE
    n_tiles = n_tokens // tm + N_CLASSES
    counts = counts[:N_CLASSES, 0].astype(jnp.int32)
    tiles = (counts + tm - 1) // tm
    tile_end = jnp.cumsum(tiles)
    classes = jnp.arange(N_CLASSES, dtype=jnp.int32)
    first_slot = (tile_end - tiles) * tm
    dest = jnp.sum(jnp.where(cls[..., None] == classes, first_slot, 0), -1) + rank
    ids = jnp.arange(n_tiles, dtype=jnp.int32)
    used = ids < tile_end[-1]
    tile_cls = jnp.sum(
        (jnp.minimum(ids, tile_end[-1] - 1)[:, None] >= tile_end).astype(jnp.int32), -1)
    grp = tile_cls // PAIRS_PER_GROUP
    pair = tile_cls % PAIRS_PER_GROUP
    pair_a = jnp.asarray([0, 0, 0, 1, 1, 2], jnp.int32)
    pair_b = jnp.asarray([1, 2, 3, 2, 3, 3], jnp.int32)
    eb = grp * EXPERTS_PER_GROUP + pair_b[pair]
    ea = jnp.where(used, grp * EXPERTS_PER_GROUP + pair_a[pair], eb)
    return dest, (ea, eb, grp, used.astype(jnp.int32))


@jax.jit
def kernel(x, c, rel_bias, norm1_g, norm2_g, w_ada, b_ada, ev_w_in, ev_w_out, ev_gmlp_ln_g, ev_gmlp_ln_b, ev_w_s, ev_b_s, od_w_in, od_w_out, od_w_dw, od_b_dw, od_conv_ln_g, od_conv_ln_b, moe_w_group, moe_b_group, moe_w_router, moe_b_router, moe_w_gate, moe_w_up, moe_w_down, final_norm_g):
    b, s, d = x.shape
    depth = w_ada.shape[0]
    mods = _ada(c, w_ada, b_ada).reshape(depth, b, 6, d)
    biases = [_branch_bias(rel_bias, dil) for _, dil in DILATED_BRANCHES]
    for layer in range(depth):
        j = layer // 2
        mod = mods[layer]
        if layer % 2 == 0:
            dils = [dil for _, dil in DILATED_BRANCHES]
            proj, *grouped = _in_proj(x, norm1_g[layer], mod, ev_w_in[j].astype(BF16),
                                      [dil for dil in dils if dil > 1])
            views = {1: (proj, proj.shape[-1] // WIDTH)}
            views.update({dil: (g, 3) for dil, g in zip([dil for dil in dils if dil > 1], grouped)})
            outs = [_dilated_branch(views[dil][0], bias, dil, views[dil][1])
                    for bias, dil in zip(biases, dils)]
            left = _combine_branches(outs, dils, s)
            right = _gmlp(proj, ev_gmlp_ln_g[j], ev_gmlp_ln_b[j], ev_w_s[j], ev_b_s[j])
            w_out = ev_w_out[j]
        else:
            proj = _in_proj(x, norm1_g[layer], mod, od_w_in[j].astype(BF16))
            left = _conv_module(proj, od_w_dw[j], od_b_dw[j], od_conv_ln_g[j], od_conv_ln_b[j])
            right = _stick_breaking(proj)
            w_out = od_w_out[j]
        r_hi, r_lo, r_b = _router_weights(moe_w_group[layer], moe_b_group[layer],
                                          moe_w_router[layer], moe_b_router[layer])
        x, h, cls, rank, counts = _out_proj(left, right, w_out.astype(BF16), x, mod,
                                            norm2_g[layer], r_hi, r_lo, r_b)
        dest, plan = _routing_plan(cls, rank, counts, b * s)
        hs = _dispatch(h, dest, (b * s // SLOT_TILE + N_CLASSES) * SLOT_TILE)
        ys = _moe(hs, plan, moe_w_gate[layer].astype(BF16), moe_w_up[layer].astype(BF16),
                  moe_w_down[layer].astype(BF16), r_hi, r_lo, r_b)
        x = _combine(ys, dest, x, mod, final_norm_g, layer == depth - 1)
    return x
```

```python
import functools
import math

import numpy as np
import jax
import jax.numpy as jnp
from jax import lax
from jax.experimental import pallas as pl
from jax.experimental.pallas import tpu as pltpu

F32 = jnp.float32
BF16 = jnp.bfloat16

HEAD_DIM = 64
HEADS = 8
WIDTH = HEADS * HEAD_DIM
PAIR = 2 * HEAD_DIM
BLK = 128
DILATED_BRANCHES = ((128, 1), (512, 4), (2048, 16))
N_BUCKETS = 32
MAX_DISTANCE = 2048
CONV_WIDTH = 31
CONV_HALO = 32
B_GROUPS = 4
N_GROUPS = 4
EXPERTS_PER_GROUP = 4
N_EXPERTS = N_GROUPS * EXPERTS_PER_GROUP
PAIRS_PER_GROUP = EXPERTS_PER_GROUP * (EXPERTS_PER_GROUP - 1) // 2
N_CLASSES = N_GROUPS * PAIRS_PER_GROUP
TOKEN_TILE = 512
SLOT_TILE = 512
EPS = 1e-6
NEG = -1e30
LOG2E = 1.4426950408889634
EXP_IS_ZERO_BELOW = -104.0
LANES = 128
SUBLANES = 8
VMEM_LIMIT = 48 * 1024 * 1024


def _params(sem):
    return pltpu.CompilerParams(dimension_semantics=sem, vmem_limit_bytes=VMEM_LIMIT)


def _bdot(a, b):
    return jnp.dot(a, b, preferred_element_type=F32)


def _bdot_nt(a, b):
    return lax.dot_general(a, b, (((1,), (1,)), ((), ())), preferred_element_type=F32)


def _split(a):
    hi = a.astype(BF16)
    lo = (a - hi.astype(F32)).astype(BF16)
    return hi, lo


def _dot3(a, w_hi, w_lo):
    a_hi, a_lo = _split(a)
    return _bdot(a_hi, w_hi) + (_bdot(a_hi, w_lo) + _bdot(a_lo, w_hi))


def _ada_kernel(c_ref, w_ref, b_ref, o_ref):
    c = c_ref[...]
    sc = c * jax.nn.sigmoid(c)
    w_hi, w_lo = _split(w_ref[...])
    o_ref[...] = _dot3(sc, w_hi, w_lo) + b_ref[...]


def _ada(c, w_ada, b_ada):
    depth, d, n = w_ada.shape
    b = c.shape[0]
    tn = 1024
    return pl.pallas_call(
        _ada_kernel,
        grid=(depth, n // tn),
        in_specs=[
            pl.BlockSpec((b, d), lambda l, j: (0, 0)),
            pl.BlockSpec((None, d, tn), lambda l, j: (l, 0, j)),
            pl.BlockSpec((None, 1, tn), lambda l, j: (l, 0, j)),
        ],
        out_specs=pl.BlockSpec((None, b, tn), lambda l, j: (l, 0, j)),
        out_shape=jax.ShapeDtypeStruct((depth, b, n), F32),
        compiler_params=_params(("arbitrary", "arbitrary")),
        name="ada_mod",
    )(c, w_ada, b_ada.reshape(depth, 1, n))


def _rms_mod(x, g, scale_row, shift_row):
    y = x * lax.rsqrt(jnp.mean(x * x, -1, keepdims=True) + EPS) * g
    return y * (1.0 + scale_row) + shift_row


def _in_kernel(x_ref, g_ref, mod_ref, w_ref, o_ref, *rest, dilations):
    m = mod_ref[...]
    h = _rms_mod(x_ref[...], g_ref[...], m[1:2], m[0:1]).astype(BF16)
    tm = x_ref.shape[0]
    n = w_ref.shape[1]
    per = WIDTH // LANES
    for j in range(n // WIDTH):
        cols = slice(j * WIDTH, (j + 1) * WIDTH)
        y = _bdot(h, w_ref[:, cols])
        o_ref[:, cols] = y.astype(o_ref.dtype)
        if dilations and j < 3:
            for c in range(per):
                rest[-1][j * per + c] = y[:, c * LANES:(c + 1) * LANES]
    for d, out in zip(dilations, rest):
        for r in range(d):
            for c in range(3 * per):
                lo = (r * 3 * per + c) * LANES
                out[:, lo:lo + LANES] = rest[-1][c, pl.ds(r, tm // d, stride=d), :].astype(out.dtype)


def _in_proj(x, g, mod, w_in, dilations=()):
    b, s, d = x.shape
    n = w_in.shape[1]
    tm = TOKEN_TILE
    qkv = 3 * WIDTH
    outs = pl.pallas_call(
        functools.partial(_in_kernel, dilations=tuple(dilations)),
        grid=(b, s // tm),
        in_specs=[
            pl.BlockSpec((None, tm, d), lambda i, t: (i, t, 0)),
            pl.BlockSpec((1, d), lambda i, t: (0, 0)),
            pl.BlockSpec((None, 6, d), lambda i, t: (i, 0, 0)),
            pl.BlockSpec((d, n), lambda i, t: (0, 0)),
        ],
        out_specs=[pl.BlockSpec((None, tm, n), lambda i, t: (i, t, 0))] + [
            pl.BlockSpec((None, tm // dil, dil * qkv), lambda i, t: (i, t, 0)) for dil in dilations],
        out_shape=[jax.ShapeDtypeStruct((b, s, n), BF16)] + [
            jax.ShapeDtypeStruct((b, s // dil, dil * qkv), BF16) for dil in dilations],
        scratch_shapes=[pltpu.VMEM((qkv // LANES, tm, LANES), F32)] if dilations else [],
        compiler_params=_params(("arbitrary", "arbitrary")),
        name="in_proj",
    )(x, g.reshape(1, d), mod, w_in)
    return outs if dilations else outs[0]


def _t5_bucket(dist):
    max_exact = N_BUCKETS // 2
    d = np.maximum(dist, 1).astype(np.float32)
    large = max_exact + (np.log(d / max_exact) / np.log(MAX_DISTANCE / max_exact)
                         * (N_BUCKETS - max_exact)).astype(np.int32)
    large = np.minimum(large, N_BUCKETS - 1)
    return np.where(dist < max_exact, dist, large).astype(np.int32)


def _branch_bias(rel_table, dilation):
    qi = np.arange(BLK)[:, None]
    kj = np.arange(2 * BLK)[None, :]
    bucket = _t5_bucket(np.clip(qi + BLK - kj, 0, None) * dilation)
    onehot = np.eye(N_BUCKETS, dtype=np.float32)[bucket]
    return jnp.einsum("qkb,bh->hqk", onehot, rel_table.astype(F32), precision=lax.Precision.HIGHEST)


def _dil_kernel(q_ref, kp_ref, kc_ref, vp_ref, vc_ref, bias_ref, o_ref, m_ref, l_ref,
                kbuf, vbuf, *, tq):
    n = pl.program_id(2)
    nsub = tq // BLK
    kbuf[0:BLK] = kp_ref[...]
    kbuf[BLK:] = kc_ref[...]
    vbuf[0:BLK] = vp_ref[...]
    vbuf[BLK:] = vc_ref[...]
    qi = lax.broadcasted_iota(jnp.int32, (BLK, 2 * BLK), 0)
    kj = lax.broadcasted_iota(jnp.int32, (BLK, 2 * BLK), 1)
    dist = qi + BLK - kj
    in_window = (dist >= 0) & (dist <= BLK)
    lane = lax.broadcasted_iota(jnp.int32, (BLK, PAIR), 1)
    scale = HEAD_DIM ** -0.5
    for i in range(nsub):
        first_key = jnp.where(n * nsub + i == 0, BLK, 0)
        valid = in_window & (kj >= first_key)
        rows = slice(i * BLK, (i + 1) * BLK)
        krows = slice(i * BLK, (i + 2) * BLK)
        m_tile = jnp.zeros((BLK, LANES), F32)
        l_tile = jnp.ones((BLK, LANES), F32)
        for p in range(HEADS // 2):
            cols = slice(p * PAIR, (p + 1) * PAIR)
            q_pair = q_ref[rows, cols]
            k_pair = kbuf[krows, cols]
            v_pair = vbuf[krows, cols]
            outs = []
            for hh in range(2):
                h = 2 * p + hh
                head_lanes = (lane >= HEAD_DIM) if hh else (lane < HEAD_DIM)
                qm = jnp.where(head_lanes, q_pair, jnp.zeros_like(q_pair))
                s = _bdot_nt(qm, k_pair) * scale
                logit = jnp.where(valid, s + bias_ref[h], NEG)
                mx = jnp.max(logit, -1, keepdims=True)
                pr = jnp.exp(logit - mx)
                den = jnp.sum(pr, -1, keepdims=True)
                outs.append(_bdot(pr.astype(BF16), v_pair) / den)
                m_tile = jnp.where(lane == h, mx, m_tile)
                l_tile = jnp.where(lane == h, den, l_tile)
            o_ref[rows, cols] = jnp.where(lane < HEAD_DIM, outs[0], outs[1]).astype(o_ref.dtype)
        m_ref[rows, :] = m_tile
        l_ref[rows, :] = l_tile


def _dilated_branch(view, bias, dilation, ncol):
    b, sub, _ = view.shape
    tq = min(512, sub)
    nsub = tq // BLK
    cur = lambda c: pl.BlockSpec((None, tq, WIDTH), lambda i, r, t: (i, t, r * ncol + c))
    prev = lambda c: pl.BlockSpec(
        (None, BLK, WIDTH), lambda i, r, t: (i, jnp.maximum(t * nsub - 1, 0), r * ncol + c))
    return pl.pallas_call(
        functools.partial(_dil_kernel, tq=tq),
        grid=(b, dilation, sub // tq),
        in_specs=[cur(0), prev(1), cur(1), prev(2), cur(2),
                  pl.BlockSpec((HEADS, BLK, 2 * BLK), lambda i, r, t: (0, 0, 0))],
        out_specs=[
            pl.BlockSpec((None, tq, WIDTH), lambda i, r, t: (i, t, r)),
            pl.BlockSpec((None, tq, LANES), lambda i, r, t: (i, t, r)),
            pl.BlockSpec((None, tq, LANES), lambda i, r, t: (i, t, r)),
        ],
        out_shape=[
            jax.ShapeDtypeStruct((b, sub, dilation * WIDTH), BF16),
            jax.ShapeDtypeStruct((b, sub, dilation * LANES), F32),
            jax.ShapeDtypeStruct((b, sub, dilation * LANES), F32),
        ],
        scratch_shapes=[pltpu.VMEM((BLK + tq, WIDTH), BF16), pltpu.VMEM((BLK + tq, WIDTH), BF16)],
        compiler_params=_params(("arbitrary", "arbitrary", "arbitrary")),
        name=f"dilated_d{dilation}",
    )(view, view, view, view, view, bias)


def _comb_kernel(*refs, dilations):
    nb = len(dilations)
    o_refs, m_refs, l_refs = refs[:nb], refs[nb:2 * nb], refs[2 * nb:3 * nb]
    e_ref, out_ref, o_scr, s_scr = refs[3 * nb:]
    tm = out_ref.shape[0]
    per = WIDTH // LANES
    os, ms, ls = [], [], []
    for bi, d in enumerate(dilations):
        if d == 1:
            os.append([o_refs[bi][:, c * LANES:(c + 1) * LANES].astype(F32) for c in range(per)])
            ms.append(m_refs[bi][...])
            ls.append(l_refs[bi][...])
            continue
        rows = lambda r: pl.ds(r, tm // d, stride=d)
        for r in range(d):
            for c in range(per):
                lo = (r * per + c) * LANES
                o_scr[bi, c, rows(r), :] = o_refs[bi][:, lo:lo + LANES].astype(F32)
            s_scr[bi, 0, rows(r), :] = m_refs[bi][:, r * LANES:(r + 1) * LANES]
            s_scr[bi, 1, rows(r), :] = l_refs[bi][:, r * LANES:(r + 1) * LANES]
        os.append([o_scr[bi, c] for c in range(per)])
        ms.append(s_scr[bi, 0])
        ls.append(s_scr[bi, 1])
    m_all = functools.reduce(jnp.maximum, ms)
    es = [l * jnp.exp(m - m_all) for l, m in zip(ls, ms)]
    tot = functools.reduce(lambda x, y: x + y, es)
    acc = [None] * per
    for e, o in zip(es, os):
        w_hi, w_lo = _split(e / tot)
        w = _bdot(w_hi, e_ref[...]) + _bdot(w_lo, e_ref[...])
        for c in range(per):
            term = w[:, c * LANES:(c + 1) * LANES] * o[c]
            acc[c] = term if acc[c] is None else acc[c] + term
    for c in range(per):
        out_ref[:, c * LANES:(c + 1) * LANES] = acc[c].astype(out_ref.dtype)


def _combine_branches(outs, dilations, s):
    b = outs[0][0].shape[0]
    tm = TOKEN_TILE
    nb = len(dilations)
    expand = np.zeros((LANES, WIDTH), np.float32)
    for h in range(HEADS):
        expand[h, h * HEAD_DIM:(h + 1) * HEAD_DIM] = 1.0
    grouped = lambda w: [pl.BlockSpec((None, tm // d, d * w), lambda i, t: (i, t, 0)) for d in dilations]
    return pl.pallas_call(
        functools.partial(_comb_kernel, dilations=tuple(dilations)),
        grid=(b, s // tm),
        in_specs=grouped(WIDTH) + grouped(LANES) + grouped(LANES)
        + [pl.BlockSpec((LANES, WIDTH), lambda i, t: (0, 0))],
        out_specs=pl.BlockSpec((None, tm, WIDTH), lambda i, t: (i, t, 0)),
        out_shape=jax.ShapeDtypeStruct((b, s, WIDTH), BF16),
        scratch_shapes=[pltpu.VMEM((nb, WIDTH // LANES, tm, LANES), F32),
                        pltpu.VMEM((nb, 2, tm, LANES), F32)],
        compiler_params=_params(("arbitrary", "arbitrary")),
        name="dilated_combine",
    )(*[o for o, _, _ in outs], *[m for _, m, _ in outs], *[l for _, _, l in outs],
      jnp.asarray(expand, BF16))


def _gelu(x):
    return 0.5 * x * (1.0 + jnp.tanh(math.sqrt(2.0 / math.pi) * (x + 0.044715 * (x * x * x))))


def _gmlp_kernel(u_ref, v_ref, g_ref, b_ref, ws_ref, bs_ref, o_ref):
    tm = u_ref.shape[0]
    u = _gelu(u_ref[...].astype(F32))
    v = _gelu(v_ref[...].astype(F32))
    mu = jnp.mean(v, -1, keepdims=True)
    vc = v - mu
    var = jnp.mean(vc * vc, -1, keepdims=True)
    v = (vc * lax.rsqrt(var + EPS) * g_ref[...] + b_ref[...]).astype(BF16)
    r = lax.broadcasted_iota(jnp.int32, (BLK, BLK), 0)
    c = lax.broadcasted_iota(jnp.int32, (BLK, BLK), 1)
    for g in range(B_GROUPS):
        w = jnp.where(c <= r, ws_ref[g], 0.0).astype(BF16)
        cols = slice(g * BLK, (g + 1) * BLK)
        for ch in range(tm // BLK):
            rows = slice(ch * BLK, (ch + 1) * BLK)
            mixed = _bdot(w, v[rows, cols]) + bs_ref[g]
            o_ref[rows, cols] = (u[rows, cols] * mixed).astype(o_ref.dtype)


def _gmlp(proj, ln_g, ln_b, w_s, b_s):
    b, s, n = proj.shape
    tm = 512
    return pl.pallas_call(
        _gmlp_kernel,
        grid=(b, s // tm),
        in_specs=[
            pl.BlockSpec((None, tm, WIDTH), lambda i, t: (i, t, 3)),
            pl.BlockSpec((None, tm, WIDTH), lambda i, t: (i, t, 4)),
            pl.BlockSpec((1, WIDTH), lambda i, t: (0, 0)),
            pl.BlockSpec((1, WIDTH), lambda i, t: (0, 0)),
            pl.BlockSpec((B_GROUPS, BLK, BLK), lambda i, t: (0, 0, 0)),
            pl.BlockSpec((B_GROUPS, BLK, 1), lambda i, t: (0, 0, 0)),
        ],
        out_specs=pl.BlockSpec((None, tm, WIDTH), lambda i, t: (i, t, 0)),
        out_shape=jax.ShapeDtypeStruct((b, s, WIDTH), BF16),
        compiler_params=_params(("arbitrary", "arbitrary")),
        name="gmlp_gate",
    )(proj, proj, ln_g.reshape(1, WIDTH), ln_b.reshape(1, WIDTH), w_s,
      b_s.reshape(B_GROUPS, BLK, 1))


def _conv_kernel(a_ref, g_ref, ah_ref, gh_ref, w_ref, bdw_ref, lng_ref, lnb_ref, o_ref, hbuf):
    tm = a_ref.shape[0]
    t = pl.program_id(1)
    halo = ah_ref[...].astype(F32) * jax.nn.sigmoid(gh_ref[...].astype(F32))
    hbuf[0, 0:CONV_HALO] = jnp.where(t == 0, 0.0, halo)
    hbuf[0, CONV_HALO:] = a_ref[...].astype(F32) * jax.nn.sigmoid(g_ref[...].astype(F32))
    n_shifted = tm + CONV_HALO - SUBLANES
    for s in range(1, SUBLANES):
        hbuf[s, 0:n_shifted] = hbuf[0, pl.ds(s, n_shifted), :]
    rc = 32
    off = CONV_HALO - (CONV_WIDTH - 1)

    def chunk(ci, carry):
        base = pl.multiple_of(ci * rc, rc)
        accs = [jnp.zeros((rc, WIDTH), F32) + bdw_ref[...]] + [None] * 3
        for k in range(CONV_WIDTH):
            s = (off + k) % SUBLANES
            term = w_ref[k:k + 1, :] * hbuf[s, pl.ds(base + (off + k - s), rc), :]
            accs[k % 4] = term if accs[k % 4] is None else accs[k % 4] + term
        acc = (accs[0] + accs[1]) + (accs[2] + accs[3])
        mu = jnp.mean(acc, -1, keepdims=True)
        xc = acc - mu
        var = jnp.mean(xc * xc, -1, keepdims=True)
        y = xc * lax.rsqrt(var + EPS) * lng_ref[...] + lnb_ref[...]
        o_ref[pl.ds(base, rc), :] = (y * jax.nn.sigmoid(y)).astype(o_ref.dtype)
        return carry

    lax.fori_loop(0, tm // rc, chunk, 0)


def _conv_module(proj, w_dw, b_dw, ln_g, ln_b):
    b, s, n = proj.shape
    tm = 512
    hb = tm // CONV_HALO
    row = lambda v: v.reshape(1, WIDTH)
    halo = lambda c: pl.BlockSpec(
        (None, CONV_HALO, WIDTH), lambda i, t: (i, jnp.maximum(t * hb - 1, 0), c))
    return pl.pallas_call(
        _conv_kernel,
        grid=(b, s // tm),
        in_specs=[
            pl.BlockSpec((None, tm, WIDTH), lambda i, t: (i, t, 0)),
            pl.BlockSpec((None, tm, WIDTH), lambda i, t: (i, t, 1)),
            halo(0), halo(1),
            pl.BlockSpec((CONV_WIDTH, WIDTH), lambda i, t: (0, 0)),
            pl.BlockSpec((1, WIDTH), lambda i, t: (0, 0)),
            pl.BlockSpec((1, WIDTH), lambda i, t: (0, 0)),
            pl.BlockSpec((1, WIDTH), lambda i, t: (0, 0)),
        ],
        out_specs=pl.BlockSpec((None, tm, WIDTH), lambda i, t: (i, t, 0)),
        out_shape=jax.ShapeDtypeStruct((b, s, WIDTH), BF16),
        scratch_shapes=[pltpu.VMEM((SUBLANES, CONV_HALO + tm, WIDTH), F32)],
        compiler_params=_params(("arbitrary", "arbitrary")),
        name="conv_module",
    )(proj, proj, proj, proj, w_dw, row(b_dw), row(ln_g), row(ln_b))


def _stick_kernel(q_ref, k_ref, v_ref, mm_ref, o_ref, q2_ref, carry_ref, acc_ref):
    i = pl.program_id(1)
    npair = WIDTH // PAIR
    lane = lax.broadcasted_iota(jnp.int32, (BLK, PAIR), 1)
    lo_lanes = lane < HEAD_DIM
    row2 = lax.broadcasted_iota(jnp.int32, (2 * BLK, BLK), 0)
    col2 = lax.broadcasted_iota(jnp.int32, (2 * BLK, BLK), 1)
    causal2 = col2 < (row2 & (BLK - 1))
    scale = HEAD_DIM ** -0.5
    for p in range(npair):
        qp = q_ref[:, p * PAIR:(p + 1) * PAIR] * scale
        zero = jnp.zeros_like(qp)
        q2_ref[p] = jnp.concatenate(
            [jnp.where(lo_lanes, qp, zero), jnp.where(lo_lanes, zero, qp)], axis=0)
    carry_ref[...] = jnp.zeros_like(carry_ref)
    acc_ref[...] = jnp.zeros_like(acc_ref)

    def blocks(js, diag):
        cols = lambda p: slice(p * PAIR, (p + 1) * PAIR)
        rows = lambda j: pl.ds(pl.multiple_of(j * BLK, BLK), BLK)

        def scores(j, p):
            return _bdot_nt(q2_ref[p], k_ref[rows(j), cols(p)])

        def keep_sums(z):
            soft = jnp.log(1.0 + jnp.exp2(jnp.abs(z) * -LOG2E))
            log_beta = jnp.minimum(z, 0.0) - soft
            log_keep = log_beta - z
            if diag:
                log_keep = jnp.where(causal2, log_keep, 0.0)
            return log_beta, _bdot(log_keep.astype(BF16), mm_ref[...])

        def weighted_values(j, p, log_beta, sums):
            carry = carry_ref[p]
            att = jnp.exp(log_beta + (sums[:, :BLK] + carry))
            if diag:
                att = jnp.where(causal2, att, 0.0)
            carry_ref[p] = carry + sums[:, BLK:]
            att = att.astype(BF16)
            vb = v_ref[rows(j), cols(p)]
            zero = jnp.zeros_like(vb)
            v2 = jnp.concatenate([jnp.where(lo_lanes, vb, zero), jnp.where(lo_lanes, zero, vb)], axis=0)
            return _bdot(jnp.concatenate([att[:BLK], att[BLK:]], axis=1), v2)

        chains = [(j, p) for j in js for p in range(npair)]
        z, ks, pv = {}, {}, {}
        for step in range(len(chains) + 3):
            if step < len(chains):
                z[step] = scores(*chains[step])
            if 0 <= step - 1 < len(chains):
                ks[step - 1] = keep_sums(z.pop(step - 1))
            if 0 <= step - 2 < len(chains):
                pv[step - 2] = weighted_values(*chains[step - 2], *ks.pop(step - 2))
            if 0 <= step - 3 < len(chains):
                acc_ref[chains[step - 3][1]] += pv.pop(step - 3)

    blocks([i], True)

    def live():
        c = carry_ref[...]
        worst = jnp.max(jnp.max(c, axis=0), axis=0, keepdims=True)
        return (worst[0, 0] >= EXP_IS_ZERO_BELOW).astype(jnp.int32)

    def two_blocks(state):
        t, _ = state
        j = i - 1 - 2 * t
        blocks([j, j - 1], False)
        return t + 1, live()

    _, alive = lax.while_loop(lambda st: (st[0] < i // 2) & (st[1] == 1), two_blocks,
                              (jnp.int32(0), jnp.int32(1)))

    @pl.when((i % 2 == 1) & (alive == 1))
    def _():
        blocks([0], False)

    for p in range(npair):
        o_ref[:, p * PAIR:(p + 1) * PAIR] = acc_ref[p].astype(o_ref.dtype)


def _suffix_sum_matrix():
    sp = np.arange(BLK)[:, None]
    sc = np.arange(BLK)[None, :]
    return jnp.asarray(
        np.concatenate([(sp > sc).astype(np.float32), np.ones((BLK, BLK), np.float32)], axis=1), BF16)


def _stick_breaking(proj):
    b, s, n = proj.shape
    npair = WIDTH // PAIR
    return pl.pallas_call(
        _stick_kernel,
        grid=(b, s // BLK),
        in_specs=[
            pl.BlockSpec((None, BLK, WIDTH), lambda i, t: (i, t, 2)),
            pl.BlockSpec((None, s, WIDTH), lambda i, t: (i, 0, 3)),
            pl.BlockSpec((None, s, WIDTH), lambda i, t: (i, 0, 4)),
            pl.BlockSpec((BLK, 2 * BLK), lambda i, t: (0, 0)),
        ],
        out_specs=pl.BlockSpec((None, BLK, WIDTH), lambda i, t: (i, t, 0)),
        out_shape=jax.ShapeDtypeStruct((b, s, WIDTH), BF16),
        scratch_shapes=[pltpu.VMEM((npair, 2 * BLK, PAIR), BF16),
                        pltpu.VMEM((npair, 2 * BLK, BLK), F32),
                        pltpu.VMEM((npair, BLK, PAIR), F32)],
        compiler_params=_params(("arbitrary", "arbitrary")),
        name="stick_breaking",
    )(proj, proj, proj, _suffix_sum_matrix())


def _lane_min_index(mask, lane):
    return jnp.min(jnp.where(mask, lane, float(LANES)), -1, keepdims=True)


def _out_kernel(l_ref, r_ref, w_ref, x_ref, mod_ref, g_ref, rh_ref, rl_ref, rb_ref, tri_ref,
                xo_ref, h_ref, cls_ref, rank_ref, cnt_ref, seen):
    tm = x_ref.shape[0]

    @pl.when((pl.program_id(0) == 0) & (pl.program_id(1) == 0))
    def _():
        seen[...] = jnp.zeros_like(seen)

    m = mod_ref[...]
    mix = _bdot(l_ref[...], w_ref[0:WIDTH, :]) + _bdot(r_ref[...], w_ref[WIDTH:, :])
    x = x_ref[...] + m[2:3] * mix
    xo_ref[...] = x
    h = _rms_mod(x, g_ref[...], m[4:5], m[3:4])
    h_ref[...] = h
    logits = _dot3(h, rh_ref[...], rl_ref[...]) + rb_ref[...]
    lane = lax.broadcasted_iota(jnp.int32, logits.shape, 1).astype(F32)
    is_group = (lane >= N_EXPERTS) & (lane < N_EXPERTS + N_GROUPS)
    glog = jnp.where(is_group, logits, -jnp.inf)
    gmax = jnp.max(glog, -1, keepdims=True)
    g_idx = _lane_min_index(glog == gmax, lane) - N_EXPERTS
    in_group = (lane >= g_idx * EXPERTS_PER_GROUP) & (lane < (g_idx + 1) * EXPERTS_PER_GROUP)
    ev = jnp.where(in_group, logits, -jnp.inf)
    v1 = jnp.max(ev, -1, keepdims=True)
    i1 = _lane_min_index(ev == v1, lane)
    ev2 = jnp.where(lane == i1, -jnp.inf, ev)
    v2 = jnp.max(ev2, -1, keepdims=True)
    i2 = _lane_min_index(ev2 == v2, lane)
    lo = jnp.minimum(i1, i2) - g_idx * EXPERTS_PER_GROUP
    hi = jnp.maximum(i1, i2) - g_idx * EXPERTS_PER_GROUP
    cls = g_idx * PAIRS_PER_GROUP + (lo * (7.0 - lo) * 0.5 + (hi - lo - 1.0))
    cls_t = jnp.transpose(jnp.broadcast_to(cls, (tm, LANES)))
    onehot_t = lax.broadcasted_iota(jnp.int32, (LANES, tm), 0).astype(F32) == cls_t
    ones_t = jnp.where(onehot_t, 1.0, 0.0)
    earlier = _bdot(ones_t.astype(BF16), tri_ref[...]) + seen[...]
    cls_ref[...] = cls_t[0:1, :].astype(jnp.int32)
    rank_ref[...] = jnp.sum(jnp.where(onehot_t, earlier, 0.0), 0, keepdims=True).astype(jnp.int32)
    seen[...] += jnp.sum(ones_t, 1, keepdims=True)
    cnt_ref[...] = seen[...]


def _out_proj(left, right, w_out, x, mod, g, r_hi, r_lo, r_b):
    b, s, d = x.shape
    tm = TOKEN_TILE
    nt = s // tm
    tile = lambda w: pl.BlockSpec((None, tm, w), lambda i, t: (i, t, 0))
    const = lambda shp: pl.BlockSpec(shp, lambda i, t: tuple(0 for _ in shp))
    row = pl.BlockSpec((None, 1, tm), lambda i, t: (i * nt + t, 0, 0))
    tri = np.arange(tm)[:, None] < np.arange(tm)[None, :]
    return pl.pallas_call(
        _out_kernel,
        grid=(b, nt),
        in_specs=[tile(WIDTH), tile(WIDTH), const((2 * WIDTH, d)), tile(d),
                  pl.BlockSpec((None, 6, d), lambda i, t: (i, 0, 0)),
                  const((1, d)), const((d, LANES)), const((d, LANES)), const((1, LANES)),
                  const((tm, tm))],
        out_specs=[tile(d), tile(d), row, row, const((LANES, 1))],
        out_shape=[jax.ShapeDtypeStruct((b, s, d), F32),
                   jax.ShapeDtypeStruct((b, s, d), F32),
                   jax.ShapeDtypeStruct((b * nt, 1, tm), jnp.int32),
                   jax.ShapeDtypeStruct((b * nt, 1, tm), jnp.int32),
                   jax.ShapeDtypeStruct((LANES, 1), F32)],
        scratch_shapes=[pltpu.VMEM((LANES, 1), F32)],
        compiler_params=_params(("arbitrary", "arbitrary")),
        name="out_proj_router",
    )(left, right, w_out, x, mod, g.reshape(1, d), r_hi, r_lo, r_b, jnp.asarray(tri, BF16))


def _router_weights(w_group, b_group, w_router, b_router):
    d = w_group.shape[0]
    w = jnp.concatenate([jnp.transpose(w_router, (1, 0, 2)).reshape(d, N_EXPERTS), w_group], axis=1)
    w = jnp.pad(w.astype(F32), ((0, 0), (0, LANES - w.shape[1])))
    bias = jnp.concatenate([b_router.reshape(N_EXPERTS), b_group]).astype(F32)
    bias = jnp.pad(bias, (0, LANES - bias.shape[0])).reshape(1, LANES)
    hi, lo = _split(w)
    return hi, lo, bias


def _to_token_tiles(x):
    n = x.shape[0]
    parts = [x[:, g * LANES:(g + 1) * LANES].reshape(n // 8, 8, LANES) for g in range(8)]
    return jnp.swapaxes(jnp.stack(parts, axis=1), 1, 2).reshape(n, 8, LANES)


def _from_token_tiles(v):
    n = v.shape[0]
    w = jnp.swapaxes(v.reshape(n // 8, 8, 8, LANES), 1, 2)
    return jnp.concatenate([w[:, g].reshape(n, LANES) for g in range(8)], axis=-1)


def _row_dma_loop(n, start_row):
    def body(r, carry):
        start_row(r)
        return carry
    lax.fori_loop(0, n, body, 0, unroll=8)


def _dispatch_kernel(dest_ref, h_ref, zeros_ref, hs_ref, buf, sem):
    del zeros_ref
    tm = h_ref.shape[0]
    buf[...] = _to_token_tiles(h_ref[...])
    _row_dma_loop(tm, lambda r: pltpu.make_async_copy(
        buf.at[r], hs_ref.at[dest_ref[0, r]], sem).start())
    pltpu.make_async_copy(buf, hs_ref.at[pl.ds(0, tm)], sem).wait()


def _dispatch(h, dest, n_slots):
    b, s, d = h.shape
    tm = TOKEN_TILE
    nt = s // tm
    return pl.pallas_call(
        _dispatch_kernel,
        grid=(b * nt,),
        in_specs=[pl.BlockSpec((None, 1, tm), lambda i: (i, 0, 0), memory_space=pltpu.SMEM),
                  pl.BlockSpec((None, tm, d), lambda i: (i // nt, i % nt, 0)),
                  pl.BlockSpec(memory_space=pl.ANY)],
        out_specs=pl.BlockSpec(memory_space=pl.ANY),
        out_shape=jax.ShapeDtypeStruct((n_slots, 8, LANES), F32),
        scratch_shapes=[pltpu.VMEM((tm, 8, LANES), F32), pltpu.SemaphoreType.DMA(())],
        input_output_aliases={2: 0},
        compiler_params=_params(("arbitrary",)),
        name="moe_dispatch",
    )(dest, h, jnp.zeros((n_slots, 8, LANES), F32))


def _moe_kernel(ea_ref, eb_ref, grp_ref, used_ref, hs_ref, wg_ref, wu_ref, wd_ref,
                rh_ref, rl_ref, rb_ref, o_ref, hb, wts, acc):
    n = pl.program_id(0)
    k = pl.program_id(1)

    @pl.when(used_ref[n] == 1)
    def _():
        @pl.when(k == 0)
        def _():
            h = _from_token_tiles(hs_ref[...])
            hb[...] = h.astype(BF16)
            logits = _dot3(h, rh_ref[...], rl_ref[...]) + rb_ref[...]
            lane = lax.broadcasted_iota(jnp.int32, logits.shape, 1)
            pick = lambda idx: jnp.sum(jnp.where(lane == idx, logits, 0.0), -1, keepdims=True)
            is_group = (lane >= N_EXPERTS) & (lane < N_EXPERTS + N_GROUPS)
            g_w = 1.0 / jnp.sum(
                jnp.where(is_group, jnp.exp(logits - pick(N_EXPERTS + grp_ref[n])), 0.0),
                -1, keepdims=True)
            la = pick(ea_ref[n])
            lb = pick(eb_ref[n])
            wts[0] = g_w * (1.0 / (1.0 + jnp.exp(lb - la)))
            wts[1] = g_w * (1.0 / (1.0 + jnp.exp(la - lb)))

        x = hb[...]
        a = _bdot(x, wg_ref[...])
        hid = (a * jax.nn.sigmoid(a)) * _bdot(x, wu_ref[...])
        y = _bdot(hid.astype(BF16), wd_ref[...])

        @pl.when(k == 0)
        def _():
            acc[...] = wts[0] * y

        @pl.when(k == 1)
        def _():
            o_ref[...] = _to_token_tiles(acc[...] + wts[1] * y)

    @pl.when((used_ref[n] == 0) & (k == 1))
    def _():
        o_ref[...] = jnp.zeros_like(o_ref)


def _moe(hs, plan, w_gate, w_up, w_down, r_hi, r_lo, r_b):
    n_slots = hs.shape[0]
    ne, d, de = w_gate.shape
    tm = SLOT_TILE
    n_tiles = n_slots // tm
    slot = pl.BlockSpec((tm, 8, LANES), lambda n, k, *_: (n, 0, 0))
    expert = lambda shp: pl.BlockSpec(
        (None,) + shp, lambda n, k, ea, eb, grp, used: (jnp.where(k == 0, ea[n], eb[n]), 0, 0))
    const = lambda shp: pl.BlockSpec(shp, lambda n, k, *_: tuple(0 for _ in shp))
    return pl.pallas_call(
        _moe_kernel,
        grid_spec=pltpu.PrefetchScalarGridSpec(
            num_scalar_prefetch=4,
            grid=(n_tiles, 2),
            in_specs=[slot, expert((d, de)), expert((d, de)), expert((de, d)),
                      const((d, LANES)), const((d, LANES)), const((1, LANES))],
            out_specs=slot,
            scratch_shapes=[pltpu.VMEM((tm, d), BF16), pltpu.VMEM((2, tm, 1), F32),
                            pltpu.VMEM((tm, d), F32)]),
        out_shape=jax.ShapeDtypeStruct((n_slots, 8, LANES), F32),
        compiler_params=_params(("arbitrary", "arbitrary")),
        name="moe_experts",
    )(*plan, hs, w_gate, w_up, w_down, r_hi, r_lo, r_b)


def _combine_kernel(dest_ref, ys_ref, x_ref, mod_ref, fg_ref, o_ref, buf, sem, *, final):
    tm = x_ref.shape[0]
    _row_dma_loop(tm, lambda r: pltpu.make_async_copy(
        ys_ref.at[dest_ref[0, r]], buf.at[r], sem).start())
    pltpu.make_async_copy(ys_ref.at[pl.ds(0, tm)], buf, sem).wait()
    x = x_ref[...] + mod_ref[5:6, :] * _from_token_tiles(buf[...])
    if final:
        x = x * lax.rsqrt(jnp.mean(x * x, -1, keepdims=True) + EPS) * fg_ref[...]
    o_ref[...] = x


def _combine(ys, dest, x, mod, final_g, final):
    b, s, d = x.shape
    tm = TOKEN_TILE
    nt = s // tm
    return pl.pallas_call(
        functools.partial(_combine_kernel, final=final),
        grid=(b * nt,),
        in_specs=[pl.BlockSpec((None, 1, tm), lambda i: (i, 0, 0), memory_space=pltpu.SMEM),
                  pl.BlockSpec(memory_space=pl.ANY),
                  pl.BlockSpec((None, tm, d), lambda i: (i // nt, i % nt, 0)),
                  pl.BlockSpec((None, 6, d), lambda i: (i // nt, 0, 0)),
                  pl.BlockSpec((1, d), lambda i: (0, 0))],
        out_specs=pl.BlockSpec((None, tm, d), lambda i: (i // nt, i % nt, 0)),
        out_shape=jax.ShapeDtypeStruct((b, s, d), F32),
        scratch_shapes=[pltpu.VMEM((tm, 8, LANES), F32), pltpu.SemaphoreType.DMA(())],
        compiler_params=_params(("arbitrary",)),
        name="moe_combine",
    )(dest, ys, x, mod, final_g.reshape(1, d))


def _routing_plan(cls, rank, counts, n_tokens):
    tm = SLOT_TILE
    n_tiles = n_tokens // tm + N_CLASSES
    counts = counts[:N_CLASSES, 0].astype(jnp.int32)
    tiles = (counts + tm - 1) // tm
    tile_end = jnp.cumsum(tiles)
    classes = jnp.arange(N_CLASSES, dtype=jnp.int32)
    first_slot = (tile_end - tiles) * tm
    dest = jnp.sum(jnp.where(cls[..., None] == classes, first_slot, 0), -1) + rank
    ids = jnp.arange(n_tiles, dtype=jnp.int32)
    used = ids < tile_end[-1]
    tile_cls = jnp.sum(
        (jnp.minimum(ids, tile_end[-1] - 1)[:, None] >= tile_end).astype(jnp.int32), -1)
    grp = tile_cls // PAIRS_PER_GROUP
    pair = tile_cls % PAIRS_PER_GROUP
    pair_a = jnp.asarray([0, 0, 0, 1, 1, 2], jnp.int32)
    pair_b = jnp.asarray([1, 2, 3, 2, 3, 3], jnp.int32)
    eb = grp * EXPERTS_PER_GROUP + pair_b[pair]
    ea = jnp.where(used, grp * EXPERTS_PER_GROUP + pair_a[pair], eb)
    return dest, (ea, eb, grp, used.astype(jnp.int32))


@jax.jit
def kernel(x, c, rel_bias, norm1_g, norm2_g, w_ada, b_ada, ev_w_in, ev_w_out, ev_gmlp_ln_g, ev_gmlp_ln_b, ev_w_s, ev_b_s, od_w_in, od_w_out, od_w_dw, od_b_dw, od_conv_ln_g, od_conv_ln_b, moe_w_group, moe_b_group, moe_w_router, moe_b_router, moe_w_gate, moe_w_up, moe_w_down, final_norm_g):
    b, s, d = x.shape
    depth = w_ada.shape[0]
    mods = _ada(c, w_ada, b_ada).reshape(depth, b, 6, d)
    biases = [_branch_bias(rel_bias, dil) for _, dil in DILATED_BRANCHES]
    for layer in range(depth):
        j = layer // 2
        mod = mods[layer]
        if layer % 2 == 0:
            dils = [dil for _, dil in DILATED_BRANCHES]
            proj, *grouped = _in_proj(x, norm1_g[layer], mod, ev_w_in[j].astype(BF16),
                                      [dil for dil in dils if dil > 1])
            views = {1: (proj, proj.shape[-1] // WIDTH)}
            views.update({dil: (g, 3) for dil, g in zip([dil for dil in dils if dil > 1], grouped)})
            outs = [_dilated_branch(views[dil][0], bias, dil, views[dil][1])
                    for bias, dil in zip(biases, dils)]
            left = _combine_branches(outs, dils, s)
            right = _gmlp(proj, ev_gmlp_ln_g[j], ev_gmlp_ln_b[j], ev_w_s[j], ev_b_s[j])
            w_out = ev_w_out[j]
        else:
            proj = _in_proj(x, norm1_g[layer], mod, od_w_in[j].astype(BF16))
            left = _conv_module(proj, od_w_dw[j], od_b_dw[j], od_conv_ln_g[j], od_conv_ln_b[j])
            right = _stick_breaking(proj)
            w_out = od_w_out[j]
        r_hi, r_lo, r_b = _router_weights(moe_w_group[layer], moe_b_group[layer],
                                          moe_w_router[layer], moe_b_router[layer])
        x, h, cls, rank, counts = _out_proj(left, right, w_out.astype(BF16), x, mod,
                                            norm2_g[layer], r_hi, r_lo, r_b)
        dest, plan = _routing_plan(cls, rank, counts, b * s)
        hs = _dispatch(h, dest, (b * s // SLOT_TILE + N_CLASSES) * SLOT_TILE)
        ys = _moe(hs, plan, moe_w_gate[layer].astype(BF16), moe_w_up[layer].astype(BF16),
                  moe_w_down[layer].astype(BF16), r_hi, r_lo, r_b)
        x = _combine(ys, dest, x, mod, final_norm_g, layer == depth - 1)
    return x
```

```python
import functools
import math

import numpy as np
import jax
import jax.numpy as jnp
from jax import lax
from jax.experimental import pallas as pl
from jax.experimental.pallas import tpu as pltpu

F32 = jnp.float32
BF16 = jnp.bfloat16

HEAD_DIM = 64
HEADS = 8
WIDTH = HEADS * HEAD_DIM
PAIR = 2 * HEAD_DIM
BLK = 128
DILATED_BRANCHES = ((128, 1), (512, 4), (2048, 16))
N_BUCKETS = 32
MAX_DISTANCE = 2048
CONV_WIDTH = 31
CONV_HALO = 32
B_GROUPS = 4
N_GROUPS = 4
EXPERTS_PER_GROUP = 4
N_EXPERTS = N_GROUPS * EXPERTS_PER_GROUP
PAIRS_PER_GROUP = EXPERTS_PER_GROUP * (EXPERTS_PER_GROUP - 1) // 2
N_CLASSES = N_GROUPS * PAIRS_PER_GROUP
TOKEN_TILE = 512
SLOT_TILE = 512
EPS = 1e-6
NEG = -1e30
LOG2E = 1.4426950408889634
EXP_IS_ZERO_BELOW = -104.0
LANES = 128
SUBLANES = 8
VMEM_LIMIT = 48 * 1024 * 1024


def _params(sem):
    return pltpu.CompilerParams(dimension_semantics=sem, vmem_limit_bytes=VMEM_LIMIT)


def _bdot(a, b):
    return jnp.dot(a, b, preferred_element_type=F32)


def _bdot_nt(a, b):
    return lax.dot_general(a, b, (((1,), (1,)), ((), ())), preferred_element_type=F32)


def _split(a):
    hi = a.astype(BF16)
    lo = (a - hi.astype(F32)).astype(BF16)
    return hi, lo


def _dot3(a, w_hi, w_lo):
    a_hi, a_lo = _split(a)
    return _bdot(a_hi, w_hi) + (_bdot(a_hi, w_lo) + _bdot(a_lo, w_hi))


def _ada_kernel(c_ref, w_ref, b_ref, o_ref):
    c = c_ref[...]
    sc = c * jax.nn.sigmoid(c)
    w_hi, w_lo = _split(w_ref[...])
    o_ref[...] = _dot3(sc, w_hi, w_lo) + b_ref[...]


def _ada(c, w_ada, b_ada):
    depth, d, n = w_ada.shape
    b = c.shape[0]
    tn = 1024
    return pl.pallas_call(
        _ada_kernel,
        grid=(depth, n // tn),
        in_specs=[
            pl.BlockSpec((b, d), lambda l, j: (0, 0)),
            pl.BlockSpec((None, d, tn), lambda l, j: (l, 0, j)),
            pl.BlockSpec((None, 1, tn), lambda l, j: (l, 0, j)),
        ],
        out_specs=pl.BlockSpec((None, b, tn), lambda l, j: (l, 0, j)),
        out_shape=jax.ShapeDtypeStruct((depth, b, n), F32),
        compiler_params=_params(("arbitrary", "arbitrary")),
        name="ada_mod",
    )(c, w_ada, b_ada.reshape(depth, 1, n))


def _rms_mod(x, g, scale_row, shift_row):
    y = x * lax.rsqrt(jnp.mean(x * x, -1, keepdims=True) + EPS) * g
    return y * (1.0 + scale_row) + shift_row


def _in_kernel(x_ref, g_ref, mod_ref, w_ref, o_ref, *rest, dilations):
    m = mod_ref[...]
    h = _rms_mod(x_ref[...], g_ref[...], m[1:2], m[0:1]).astype(BF16)
    tm = x_ref.shape[0]
    n = w_ref.shape[1]
    per = WIDTH // LANES
    for j in range(n // WIDTH):
        cols = slice(j * WIDTH, (j + 1) * WIDTH)
        y = _bdot(h, w_ref[:, cols])
        o_ref[:, cols] = y.astype(o_ref.dtype)
        if dilations and j < 3:
            for c in range(per):
                rest[-1][j * per + c] = y[:, c * LANES:(c + 1) * LANES]
    for d, out in zip(dilations, rest):
        for r in range(d):
            for c in range(3 * per):
                lo = (r * 3 * per + c) * LANES
                out[:, lo:lo + LANES] = rest[-1][c, pl.ds(r, tm // d, stride=d), :].astype(out.dtype)


def _in_proj(x, g, mod, w_in, dilations=()):
    b, s, d = x.shape
    n = w_in.shape[1]
    tm = TOKEN_TILE
    qkv = 3 * WIDTH
    outs = pl.pallas_call(
        functools.partial(_in_kernel, dilations=tuple(dilations)),
        grid=(b, s // tm),
        in_specs=[
            pl.BlockSpec((None, tm, d), lambda i, t: (i, t, 0)),
            pl.BlockSpec((1, d), lambda i, t: (0, 0)),
            pl.BlockSpec((None, 6, d), lambda i, t: (i, 0, 0)),
            pl.BlockSpec((d, n), lambda i, t: (0, 0)),
        ],
        out_specs=[pl.BlockSpec((None, tm, n), lambda i, t: (i, t, 0))] + [
            pl.BlockSpec((None, tm // dil, dil * qkv), lambda i, t: (i, t, 0)) for dil in dilations],
        out_shape=[jax.ShapeDtypeStruct((b, s, n), BF16)] + [
            jax.ShapeDtypeStruct((b, s // dil, dil * qkv), BF16) for dil in dilations],
        scratch_shapes=[pltpu.VMEM((qkv // LANES, tm, LANES), F32)] if dilations else [],
        compiler_params=_params(("arbitrary", "arbitrary")),
        name="in_proj",
    )(x, g.reshape(1, d), mod, w_in)
    return outs if dilations else outs[0]


def _t5_bucket(dist):
    max_exact = N_BUCKETS // 2
    d = np.maximum(dist, 1).astype(np.float32)
    large = max_exact + (np.log(d / max_exact) / np.log(MAX_DISTANCE / max_exact)
                         * (N_BUCKETS - max_exact)).astype(np.int32)
    large = np.minimum(large, N_BUCKETS - 1)
    return np.where(dist < max_exact, dist, large).astype(np.int32)


def _branch_bias(rel_table, dilation):
    qi = np.arange(BLK)[:, None]
    kj = np.arange(2 * BLK)[None, :]
    bucket = _t5_bucket(np.clip(qi + BLK - kj, 0, None) * dilation)
    onehot = np.eye(N_BUCKETS, dtype=np.float32)[bucket]
    bias = jnp.einsum("qkb,bh->hqk", onehot, rel_table.astype(F32), precision=lax.Precision.HIGHEST)
    dist = qi + BLK - kj
    in_window = (dist >= 0) & (dist <= BLK)
    return jnp.where(in_window[None], bias, NEG)


def _dil_kernel(q_ref, kp_ref, kc_ref, vp_ref, vc_ref, bias_ref, o_ref, m_ref, l_ref,
                kbuf, vbuf, *, tq):
    n = pl.program_id(2)
    nsub = tq // BLK
    kbuf[0:BLK] = kp_ref[...]
    kbuf[BLK:] = kc_ref[...]
    vbuf[0:BLK] = vp_ref[...]
    vbuf[BLK:] = vc_ref[...]
    kj = lax.broadcasted_iota(jnp.int32, (BLK, 2 * BLK), 1)
    lane = lax.broadcasted_iota(jnp.int32, (BLK, PAIR), 1)
    scale = HEAD_DIM ** -0.5
    has_prev = kj >= jnp.where(n == 0, BLK, 0)
    for i in range(nsub):
        rows = slice(i * BLK, (i + 1) * BLK)
        krows = slice(i * BLK, (i + 2) * BLK)
        m_tile = jnp.zeros((BLK, LANES), F32)
        l_tile = jnp.ones((BLK, LANES), F32)
        for p in range(HEADS // 2):
            cols = slice(p * PAIR, (p + 1) * PAIR)
            q_pair = q_ref[rows, cols] * scale
            k_pair = kbuf[krows, cols]
            v_pair = vbuf[krows, cols]
            outs = []
            for hh in range(2):
                h = 2 * p + hh
                head_lanes = (lane >= HEAD_DIM) if hh else (lane < HEAD_DIM)
                qm = jnp.where(head_lanes, q_pair, jnp.zeros_like(q_pair))
                logit = _bdot_nt(qm, k_pair) + bias_ref[h]
                if i == 0:
                    logit = jnp.where(has_prev, logit, NEG)
                mx = jnp.max(logit, -1, keepdims=True)
                pr = jnp.exp(logit - mx)
                den = jnp.sum(pr, -1, keepdims=True)
                outs.append(_bdot(pr.astype(BF16), v_pair) / den)
                m_tile = jnp.where(lane == h, mx, m_tile)
                l_tile = jnp.where(lane == h, den, l_tile)
            o_ref[rows, cols] = jnp.where(lane < HEAD_DIM, outs[0], outs[1]).astype(o_ref.dtype)
        m_ref[rows, :] = m_tile
        l_ref[rows, :] = l_tile


def _dilated_branch(view, bias, dilation, ncol):
    b, sub, _ = view.shape
    tq = min(512, sub)
    nsub = tq // BLK
    cur = lambda c: pl.BlockSpec((None, tq, WIDTH), lambda i, r, t: (i, t, r * ncol + c))
    prev = lambda c: pl.BlockSpec(
        (None, BLK, WIDTH), lambda i, r, t: (i, jnp.maximum(t * nsub - 1, 0), r * ncol + c))
    return pl.pallas_call(
        functools.partial(_dil_kernel, tq=tq),
        grid=(b, dilation, sub // tq),
        in_specs=[cur(0), prev(1), cur(1), prev(2), cur(2),
                  pl.BlockSpec((HEADS, BLK, 2 * BLK), lambda i, r, t: (0, 0, 0))],
        out_specs=[
            pl.BlockSpec((None, tq, WIDTH), lambda i, r, t: (i, t, r)),
            pl.BlockSpec((None, tq, LANES), lambda i, r, t: (i, t, r)),
            pl.BlockSpec((None, tq, LANES), lambda i, r, t: (i, t, r)),
        ],
        out_shape=[
            jax.ShapeDtypeStruct((b, sub, dilation * WIDTH), BF16),
            jax.ShapeDtypeStruct((b, sub, dilation * LANES), F32),
            jax.ShapeDtypeStruct((b, sub, dilation * LANES), F32),
        ],
        scratch_shapes=[pltpu.VMEM((BLK + tq, WIDTH), BF16), pltpu.VMEM((BLK + tq, WIDTH), BF16)],
        compiler_params=_params(("arbitrary", "arbitrary", "arbitrary")),
        name=f"dilated_d{dilation}",
    )(view, view, view, view, view, bias)


def _comb_kernel(*refs, dilations):
    nb = len(dilations)
    o_refs, m_refs, l_refs = refs[:nb], refs[nb:2 * nb], refs[2 * nb:3 * nb]
    e_ref, out_ref, o_scr, s_scr = refs[3 * nb:]
    tm = out_ref.shape[0]
    per = WIDTH // LANES
    os, ms, ls = [], [], []
    for bi, d in enumerate(dilations):
        if d == 1:
            os.append([o_refs[bi][:, c * LANES:(c + 1) * LANES].astype(F32) for c in range(per)])
            ms.append(m_refs[bi][...])
            ls.append(l_refs[bi][...])
            continue
        rows = lambda r: pl.ds(r, tm // d, stride=d)
        for r in range(d):
            for c in range(per):
                lo = (r * per + c) * LANES
                o_scr[bi, c, rows(r), :] = o_refs[bi][:, lo:lo + LANES].astype(F32)
            s_scr[bi, 0, rows(r), :] = m_refs[bi][:, r * LANES:(r + 1) * LANES]
            s_scr[bi, 1, rows(r), :] = l_refs[bi][:, r * LANES:(r + 1) * LANES]
        os.append([o_scr[bi, c] for c in range(per)])
        ms.append(s_scr[bi, 0])
        ls.append(s_scr[bi, 1])
    m_all = functools.reduce(jnp.maximum, ms)
    es = [l * jnp.exp(m - m_all) for l, m in zip(ls, ms)]
    tot = functools.reduce(lambda x, y: x + y, es)
    acc = [None] * per
    for e, o in zip(es, os):
        w_hi, w_lo = _split(e / tot)
        w = _bdot(w_hi, e_ref[...]) + _bdot(w_lo, e_ref[...])
        for c in range(per):
            term = w[:, c * LANES:(c + 1) * LANES] * o[c]
            acc[c] = term if acc[c] is None else acc[c] + term
    for c in range(per):
        out_ref[:, c * LANES:(c + 1) * LANES] = acc[c].astype(out_ref.dtype)


def _combine_branches(outs, dilations, s):
    b = outs[0][0].shape[0]
    tm = TOKEN_TILE
    nb = len(dilations)
    expand = np.zeros((LANES, WIDTH), np.float32)
    for h in range(HEADS):
        expand[h, h * HEAD_DIM:(h + 1) * HEAD_DIM] = 1.0
    grouped = lambda w: [pl.BlockSpec((None, tm // d, d * w), lambda i, t: (i, t, 0)) for d in dilations]
    return pl.pallas_call(
        functools.partial(_comb_kernel, dilations=tuple(dilations)),
        grid=(b, s // tm),
        in_specs=grouped(WIDTH) + grouped(LANES) + grouped(LANES)
        + [pl.BlockSpec((LANES, WIDTH), lambda i, t: (0, 0))],
        out_specs=pl.BlockSpec((None, tm, WIDTH), lambda i, t: (i, t, 0)),
        out_shape=jax.ShapeDtypeStruct((b, s, WIDTH), BF16),
        scratch_shapes=[pltpu.VMEM((nb, WIDTH // LANES, tm, LANES), F32),
                        pltpu.VMEM((nb, 2, tm, LANES), F32)],
        compiler_params=_params(("arbitrary", "arbitrary")),
        name="dilated_combine",
    )(*[o for o, _, _ in outs], *[m for _, m, _ in outs], *[l for _, _, l in outs],
      jnp.asarray(expand, BF16))


def _gelu(x):
    return 0.5 * x * (1.0 + jnp.tanh(math.sqrt(2.0 / math.pi) * (x + 0.044715 * (x * x * x))))


def _gmlp_kernel(u_ref, v_ref, g_ref, b_ref, ws_ref, bs_ref, o_ref):
    tm = u_ref.shape[0]
    u = _gelu(u_ref[...].astype(F32))
    v = _gelu(v_ref[...].astype(F32))
    mu = jnp.mean(v, -1, keepdims=True)
    vc = v - mu
    var = jnp.mean(vc * vc, -1, keepdims=True)
    v = (vc * lax.rsqrt(var + EPS) * g_ref[...] + b_ref[...]).astype(BF16)
    r = lax.broadcasted_iota(jnp.int32, (BLK, BLK), 0)
    c = lax.broadcasted_iota(jnp.int32, (BLK, BLK), 1)
    for g in range(B_GROUPS):
        w = jnp.where(c <= r, ws_ref[g], 0.0).astype(BF16)
        cols = slice(g * BLK, (g + 1) * BLK)
        for ch in range(tm // BLK):
            rows = slice(ch * BLK, (ch + 1) * BLK)
            mixed = _bdot(w, v[rows, cols]) + bs_ref[g]
            o_ref[rows, cols] = (u[rows, cols] * mixed).astype(o_ref.dtype)


def _gmlp(proj, ln_g, ln_b, w_s, b_s):
    b, s, n = proj.shape
    tm = 512
    return pl.pallas_call(
        _gmlp_kernel,
        grid=(b, s // tm),
        in_specs=[
            pl.BlockSpec((None, tm, WIDTH), lambda i, t: (i, t, 3)),
            pl.BlockSpec((None, tm, WIDTH), lambda i, t: (i, t, 4)),
            pl.BlockSpec((1, WIDTH), lambda i, t: (0, 0)),
            pl.BlockSpec((1, WIDTH), lambda i, t: (0, 0)),
            pl.BlockSpec((B_GROUPS, BLK, BLK), lambda i, t: (0, 0, 0)),
            pl.BlockSpec((B_GROUPS, BLK, 1), lambda i, t: (0, 0, 0)),
        ],
        out_specs=pl.BlockSpec((None, tm, WIDTH), lambda i, t: (i, t, 0)),
        out_shape=jax.ShapeDtypeStruct((b, s, WIDTH), BF16),
        compiler_params=_params(("arbitrary", "arbitrary")),
        name="gmlp_gate",
    )(proj, proj, ln_g.reshape(1, WIDTH), ln_b.reshape(1, WIDTH), w_s,
      b_s.reshape(B_GROUPS, BLK, 1))


def _conv_kernel(a_ref, g_ref, ah_ref, gh_ref, w_ref, bdw_ref, lng_ref, lnb_ref, o_ref, hbuf, wbuf):
    tm = a_ref.shape[0]
    t = pl.program_id(1)
    halo = ah_ref[...].astype(F32) * jax.nn.sigmoid(gh_ref[...].astype(F32))
    hbuf[0, 0:CONV_HALO] = jnp.where(t == 0, 0.0, halo)
    hbuf[0, CONV_HALO:] = a_ref[...].astype(F32) * jax.nn.sigmoid(g_ref[...].astype(F32))
    n_shifted = tm + CONV_HALO - SUBLANES
    for s in range(1, SUBLANES):
        hbuf[s, 0:n_shifted] = hbuf[0, pl.ds(s, n_shifted), :]
    for k in range(CONV_WIDTH):
        wbuf[k] = jnp.broadcast_to(w_ref[k:k + 1, :], (SUBLANES, WIDTH))
    rc = 32
    off = CONV_HALO - (CONV_WIDTH - 1)

    def chunk(ci, carry):
        base = pl.multiple_of(ci * rc, rc)
        accs = [jnp.zeros((rc, WIDTH), F32) + bdw_ref[...], None]
        for k in range(CONV_WIDTH):
            s = (off + k) % SUBLANES
            win = hbuf[s, pl.ds(base + (off + k - s), rc), :].reshape(rc // SUBLANES, SUBLANES, WIDTH)
            term = (wbuf[k] * win).reshape(rc, WIDTH)
            accs[k % 2] = term if accs[k % 2] is None else accs[k % 2] + term
        acc = accs[0] + accs[1]
        mu = jnp.mean(acc, -1, keepdims=True)
        xc = acc - mu
        var = jnp.mean(xc * xc, -1, keepdims=True)
        y = xc * lax.rsqrt(var + EPS) * lng_ref[...] + lnb_ref[...]
        o_ref[pl.ds(base, rc), :] = (y * jax.nn.sigmoid(y)).astype(o_ref.dtype)
        return carry

    lax.fori_loop(0, tm // rc, chunk, 0)


def _conv_module(proj, w_dw, b_dw, ln_g, ln_b):
    b, s, n = proj.shape
    tm = 512
    hb = tm // CONV_HALO
    row = lambda v: v.reshape(1, WIDTH)
    halo = lambda c: pl.BlockSpec(
        (None, CONV_HALO, WIDTH), lambda i, t: (i, jnp.maximum(t * hb - 1, 0), c))
    return pl.pallas_call(
        _conv_kernel,
        grid=(b, s // tm),
        in_specs=[
            pl.BlockSpec((None, tm, WIDTH), lambda i, t: (i, t, 0)),
            pl.BlockSpec((None, tm, WIDTH), lambda i, t: (i, t, 1)),
            halo(0), halo(1),
            pl.BlockSpec((CONV_WIDTH, WIDTH), lambda i, t: (0, 0)),
            pl.BlockSpec((1, WIDTH), lambda i, t: (0, 0)),
            pl.BlockSpec((1, WIDTH), lambda i, t: (0, 0)),
            pl.BlockSpec((1, WIDTH), lambda i, t: (0, 0)),
        ],
        out_specs=pl.BlockSpec((None, tm, WIDTH), lambda i, t: (i, t, 0)),
        out_shape=jax.ShapeDtypeStruct((b, s, WIDTH), BF16),
        scratch_shapes=[pltpu.VMEM((SUBLANES, CONV_HALO + tm, WIDTH), F32),
                        pltpu.VMEM((CONV_WIDTH, SUBLANES, WIDTH), F32)],
        compiler_params=_params(("arbitrary", "arbitrary")),
        name="conv_module",
    )(proj, proj, proj, proj, w_dw, row(b_dw), row(ln_g), row(ln_b))


def _stick_kernel(q_ref, k_ref, v_ref, mm_ref, o_ref, q2_ref, carry_ref, acc_ref):
    i = pl.program_id(1)
    npair = WIDTH // PAIR
    lane = lax.broadcasted_iota(jnp.int32, (BLK, PAIR), 1)
    lo_lanes = lane < HEAD_DIM
    row2 = lax.broadcasted_iota(jnp.int32, (2 * BLK, BLK), 0)
    col2 = lax.broadcasted_iota(jnp.int32, (2 * BLK, BLK), 1)
    causal2 = col2 < (row2 & (BLK - 1))
    scale = HEAD_DIM ** -0.5
    for p in range(npair):
        qp = q_ref[:, p * PAIR:(p + 1) * PAIR] * scale
        zero = jnp.zeros_like(qp)
        q2_ref[p] = jnp.concatenate(
            [jnp.where(lo_lanes, qp, zero), jnp.where(lo_lanes, zero, qp)], axis=0)
    carry_ref[...] = jnp.zeros_like(carry_ref)
    acc_ref[...] = jnp.zeros_like(acc_ref)

    def blocks(js, diag):
        cols = lambda p: slice(p * PAIR, (p + 1) * PAIR)
        rows = lambda j: pl.ds(pl.multiple_of(j * BLK, BLK), BLK)

        def scores(j, p):
            return _bdot_nt(q2_ref[p], k_ref[rows(j), cols(p)])

        def keep_sums(z):
            soft = jnp.log(1.0 + jnp.exp2(jnp.abs(z) * -LOG2E))
            log_beta = jnp.minimum(z, 0.0) - soft
            log_keep = log_beta - z
            if diag:
                log_keep = jnp.where(causal2, log_keep, 0.0)
            return log_beta, _bdot(log_keep.astype(BF16), mm_ref[...])

        def weighted_values(j, p, log_beta, sums):
            carry = carry_ref[p]
            att = jnp.exp(log_beta + (sums[:, :BLK] + carry))
            if diag:
                att = jnp.where(causal2, att, 0.0)
            carry_ref[p] = carry + sums[:, BLK:]
            att = att.astype(BF16)
            vb = v_ref[rows(j), cols(p)]
            zero = jnp.zeros_like(vb)
            v2 = jnp.concatenate([jnp.where(lo_lanes, vb, zero), jnp.where(lo_lanes, zero, vb)], axis=0)
            return _bdot(jnp.concatenate([att[:BLK], att[BLK:]], axis=1), v2)

        chains = [(j, p) for j in js for p in range(npair)]
        z, ks, pv = {}, {}, {}
        for step in range(len(chains) + 3):
            if step < len(chains):
                z[step] = scores(*chains[step])
            if 0 <= step - 1 < len(chains):
                ks[step - 1] = keep_sums(z.pop(step - 1))
            if 0 <= step - 2 < len(chains):
                pv[step - 2] = weighted_values(*chains[step - 2], *ks.pop(step - 2))
            if 0 <= step - 3 < len(chains):
                acc_ref[chains[step - 3][1]] += pv.pop(step - 3)

    blocks([i], True)

    def live():
        c = carry_ref[...]
        worst = jnp.max(jnp.max(c, axis=0), axis=0, keepdims=True)
        return (worst[0, 0] >= EXP_IS_ZERO_BELOW).astype(jnp.int32)

    def two_blocks(state):
        t, _ = state
        j = i - 1 - 2 * t
        blocks([j, j - 1], False)
        return t + 1, live()

    _, alive = lax.while_loop(lambda st: (st[0] < i // 2) & (st[1] == 1), two_blocks,
                              (jnp.int32(0), jnp.int32(1)))

    @pl.when((i % 2 == 1) & (alive == 1))
    def _():
        blocks([0], False)

    for p in range(npair):
        o_ref[:, p * PAIR:(p + 1) * PAIR] = acc_ref[p].astype(o_ref.dtype)


def _suffix_sum_matrix():
    sp = np.arange(BLK)[:, None]
    sc = np.arange(BLK)[None, :]
    return jnp.asarray(
        np.concatenate([(sp > sc).astype(np.float32), np.ones((BLK, BLK), np.float32)], axis=1), BF16)


def _stick_breaking(proj):
    b, s, n = proj.shape
    npair = WIDTH // PAIR
    return pl.pallas_call(
        _stick_kernel,
        grid=(b, s // BLK),
        in_specs=[
            pl.BlockSpec((None, BLK, WIDTH), lambda i, t: (i, t, 2)),
            pl.BlockSpec((None, s, WIDTH), lambda i, t: (i, 0, 3)),
            pl.BlockSpec((None, s, WIDTH), lambda i, t: (i, 0, 4)),
            pl.BlockSpec((BLK, 2 * BLK), lambda i, t: (0, 0)),
        ],
        out_specs=pl.BlockSpec((None, BLK, WIDTH), lambda i, t: (i, t, 0)),
        out_shape=jax.ShapeDtypeStruct((b, s, WIDTH), BF16),
        scratch_shapes=[pltpu.VMEM((npair, 2 * BLK, PAIR), BF16),
                        pltpu.VMEM((npair, 2 * BLK, BLK), F32),
                        pltpu.VMEM((npair, BLK, PAIR), F32)],
        compiler_params=_params(("arbitrary", "arbitrary")),
        name="stick_breaking",
    )(proj, proj, proj, _suffix_sum_matrix())


def _lane_min_index(mask, lane):
    return jnp.min(jnp.where(mask, lane, float(LANES)), -1, keepdims=True)


def _out_kernel(l_ref, r_ref, w_ref, x_ref, mod_ref, g_ref, rh_ref, rl_ref, rb_ref, tri_ref,
                xo_ref, h_ref, cls_ref, rank_ref, cnt_ref, seen):
    tm = x_ref.shape[0]

    @pl.when((pl.program_id(0) == 0) & (pl.program_id(1) == 0))
    def _():
        seen[...] = jnp.zeros_like(seen)

    m = mod_ref[...]
    mix = _bdot(l_ref[...], w_ref[0:WIDTH, :]) + _bdot(r_ref[...], w_ref[WIDTH:, :])
    x = x_ref[...] + m[2:3] * mix
    xo_ref[...] = x
    h = _rms_mod(x, g_ref[...], m[4:5], m[3:4])
    h_ref[...] = h
    logits = _dot3(h, rh_ref[...], rl_ref[...]) + rb_ref[...]
    lane = lax.broadcasted_iota(jnp.int32, logits.shape, 1).astype(F32)
    is_group = (lane >= N_EXPERTS) & (lane < N_EXPERTS + N_GROUPS)
    glog = jnp.where(is_group, logits, -jnp.inf)
    gmax = jnp.max(glog, -1, keepdims=True)
    g_idx = _lane_min_index(glog == gmax, lane) - N_EXPERTS
    in_group = (lane >= g_idx * EXPERTS_PER_GROUP) & (lane < (g_idx + 1) * EXPERTS_PER_GROUP)
    ev = jnp.where(in_group, logits, -jnp.inf)
    v1 = jnp.max(ev, -1, keepdims=True)
    i1 = _lane_min_index(ev == v1, lane)
    ev2 = jnp.where(lane == i1, -jnp.inf, ev)
    v2 = jnp.max(ev2, -1, keepdims=True)
    i2 = _lane_min_index(ev2 == v2, lane)
    lo = jnp.minimum(i1, i2) - g_idx * EXPERTS_PER_GROUP
    hi = jnp.maximum(i1, i2) - g_idx * EXPERTS_PER_GROUP
    cls = g_idx * PAIRS_PER_GROUP + (lo * (7.0 - lo) * 0.5 + (hi - lo - 1.0))
    cls_t = jnp.transpose(jnp.broadcast_to(cls, (tm, LANES)))
    onehot_t = lax.broadcasted_iota(jnp.int32, (LANES, tm), 0).astype(F32) == cls_t
    ones_t = jnp.where(onehot_t, 1.0, 0.0)
    earlier = _bdot(ones_t.astype(BF16), tri_ref[...]) + seen[...]
    cls_ref[...] = cls_t[0:1, :].astype(jnp.int32)
    rank_ref[...] = jnp.sum(jnp.where(onehot_t, earlier, 0.0), 0, keepdims=True).astype(jnp.int32)
    seen[...] += jnp.sum(ones_t, 1, keepdims=True)
    cnt_ref[...] = seen[...]


def _out_proj(left, right, w_out, x, mod, g, r_hi, r_lo, r_b):
    b, s, d = x.shape
    tm = TOKEN_TILE
    nt = s // tm
    tile = lambda w: pl.BlockSpec((None, tm, w), lambda i, t: (i, t, 0))
    const = lambda shp: pl.BlockSpec(shp, lambda i, t: tuple(0 for _ in shp))
    row = pl.BlockSpec((None, 1, tm), lambda i, t: (i * nt + t, 0, 0))
    tri = np.arange(tm)[:, None] < np.arange(tm)[None, :]
    return pl.pallas_call(
        _out_kernel,
        grid=(b, nt),
        in_specs=[tile(WIDTH), tile(WIDTH), const((2 * WIDTH, d)), tile(d),
                  pl.BlockSpec((None, 6, d), lambda i, t: (i, 0, 0)),
                  const((1, d)), const((d, LANES)), const((d, LANES)), const((1, LANES)),
                  const((tm, tm))],
        out_specs=[tile(d), tile(d), row, row, const((LANES, 1))],
        out_shape=[jax.ShapeDtypeStruct((b, s, d), F32),
                   jax.ShapeDtypeStruct((b, s, d), F32),
                   jax.ShapeDtypeStruct((b * nt, 1, tm), jnp.int32),
                   jax.ShapeDtypeStruct((b * nt, 1, tm), jnp.int32),
                   jax.ShapeDtypeStruct((LANES, 1), F32)],
        scratch_shapes=[pltpu.VMEM((LANES, 1), F32)],
        compiler_params=_params(("arbitrary", "arbitrary")),
        name="out_proj_router",
    )(left, right, w_out, x, mod, g.reshape(1, d), r_hi, r_lo, r_b, jnp.asarray(tri, BF16))


def _router_weights(w_group, b_group, w_router, b_router):
    d = w_group.shape[0]
    w = jnp.concatenate([jnp.transpose(w_router, (1, 0, 2)).reshape(d, N_EXPERTS), w_group], axis=1)
    w = jnp.pad(w.astype(F32), ((0, 0), (0, LANES - w.shape[1])))
    bias = jnp.concatenate([b_router.reshape(N_EXPERTS), b_group]).astype(F32)
    bias = jnp.pad(bias, (0, LANES - bias.shape[0])).reshape(1, LANES)
    hi, lo = _split(w)
    return hi, lo, bias


def _to_token_tiles(x):
    n = x.shape[0]
    parts = [x[:, g * LANES:(g + 1) * LANES].reshape(n // 8, 8, LANES) for g in range(8)]
    return jnp.swapaxes(jnp.stack(parts, axis=1), 1, 2).reshape(n, 8, LANES)


def _from_token_tiles(v):
    n = v.shape[0]
    w = jnp.swapaxes(v.reshape(n // 8, 8, 8, LANES), 1, 2)
    return jnp.concatenate([w[:, g].reshape(n, LANES) for g in range(8)], axis=-1)


def _row_dma_loop(n, start_row):
    def body(r, carry):
        start_row(r)
        return carry
    lax.fori_loop(0, n, body, 0, unroll=8)


def _dispatch_kernel(dest_ref, h_ref, zeros_ref, hs_ref, buf, sem):
    del zeros_ref
    tm = h_ref.shape[0]
    buf[...] = _to_token_tiles(h_ref[...])
    _row_dma_loop(tm, lambda r: pltpu.make_async_copy(
        buf.at[r], hs_ref.at[dest_ref[0, r]], sem).start())
    pltpu.make_async_copy(buf, hs_ref.at[pl.ds(0, tm)], sem).wait()


def _dispatch(h, dest, n_slots):
    b, s, d = h.shape
    tm = TOKEN_TILE
    nt = s // tm
    return pl.pallas_call(
        _dispatch_kernel,
        grid=(b * nt,),
        in_specs=[pl.BlockSpec((None, 1, tm), lambda i: (i, 0, 0), memory_space=pltpu.SMEM),
                  pl.BlockSpec((None, tm, d), lambda i: (i // nt, i % nt, 0)),
                  pl.BlockSpec(memory_space=pl.ANY)],
        out_specs=pl.BlockSpec(memory_space=pl.ANY),
        out_shape=jax.ShapeDtypeStruct((n_slots, 8, LANES), F32),
        scratch_shapes=[pltpu.VMEM((tm, 8, LANES), F32), pltpu.SemaphoreType.DMA(())],
        input_output_aliases={2: 0},
        compiler_params=_params(("arbitrary",)),
        name="moe_dispatch",
    )(dest, h, jnp.zeros((n_slots, 8, LANES), F32))


def _moe_kernel(ea_ref, eb_ref, grp_ref, used_ref, hs_ref, wga_ref, wua_ref, wda_ref,
                wgb_ref, wub_ref, wdb_ref, rh_ref, rb_ref, o_ref):
    n = pl.program_id(0)

    @pl.when(used_ref[n] == 1)
    def _():
        h = _from_token_tiles(hs_ref[...]).astype(BF16)
        logits = _bdot(h, rh_ref[...]) + rb_ref[...]
        lane = lax.broadcasted_iota(jnp.int32, logits.shape, 1)
        pick = lambda idx: jnp.sum(jnp.where(lane == idx, logits, 0.0), -1, keepdims=True)
        is_group = (lane >= N_EXPERTS) & (lane < N_EXPERTS + N_GROUPS)
        g_w = 1.0 / jnp.sum(
            jnp.where(is_group, jnp.exp(logits - pick(N_EXPERTS + grp_ref[n])), 0.0),
            -1, keepdims=True)
        la = pick(ea_ref[n])
        lb = pick(eb_ref[n])

        def expert(wg_ref, wu_ref, wd_ref):
            a = _bdot(h, wg_ref[...])
            hid = (a * jax.nn.sigmoid(a)) * _bdot(h, wu_ref[...])
            return _bdot(hid.astype(BF16), wd_ref[...])

        y = (g_w * (1.0 / (1.0 + jnp.exp(lb - la)))) * expert(wga_ref, wua_ref, wda_ref)
        y = y + (g_w * (1.0 / (1.0 + jnp.exp(la - lb)))) * expert(wgb_ref, wub_ref, wdb_ref)
        o_ref[...] = _to_token_tiles(y)

    @pl.when(used_ref[n] == 0)
    def _():
        o_ref[...] = jnp.zeros_like(o_ref)


def _moe(hs, plan, w_gate, w_up, w_down, r_hi, r_b):
    n_slots = hs.shape[0]
    ne, d, de = w_gate.shape
    tm = SLOT_TILE
    slot = pl.BlockSpec((tm, 8, LANES), lambda n, *_: (n, 0, 0))
    first = lambda shp: pl.BlockSpec((None,) + shp, lambda n, ea, eb, grp, used: (ea[n], 0, 0))
    second = lambda shp: pl.BlockSpec((None,) + shp, lambda n, ea, eb, grp, used: (eb[n], 0, 0))
    const = lambda shp: pl.BlockSpec(shp, lambda n, *_: tuple(0 for _ in shp))
    return pl.pallas_call(
        _moe_kernel,
        grid_spec=pltpu.PrefetchScalarGridSpec(
            num_scalar_prefetch=4,
            grid=(n_slots // tm,),
            in_specs=[slot, first((d, de)), first((d, de)), first((de, d)),
                      second((d, de)), second((d, de)), second((de, d)),
                      const((d, LANES)), const((1, LANES))],
            out_specs=slot),
        out_shape=jax.ShapeDtypeStruct((n_slots, 8, LANES), F32),
        compiler_params=_params(("arbitrary",)),
        name="moe_experts",
    )(*plan, hs, w_gate, w_up, w_down, w_gate, w_up, w_down, r_hi, r_b)


def _combine_kernel(dest_ref, ys_ref, x_ref, mod_ref, fg_ref, o_ref, buf, sem, *, final):
    tm = x_ref.shape[0]
    _row_dma_loop(tm, lambda r: pltpu.make_async_copy(
        ys_ref.at[dest_ref[0, r]], buf.at[r], sem).start())
    pltpu.make_async_copy(ys_ref.at[pl.ds(0, tm)], buf, sem).wait()
    x = x_ref[...] + mod_ref[5:6, :] * _from_token_tiles(buf[...])
    if final:
        x = x * lax.rsqrt(jnp.mean(x * x, -1, keepdims=True) + EPS) * fg_ref[...]
    o_ref[...] = x


def _combine(ys, dest, x, mod, final_g, final):
    b, s, d = x.shape
    tm = TOKEN_TILE
    nt = s // tm
    return pl.pallas_call(
        functools.partial(_combine_kernel, final=final),
        grid=(b * nt,),
        in_specs=[pl.BlockSpec((None, 1, tm), lambda i: (i, 0, 0), memory_space=pltpu.SMEM),
                  pl.BlockSpec(memory_space=pl.ANY),
                  pl.BlockSpec((None, tm, d), lambda i: (i // nt, i % nt, 0)),
                  pl.BlockSpec((None, 6, d), lambda i: (i // nt, 0, 0)),
                  pl.BlockSpec((1, d), lambda i: (0, 0))],
        out_specs=pl.BlockSpec((None, tm, d), lambda i: (i // nt, i % nt, 0)),
        out_shape=jax.ShapeDtypeStruct((b, s, d), F32),
        scratch_shapes=[pltpu.VMEM((tm, 8, LANES), F32), pltpu.SemaphoreType.DMA(())],
        compiler_params=_params(("arbitrary",)),
        name="moe_combine",
    )(dest, ys, x, mod, final_g.reshape(1, d))


def _routing_plan(cls, rank, counts, n_tokens):
    tm = SLOT_TILE
    n_tiles = n_tokens // tm + N_CLASSES
    counts = counts[:N_CLASSES, 0].astype(jnp.int32)
    tiles = (counts + tm - 1) // tm
    tile_end = jnp.cumsum(tiles)
    classes = jnp.arange(N_CLASSES, dtype=jnp.int32)
    first_slot = (tile_end - tiles) * tm
    dest = jnp.sum(jnp.where(cls[..., None] == classes, first_slot, 0), -1) + rank
    ids = jnp.arange(n_tiles, dtype=jnp.int32)
    used = ids < tile_end[-1]
    tile_cls = jnp.sum(
        (jnp.minimum(ids, tile_end[-1] - 1)[:, None] >= tile_end).astype(jnp.int32), -1)
    grp = tile_cls // PAIRS_PER_GROUP
    pair = tile_cls % PAIRS_PER_GROUP
    pair_a = jnp.asarray([0, 0, 0, 1, 1, 2], jnp.int32)
    pair_b = jnp.asarray([1, 2, 3, 2, 3, 3], jnp.int32)
    eb = grp * EXPERTS_PER_GROUP + pair_b[pair]
    ea = jnp.where(used, grp * EXPERTS_PER_GROUP + pair_a[pair], eb)
    return dest, (ea, eb, grp, used.astype(jnp.int32))


@jax.jit
def kernel(x, c, rel_bias, norm1_g, norm2_g, w_ada, b_ada, ev_w_in, ev_w_out, ev_gmlp_ln_g, ev_gmlp_ln_b, ev_w_s, ev_b_s, od_w_in, od_w_out, od_w_dw, od_b_dw, od_conv_ln_g, od_conv_ln_b, moe_w_group, moe_b_group, moe_w_router, moe_b_router, moe_w_gate, moe_w_up, moe_w_down, final_norm_g):
    b, s, d = x.shape
    depth = w_ada.shape[0]
    mods = _ada(c, w_ada, b_ada).reshape(depth, b, 6, d)
    biases = [_branch_bias(rel_bias, dil) for _, dil in DILATED_BRANCHES]
    for layer in range(depth):
        j = layer // 2
        mod = mods[layer]
        if layer % 2 == 0:
            dils = [dil for _, dil in DILATED_BRANCHES]
            proj, *grouped = _in_proj(x, norm1_g[layer], mod, ev_w_in[j].astype(BF16),
                                      [dil for dil in dils if dil > 1])
            views = {1: (proj, proj.shape[-1] // WIDTH)}
            views.update({dil: (g, 3) for dil, g in zip([dil for dil in dils if dil > 1], grouped)})
            outs = [_dilated_branch(views[dil][0], bias, dil, views[dil][1])
                    for bias, dil in zip(biases, dils)]
            left = _combine_branches(outs, dils, s)
            right = _gmlp(proj, ev_gmlp_ln_g[j], ev_gmlp_ln_b[j], ev_w_s[j], ev_b_s[j])
            w_out = ev_w_out[j]
        else:
            proj = _in_proj(x, norm1_g[layer], mod, od_w_in[j].astype(BF16))
            left = _conv_module(proj, od_w_dw[j], od_b_dw[j], od_conv_ln_g[j], od_conv_ln_b[j])
            right = _stick_breaking(proj)
            w_out = od_w_out[j]
        r_hi, r_lo, r_b = _router_weights(moe_w_group[layer], moe_b_group[layer],
                                          moe_w_router[layer], moe_b_router[layer])
        x, h, cls, rank, counts = _out_proj(left, right, w_out.astype(BF16), x, mod,
                                            norm2_g[layer], r_hi, r_lo, r_b)
        dest, plan = _routing_plan(cls, rank, counts, b * s)
        hs = _dispatch(h, dest, (b * s // SLOT_TILE + N_CLASSES) * SLOT_TILE)
        ys = _moe(hs, plan, moe_w_gate[layer].astype(BF16), moe_w_up[layer].astype(BF16),
                  moe_w_down[layer].astype(BF16), r_hi, r_b)
        x = _combine(ys, dest, x, mod, final_norm_g, layer == depth - 1)
    return x
```

```python
import functools
import math

import numpy as np
import jax
import jax.numpy as jnp
from jax import lax
from jax.experimental import pallas as pl
from jax.experimental.pallas import tpu as pltpu

F32 = jnp.float32
BF16 = jnp.bfloat16

HEAD_DIM = 64
HEADS = 8
WIDTH = HEADS * HEAD_DIM
PAIR = 2 * HEAD_DIM
BLK = 128
DILATED_BRANCHES = ((128, 1), (512, 4), (2048, 16))
N_BUCKETS = 32
MAX_DISTANCE = 2048
CONV_WIDTH = 31
CONV_HALO = 32
CONV_TAP_UNROLL = 4
B_GROUPS = 4
N_GROUPS = 4
EXPERTS_PER_GROUP = 4
N_EXPERTS = N_GROUPS * EXPERTS_PER_GROUP
PAIRS_PER_GROUP = EXPERTS_PER_GROUP * (EXPERTS_PER_GROUP - 1) // 2
N_CLASSES = N_GROUPS * PAIRS_PER_GROUP
TOKEN_TILE = 512
SLOT_TILE = 512
EPS = 1e-6
NEG = -1e30
LOG2E = 1.4426950408889634
EXP_IS_ZERO_BELOW = -104.0
LANES = 128
SUBLANES = 8
VMEM_LIMIT = 48 * 1024 * 1024


def _params(sem):
    return pltpu.CompilerParams(dimension_semantics=sem, vmem_limit_bytes=VMEM_LIMIT)


def _bdot(a, b):
    return jnp.dot(a, b, preferred_element_type=F32)


def _bdot_nt(a, b):
    return lax.dot_general(a, b, (((1,), (1,)), ((), ())), preferred_element_type=F32)


def _split(a):
    hi = a.astype(BF16)
    lo = (a - hi.astype(F32)).astype(BF16)
    return hi, lo


def _dot3(a, w_hi, w_lo):
    a_hi, a_lo = _split(a)
    return _bdot(a_hi, w_hi) + (_bdot(a_hi, w_lo) + _bdot(a_lo, w_hi))


def _ada_kernel(c_ref, w_ref, b_ref, o_ref):
    c = c_ref[...]
    sc = c * jax.nn.sigmoid(c)
    w_hi, w_lo = _split(w_ref[...])
    o_ref[...] = _dot3(sc, w_hi, w_lo) + b_ref[...]


def _ada(c, w_ada, b_ada):
    depth, d, n = w_ada.shape
    b = c.shape[0]
    tn = 1024
    return pl.pallas_call(
        _ada_kernel,
        grid=(depth, n // tn),
        in_specs=[
            pl.BlockSpec((b, d), lambda l, j: (0, 0)),
            pl.BlockSpec((None, d, tn), lambda l, j: (l, 0, j)),
            pl.BlockSpec((None, 1, tn), lambda l, j: (l, 0, j)),
        ],
        out_specs=pl.BlockSpec((None, b, tn), lambda l, j: (l, 0, j)),
        out_shape=jax.ShapeDtypeStruct((depth, b, n), F32),
        compiler_params=_params(("arbitrary", "arbitrary")),
        name="ada_mod",
    )(c, w_ada, b_ada.reshape(depth, 1, n))


def _rms_mod(x, g, scale_row, shift_row):
    y = x * lax.rsqrt(jnp.mean(x * x, -1, keepdims=True) + EPS) * g
    return y * (1.0 + scale_row) + shift_row


def _in_body(x, g_ref, mod_ref, w_ref, o_ref, rest, dilations):
    m = mod_ref[...]
    h = _rms_mod(x, g_ref[...], m[1:2], m[0:1]).astype(BF16)
    tm = x.shape[0]
    n = w_ref.shape[1]
    per = WIDTH // LANES
    for j in range(n // WIDTH):
        cols = slice(j * WIDTH, (j + 1) * WIDTH)
        y = _bdot(h, w_ref[:, cols])
        o_ref[:, cols] = y.astype(o_ref.dtype)
        if dilations and j < 3:
            for c in range(per):
                rest[-1][j * per + c] = y[:, c * LANES:(c + 1) * LANES]
    for d, out in zip(dilations, rest):
        for r in range(d):
            for c in range(3 * per):
                lo = (r * 3 * per + c) * LANES
                out[:, lo:lo + LANES] = rest[-1][c, pl.ds(r, tm // d, stride=d), :].astype(out.dtype)


def _in_kernel(x_ref, g_ref, mod_ref, w_ref, o_ref, *rest, dilations):
    _in_body(x_ref[...], g_ref, mod_ref, w_ref, o_ref, rest, dilations)


def _in_proj_outputs(b, s, n, tm, dilations, index):
    qkv = 3 * WIDTH
    specs = [pl.BlockSpec((None, tm, n), index)] + [
        pl.BlockSpec((None, tm // dil, dil * qkv), index) for dil in dilations]
    shapes = [jax.ShapeDtypeStruct((b, s, n), BF16)] + [
        jax.ShapeDtypeStruct((b, s // dil, dil * qkv), BF16) for dil in dilations]
    scratch = [pltpu.VMEM((qkv // LANES, tm, LANES), F32)] if dilations else []
    return specs, shapes, scratch


def _in_proj(x, g, mod, w_in, dilations=()):
    b, s, d = x.shape
    n = w_in.shape[1]
    tm = TOKEN_TILE
    specs, shapes, scratch = _in_proj_outputs(b, s, n, tm, dilations, lambda i, t: (i, t, 0))
    outs = pl.pallas_call(
        functools.partial(_in_kernel, dilations=tuple(dilations)),
        grid=(b, s // tm),
        in_specs=[
            pl.BlockSpec((None, tm, d), lambda i, t: (i, t, 0)),
            pl.BlockSpec((1, d), lambda i, t: (0, 0)),
            pl.BlockSpec((None, 6, d), lambda i, t: (i, 0, 0)),
            pl.BlockSpec((d, n), lambda i, t: (0, 0)),
        ],
        out_specs=specs,
        out_shape=shapes,
        scratch_shapes=scratch,
        compiler_params=_params(("arbitrary", "arbitrary")),
        name="in_proj",
    )(x, g.reshape(1, d), mod, w_in)
    return outs if dilations else outs[0]


def _t5_bucket(dist):
    max_exact = N_BUCKETS // 2
    d = np.maximum(dist, 1).astype(np.float32)
    large = max_exact + (np.log(d / max_exact) / np.log(MAX_DISTANCE / max_exact)
                         * (N_BUCKETS - max_exact)).astype(np.int32)
    large = np.minimum(large, N_BUCKETS - 1)
    return np.where(dist < max_exact, dist, large).astype(np.int32)


def _branch_bias(rel_table, dilation):
    qi = np.arange(BLK)[:, None]
    kj = np.arange(2 * BLK)[None, :]
    bucket = _t5_bucket(np.clip(qi + BLK - kj, 0, None) * dilation)
    onehot = np.eye(N_BUCKETS, dtype=np.float32)[bucket]
    bias = jnp.einsum("qkb,bh->hqk", onehot, rel_table.astype(F32), precision=lax.Precision.HIGHEST)
    dist = qi + BLK - kj
    in_window = (dist >= 0) & (dist <= BLK)
    return jnp.where(in_window[None], bias, NEG)


def _dil_kernel(q_ref, kp_ref, kc_ref, vp_ref, vc_ref, bias_ref, o_ref, m_ref, l_ref,
                kbuf, vbuf, *, tq):
    n = pl.program_id(2)
    nsub = tq // BLK
    kbuf[0:BLK] = kp_ref[...]
    kbuf[BLK:] = kc_ref[...]
    vbuf[0:BLK] = vp_ref[...]
    vbuf[BLK:] = vc_ref[...]
    kj = lax.broadcasted_iota(jnp.int32, (BLK, 2 * BLK), 1)
    lane = lax.broadcasted_iota(jnp.int32, (BLK, PAIR), 1)
    scale = HEAD_DIM ** -0.5
    has_prev = kj >= jnp.where(n == 0, BLK, 0)
    for i in range(nsub):
        rows = slice(i * BLK, (i + 1) * BLK)
        krows = slice(i * BLK, (i + 2) * BLK)
        m_tile = jnp.zeros((BLK, LANES), F32)
        l_tile = jnp.ones((BLK, LANES), F32)
        for p in range(HEADS // 2):
            cols = slice(p * PAIR, (p + 1) * PAIR)
            q_pair = q_ref[rows, cols] * scale
            k_pair = kbuf[krows, cols]
            v_pair = vbuf[krows, cols]
            outs = []
            for hh in range(2):
                h = 2 * p + hh
                head_lanes = (lane >= HEAD_DIM) if hh else (lane < HEAD_DIM)
                qm = jnp.where(head_lanes, q_pair, jnp.zeros_like(q_pair))
                logit = _bdot_nt(qm, k_pair) + bias_ref[h]
                if i == 0:
                    logit = jnp.where(has_prev, logit, NEG)
                mx = jnp.max(logit, -1, keepdims=True)
                pr = jnp.exp(logit - mx)
                den = jnp.sum(pr, -1, keepdims=True)
                outs.append(_bdot(pr.astype(BF16), v_pair) / den)
                m_tile = jnp.where(lane == h, mx, m_tile)
                l_tile = jnp.where(lane == h, den, l_tile)
            o_ref[rows, cols] = jnp.where(lane < HEAD_DIM, outs[0], outs[1]).astype(o_ref.dtype)
        m_ref[rows, :] = m_tile
        l_ref[rows, :] = l_tile


def _dilated_branch(view, bias, dilation, ncol):
    b, sub, _ = view.shape
    tq = min(512, sub)
    nsub = tq // BLK
    cur = lambda c: pl.BlockSpec((None, tq, WIDTH), lambda i, r, t: (i, t, r * ncol + c))
    prev = lambda c: pl.BlockSpec(
        (None, BLK, WIDTH), lambda i, r, t: (i, jnp.maximum(t * nsub - 1, 0), r * ncol + c))
    return pl.pallas_call(
        functools.partial(_dil_kernel, tq=tq),
        grid=(b, dilation, sub // tq),
        in_specs=[cur(0), prev(1), cur(1), prev(2), cur(2),
                  pl.BlockSpec((HEADS, BLK, 2 * BLK), lambda i, r, t: (0, 0, 0))],
        out_specs=[
            pl.BlockSpec((None, tq, WIDTH), lambda i, r, t: (i, t, r)),
            pl.BlockSpec((None, tq, LANES), lambda i, r, t: (i, t, r)),
            pl.BlockSpec((None, tq, LANES), lambda i, r, t: (i, t, r)),
        ],
        out_shape=[
            jax.ShapeDtypeStruct((b, sub, dilation * WIDTH), BF16),
            jax.ShapeDtypeStruct((b, sub, dilation * LANES), F32),
            jax.ShapeDtypeStruct((b, sub, dilation * LANES), F32),
        ],
        scratch_shapes=[pltpu.VMEM((BLK + tq, WIDTH), BF16), pltpu.VMEM((BLK + tq, WIDTH), BF16)],
        compiler_params=_params(("arbitrary", "arbitrary", "arbitrary")),
        name=f"dilated_d{dilation}",
    )(view, view, view, view, view, bias)


def _comb_kernel(*refs, dilations):
    nb = len(dilations)
    o_refs, m_refs, l_refs = refs[:nb], refs[nb:2 * nb], refs[2 * nb:3 * nb]
    e_ref, out_ref, o_scr, s_scr = refs[3 * nb:]
    tm = out_ref.shape[0]
    per = WIDTH // LANES
    os, ms, ls = [], [], []
    for bi, d in enumerate(dilations):
        if d == 1:
            os.append([o_refs[bi][:, c * LANES:(c + 1) * LANES].astype(F32) for c in range(per)])
            ms.append(m_refs[bi][...])
            ls.append(l_refs[bi][...])
            continue
        rows = lambda r: pl.ds(r, tm // d, stride=d)
        for r in range(d):
            for c in range(per):
                lo = (r * per + c) * LANES
                o_scr[bi, c, rows(r), :] = o_refs[bi][:, lo:lo + LANES].astype(F32)
            s_scr[bi, 0, rows(r), :] = m_refs[bi][:, r * LANES:(r + 1) * LANES]
            s_scr[bi, 1, rows(r), :] = l_refs[bi][:, r * LANES:(r + 1) * LANES]
        os.append([o_scr[bi, c] for c in range(per)])
        ms.append(s_scr[bi, 0])
        ls.append(s_scr[bi, 1])
    m_all = functools.reduce(jnp.maximum, ms)
    es = [l * jnp.exp(m - m_all) for l, m in zip(ls, ms)]
    tot = functools.reduce(lambda x, y: x + y, es)
    acc = [None] * per
    for e, o in zip(es, os):
        w_hi, w_lo = _split(e / tot)
        w = _bdot(w_hi, e_ref[...]) + _bdot(w_lo, e_ref[...])
        for c in range(per):
            term = w[:, c * LANES:(c + 1) * LANES] * o[c]
            acc[c] = term if acc[c] is None else acc[c] + term
    for c in range(per):
        out_ref[:, c * LANES:(c + 1) * LANES] = acc[c].astype(out_ref.dtype)


def _combine_branches(outs, dilations, s):
    b = outs[0][0].shape[0]
    tm = TOKEN_TILE
    nb = len(dilations)
    expand = np.zeros((LANES, WIDTH), np.float32)
    for h in range(HEADS):
        expand[h, h * HEAD_DIM:(h + 1) * HEAD_DIM] = 1.0
    grouped = lambda w: [pl.BlockSpec((None, tm // d, d * w), lambda i, t: (i, t, 0)) for d in dilations]
    return pl.pallas_call(
        functools.partial(_comb_kernel, dilations=tuple(dilations)),
        grid=(b, s // tm),
        in_specs=grouped(WIDTH) + grouped(LANES) + grouped(LANES)
        + [pl.BlockSpec((LANES, WIDTH), lambda i, t: (0, 0))],
        out_specs=pl.BlockSpec((None, tm, WIDTH), lambda i, t: (i, t, 0)),
        out_shape=jax.ShapeDtypeStruct((b, s, WIDTH), BF16),
        scratch_shapes=[pltpu.VMEM((nb, WIDTH // LANES, tm, LANES), F32),
                        pltpu.VMEM((nb, 2, tm, LANES), F32)],
        compiler_params=_params(("arbitrary", "arbitrary")),
        name="dilated_combine",
    )(*[o for o, _, _ in outs], *[m for _, m, _ in outs], *[l for _, _, l in outs],
      jnp.asarray(expand, BF16))


def _gelu(x):
    return 0.5 * x * (1.0 + jnp.tanh(math.sqrt(2.0 / math.pi) * (x + 0.044715 * (x * x * x))))


def _gmlp_kernel(u_ref, v_ref, g_ref, b_ref, ws_ref, bs_ref, o_ref):
    tm = u_ref.shape[0]
    u = _gelu(u_ref[...].astype(F32))
    v = _gelu(v_ref[...].astype(F32))
    mu = jnp.mean(v, -1, keepdims=True)
    vc = v - mu
    var = jnp.mean(vc * vc, -1, keepdims=True)
    v = (vc * lax.rsqrt(var + EPS) * g_ref[...] + b_ref[...]).astype(BF16)
    r = lax.broadcasted_iota(jnp.int32, (BLK, BLK), 0)
    c = lax.broadcasted_iota(jnp.int32, (BLK, BLK), 1)
    for g in range(B_GROUPS):
        w = jnp.where(c <= r, ws_ref[g], 0.0).astype(BF16)
        cols = slice(g * BLK, (g + 1) * BLK)
        for ch in range(tm // BLK):
            rows = slice(ch * BLK, (ch + 1) * BLK)
            mixed = _bdot(w, v[rows, cols]) + bs_ref[g]
            o_ref[rows, cols] = (u[rows, cols] * mixed).astype(o_ref.dtype)


def _gmlp(proj, ln_g, ln_b, w_s, b_s):
    b, s, n = proj.shape
    tm = 512
    return pl.pallas_call(
        _gmlp_kernel,
        grid=(b, s // tm),
        in_specs=[
            pl.BlockSpec((None, tm, WIDTH), lambda i, t: (i, t, 3)),
            pl.BlockSpec((None, tm, WIDTH), lambda i, t: (i, t, 4)),
            pl.BlockSpec((1, WIDTH), lambda i, t: (0, 0)),
            pl.BlockSpec((1, WIDTH), lambda i, t: (0, 0)),
            pl.BlockSpec((B_GROUPS, BLK, BLK), lambda i, t: (0, 0, 0)),
            pl.BlockSpec((B_GROUPS, BLK, 1), lambda i, t: (0, 0, 0)),
        ],
        out_specs=pl.BlockSpec((None, tm, WIDTH), lambda i, t: (i, t, 0)),
        out_shape=jax.ShapeDtypeStruct((b, s, WIDTH), BF16),
        compiler_params=_params(("arbitrary", "arbitrary")),
        name="gmlp_gate",
    )(proj, proj, ln_g.reshape(1, WIDTH), ln_b.reshape(1, WIDTH), w_s,
      b_s.reshape(B_GROUPS, BLK, 1))


def _conv_kernel(a_ref, g_ref, ah_ref, gh_ref, w_ref, bdw_ref, lng_ref, lnb_ref, o_ref,
                 hbuf, wbuf, ybuf):
    tm = a_ref.shape[0]
    t = pl.program_id(1)
    halo = ah_ref[...].astype(F32) * jax.nn.sigmoid(gh_ref[...].astype(F32))
    hbuf[0, 0:CONV_HALO] = jnp.where(t == 0, 0.0, halo)
    hbuf[0, CONV_HALO:] = a_ref[...].astype(F32) * jax.nn.sigmoid(g_ref[...].astype(F32))
    n_shifted = tm + CONV_HALO - SUBLANES
    for s in range(1, SUBLANES):
        hbuf[s, 0:n_shifted] = hbuf[0, pl.ds(s, n_shifted), :]
    for k in range(CONV_WIDTH):
        wbuf[k] = jnp.broadcast_to(w_ref[k:k + 1, :], (SUBLANES, WIDTH))
    rc = 32
    off = CONV_HALO - (CONV_WIDTH - 1)

    def chunk(ci, carry):
        base = pl.multiple_of(ci * rc, rc)
        def tap(k, acc):
            q = off + k
            s = q & (SUBLANES - 1)
            start = pl.multiple_of(base + (q - s), SUBLANES)
            win = hbuf[s, pl.ds(start, rc), :].reshape(rc // SUBLANES, SUBLANES, WIDTH)
            return acc + (wbuf[k] * win).reshape(rc, WIDTH)

        acc = lax.fori_loop(0, CONV_WIDTH, tap, jnp.zeros((rc, WIDTH), F32) + bdw_ref[...],
                            unroll=CONV_TAP_UNROLL)
        ybuf[pl.ds(base, rc), :] = acc
        return carry

    lax.fori_loop(0, tm // rc, chunk, 0)
    acc = ybuf[...]
    mu = jnp.mean(acc, -1, keepdims=True)
    xc = acc - mu
    var = jnp.mean(xc * xc, -1, keepdims=True)
    y = xc * lax.rsqrt(var + EPS) * lng_ref[...] + lnb_ref[...]
    o_ref[...] = (y * jax.nn.sigmoid(y)).astype(o_ref.dtype)


def _conv_module(proj, w_dw, b_dw, ln_g, ln_b):
    b, s, n = proj.shape
    tm = 512
    hb = tm // CONV_HALO
    row = lambda v: v.reshape(1, WIDTH)
    halo = lambda c: pl.BlockSpec(
        (None, CONV_HALO, WIDTH), lambda i, t: (i, jnp.maximum(t * hb - 1, 0), c))
    return pl.pallas_call(
        _conv_kernel,
        grid=(b, s // tm),
        in_specs=[
            pl.BlockSpec((None, tm, WIDTH), lambda i, t: (i, t, 0)),
            pl.BlockSpec((None, tm, WIDTH), lambda i, t: (i, t, 1)),
            halo(0), halo(1),
            pl.BlockSpec((CONV_WIDTH, WIDTH), lambda i, t: (0, 0)),
            pl.BlockSpec((1, WIDTH), lambda i, t: (0, 0)),
            pl.BlockSpec((1, WIDTH), lambda i, t: (0, 0)),
            pl.BlockSpec((1, WIDTH), lambda i, t: (0, 0)),
        ],
        out_specs=pl.BlockSpec((None, tm, WIDTH), lambda i, t: (i, t, 0)),
        out_shape=jax.ShapeDtypeStruct((b, s, WIDTH), BF16),
        scratch_shapes=[pltpu.VMEM((SUBLANES, CONV_HALO + tm, WIDTH), F32),
                        pltpu.VMEM((CONV_WIDTH, SUBLANES, WIDTH), F32),
                        pltpu.VMEM((tm, WIDTH), F32)],
        compiler_params=_params(("arbitrary", "arbitrary")),
        name="conv_module",
    )(proj, proj, proj, proj, w_dw, row(b_dw), row(ln_g), row(ln_b))


def _stick_kernel(q_ref, k_ref, v_ref, mm_ref, o_ref, q2_ref, carry_ref, acc_ref):
    i = pl.program_id(1)
    npair = WIDTH // PAIR
    lane = lax.broadcasted_iota(jnp.int32, (BLK, PAIR), 1)
    lo_lanes = lane < HEAD_DIM
    row2 = lax.broadcasted_iota(jnp.int32, (2 * BLK, BLK), 0)
    col2 = lax.broadcasted_iota(jnp.int32, (2 * BLK, BLK), 1)
    causal2 = col2 < (row2 & (BLK - 1))
    scale = HEAD_DIM ** -0.5
    for p in range(npair):
        qp = q_ref[:, p * PAIR:(p + 1) * PAIR] * scale
        zero = jnp.zeros_like(qp)
        q2_ref[p] = jnp.concatenate(
            [jnp.where(lo_lanes, qp, zero), jnp.where(lo_lanes, zero, qp)], axis=0)
    carry_ref[...] = jnp.zeros_like(carry_ref)
    acc_ref[...] = jnp.zeros_like(acc_ref)

    def blocks(js, diag):
        cols = lambda p: slice(p * PAIR, (p + 1) * PAIR)
        rows = lambda j: pl.ds(pl.multiple_of(j * BLK, BLK), BLK)

        def scores(j, p):
            return _bdot_nt(q2_ref[p], k_ref[rows(j), cols(p)])

        def keep_sums(z):
            soft = jnp.log(1.0 + jnp.exp2(jnp.abs(z) * -LOG2E))
            log_beta = jnp.minimum(z, 0.0) - soft
            log_keep = log_beta - z
            if diag:
                log_keep = jnp.where(causal2, log_keep, 0.0)
            return log_beta, _bdot(log_keep.astype(BF16), mm_ref[...])

        def weighted_values(j, p, log_beta, sums):
            carry = carry_ref[p]
            att = jnp.exp(log_beta + (sums[:, :BLK] + carry))
            if diag:
                att = jnp.where(causal2, att, 0.0)
            carry_ref[p] = carry + sums[:, BLK:]
            att = att.astype(BF16)
            vb = v_ref[rows(j), cols(p)]
            zero = jnp.zeros_like(vb)
            v2 = jnp.concatenate([jnp.where(lo_lanes, vb, zero), jnp.where(lo_lanes, zero, vb)], axis=0)
            return _bdot(jnp.concatenate([att[:BLK], att[BLK:]], axis=1), v2)

        chains = [(j, p) for j in js for p in range(npair)]
        z, ks, pv = {}, {}, {}
        for step in range(len(chains) + 3):
            if step < len(chains):
                z[step] = scores(*chains[step])
            if 0 <= step - 1 < len(chains):
                ks[step - 1] = keep_sums(z.pop(step - 1))
            if 0 <= step - 2 < len(chains):
                pv[step - 2] = weighted_values(*chains[step - 2], *ks.pop(step - 2))
            if 0 <= step - 3 < len(chains):
                acc_ref[chains[step - 3][1]] += pv.pop(step - 3)

    blocks([i], True)

    def live():
        c = carry_ref[...]
        worst = jnp.max(jnp.max(c, axis=0), axis=0, keepdims=True)
        return (worst[0, 0] >= EXP_IS_ZERO_BELOW).astype(jnp.int32)

    def two_blocks(state):
        t, _ = state
        j = i - 1 - 2 * t
        blocks([j, j - 1], False)
        return t + 1, live()

    _, alive = lax.while_loop(lambda st: (st[0] < i // 2) & (st[1] == 1), two_blocks,
                              (jnp.int32(0), jnp.int32(1)))

    @pl.when((i % 2 == 1) & (alive == 1))
    def _():
        blocks([0], False)

    for p in range(npair):
        o_ref[:, p * PAIR:(p + 1) * PAIR] = acc_ref[p].astype(o_ref.dtype)


def _suffix_sum_matrix():
    sp = np.arange(BLK)[:, None]
    sc = np.arange(BLK)[None, :]
    return jnp.asarray(
        np.concatenate([(sp > sc).astype(np.float32), np.ones((BLK, BLK), np.float32)], axis=1), BF16)


def _stick_breaking(proj):
    b, s, n = proj.shape
    npair = WIDTH // PAIR
    return pl.pallas_call(
        _stick_kernel,
        grid=(b, s // BLK),
        in_specs=[
            pl.BlockSpec((None, BLK, WIDTH), lambda i, t: (i, t, 2)),
            pl.BlockSpec((None, s, WIDTH), lambda i, t: (i, 0, 3)),
            pl.BlockSpec((None, s, WIDTH), lambda i, t: (i, 0, 4)),
            pl.BlockSpec((BLK, 2 * BLK), lambda i, t: (0, 0)),
        ],
        out_specs=pl.BlockSpec((None, BLK, WIDTH), lambda i, t: (i, t, 0)),
        out_shape=jax.ShapeDtypeStruct((b, s, WIDTH), BF16),
        scratch_shapes=[pltpu.VMEM((npair, 2 * BLK, PAIR), BF16),
                        pltpu.VMEM((npair, 2 * BLK, BLK), F32),
                        pltpu.VMEM((npair, BLK, PAIR), F32)],
        compiler_params=_params(("arbitrary", "arbitrary")),
        name="stick_breaking",
    )(proj, proj, proj, _suffix_sum_matrix())


def _lane_min_index(mask, lane):
    return jnp.min(jnp.where(mask, lane, float(LANES)), -1, keepdims=True)


def _out_kernel(l_ref, r_ref, w_ref, x_ref, mod_ref, g_ref, rh_ref, rl_ref, rb_ref, tri_ref,
                xo_ref, h_ref, cls_ref, rank_ref, cnt_ref, seen):
    tm = x_ref.shape[0]

    @pl.when((pl.program_id(0) == 0) & (pl.program_id(1) == 0))
    def _():
        seen[...] = jnp.zeros_like(seen)

    m = mod_ref[...]
    mix = _bdot(l_ref[...], w_ref[0:WIDTH, :]) + _bdot(r_ref[...], w_ref[WIDTH:, :])
    x = x_ref[...] + m[2:3] * mix
    xo_ref[...] = x
    h = _rms_mod(x, g_ref[...], m[4:5], m[3:4])
    h_ref[...] = h
    logits = _dot3(h, rh_ref[...], rl_ref[...]) + rb_ref[...]
    lane = lax.broadcasted_iota(jnp.int32, logits.shape, 1).astype(F32)
    is_group = (lane >= N_EXPERTS) & (lane < N_EXPERTS + N_GROUPS)
    glog = jnp.where(is_group, logits, -jnp.inf)
    gmax = jnp.max(glog, -1, keepdims=True)
    g_idx = _lane_min_index(glog == gmax, lane) - N_EXPERTS
    in_group = (lane >= g_idx * EXPERTS_PER_GROUP) & (lane < (g_idx + 1) * EXPERTS_PER_GROUP)
    ev = jnp.where(in_group, logits, -jnp.inf)
    v1 = jnp.max(ev, -1, keepdims=True)
    i1 = _lane_min_index(ev == v1, lane)
    ev2 = jnp.where(lane == i1, -jnp.inf, ev)
    v2 = jnp.max(ev2, -1, keepdims=True)
    i2 = _lane_min_index(ev2 == v2, lane)
    lo = jnp.minimum(i1, i2) - g_idx * EXPERTS_PER_GROUP
    hi = jnp.maximum(i1, i2) - g_idx * EXPERTS_PER_GROUP
    cls = g_idx * PAIRS_PER_GROUP + (lo * (7.0 - lo) * 0.5 + (hi - lo - 1.0))
    cls_t = jnp.transpose(jnp.broadcast_to(cls, (tm, LANES)))
    onehot_t = lax.broadcasted_iota(jnp.int32, (LANES, tm), 0).astype(F32) == cls_t
    ones_t = jnp.where(onehot_t, 1.0, 0.0)
    earlier = _bdot(ones_t.astype(BF16), tri_ref[...]) + seen[...]
    cls_ref[...] = cls_t[0:1, :].astype(jnp.int32)
    rank_ref[...] = jnp.sum(jnp.where(onehot_t, earlier, 0.0), 0, keepdims=True).astype(jnp.int32)
    seen[...] += jnp.sum(ones_t, 1, keepdims=True)
    cnt_ref[...] = seen[...]


def _out_proj(left, right, w_out, x, mod, g, r_hi, r_lo, r_b):
    b, s, d = x.shape
    tm = TOKEN_TILE
    nt = s // tm
    tile = lambda w: pl.BlockSpec((None, tm, w), lambda i, t: (i, t, 0))
    const = lambda shp: pl.BlockSpec(shp, lambda i, t: tuple(0 for _ in shp))
    row = pl.BlockSpec((None, 1, tm), lambda i, t: (i * nt + t, 0, 0))
    tri = np.arange(tm)[:, None] < np.arange(tm)[None, :]
    return pl.pallas_call(
        _out_kernel,
        grid=(b, nt),
        in_specs=[tile(WIDTH), tile(WIDTH), const((2 * WIDTH, d)), tile(d),
                  pl.BlockSpec((None, 6, d), lambda i, t: (i, 0, 0)),
                  const((1, d)), const((d, LANES)), const((d, LANES)), const((1, LANES)),
                  const((tm, tm))],
        out_specs=[tile(d), tile(d), row, row, const((LANES, 1))],
        out_shape=[jax.ShapeDtypeStruct((b, s, d), F32),
                   jax.ShapeDtypeStruct((b, s, d), F32),
                   jax.ShapeDtypeStruct((b * nt, 1, tm), jnp.int32),
                   jax.ShapeDtypeStruct((b * nt, 1, tm), jnp.int32),
                   jax.ShapeDtypeStruct((LANES, 1), F32)],
        scratch_shapes=[pltpu.VMEM((LANES, 1), F32)],
        compiler_params=_params(("arbitrary", "arbitrary")),
        name="out_proj_router",
    )(left, right, w_out, x, mod, g.reshape(1, d), r_hi, r_lo, r_b, jnp.asarray(tri, BF16))


def _router_weights(w_group, b_group, w_router, b_router):
    d = w_group.shape[0]
    w = jnp.concatenate([jnp.transpose(w_router, (1, 0, 2)).reshape(d, N_EXPERTS), w_group], axis=1)
    w = jnp.pad(w.astype(F32), ((0, 0), (0, LANES - w.shape[1])))
    bias = jnp.concatenate([b_router.reshape(N_EXPERTS), b_group]).astype(F32)
    bias = jnp.pad(bias, (0, LANES - bias.shape[0])).reshape(1, LANES)
    hi, lo = _split(w)
    return hi, lo, bias


def _to_token_tiles(x):
    n = x.shape[0]
    parts = [x[:, g * LANES:(g + 1) * LANES].reshape(n // 8, 8, LANES) for g in range(8)]
    return jnp.swapaxes(jnp.stack(parts, axis=1), 1, 2).reshape(n, 8, LANES)


def _from_token_tiles(v):
    n = v.shape[0]
    w = jnp.swapaxes(v.reshape(n // 8, 8, 8, LANES), 1, 2)
    return jnp.concatenate([w[:, g].reshape(n, LANES) for g in range(8)], axis=-1)


def _row_dma_loop(n, start_row):
    def body(r, carry):
        start_row(r)
        return carry
    lax.fori_loop(0, n, body, 0, unroll=8)


def _dispatch_kernel(pad_ref, dest_ref, h_ref, hs_ref, buf, sem):
    tm = h_ref.shape[0]

    @pl.when(pl.program_id(0) == 0)
    def _():
        buf[...] = jnp.zeros_like(buf)
        for c in range(pad_ref.shape[0]):
            fill = pltpu.make_async_copy(buf, hs_ref.at[pl.ds(pad_ref[c], tm)], sem)
            fill.start()
            fill.wait()

    buf[...] = _to_token_tiles(h_ref[...])
    _row_dma_loop(tm, lambda r: pltpu.make_async_copy(
        buf.at[r], hs_ref.at[dest_ref[0, r]], sem).start())
    pltpu.make_async_copy(buf, hs_ref.at[pl.ds(0, tm)], sem).wait()


def _dispatch(h, dest, pad_start, n_slots):
    b, s, d = h.shape
    tm = TOKEN_TILE
    assert tm == SLOT_TILE
    nt = s // tm
    return pl.pallas_call(
        _dispatch_kernel,
        grid_spec=pltpu.PrefetchScalarGridSpec(
            num_scalar_prefetch=1,
            grid=(b * nt,),
            in_specs=[pl.BlockSpec((None, 1, tm), lambda i, pad: (i, 0, 0), memory_space=pltpu.SMEM),
                      pl.BlockSpec((None, tm, d), lambda i, pad: (i // nt, i % nt, 0))],
            out_specs=pl.BlockSpec(memory_space=pl.ANY),
            scratch_shapes=[pltpu.VMEM((tm, 8, LANES), F32), pltpu.SemaphoreType.DMA(())]),
        out_shape=jax.ShapeDtypeStruct((n_slots, 8, LANES), F32),
        compiler_params=_params(("arbitrary",)),
        name="moe_dispatch",
    )(pad_start, dest, h)


def _moe_kernel(blk_ref, ea_ref, eb_ref, grp_ref, used_ref, hs_ref, wga_ref, wua_ref, wda_ref,
                wgb_ref, wub_ref, wdb_ref, rh_ref, rb_ref, o_ref):
    n = pl.program_id(0)

    @pl.when(used_ref[n] == 1)
    def _():
        h = _from_token_tiles(hs_ref[...]).astype(BF16)
        logits = _bdot(h, rh_ref[...]) + rb_ref[...]
        lane = lax.broadcasted_iota(jnp.int32, logits.shape, 1)
        pick = lambda idx: jnp.sum(jnp.where(lane == idx, logits, 0.0), -1, keepdims=True)
        is_group = (lane >= N_EXPERTS) & (lane < N_EXPERTS + N_GROUPS)
        g_w = 1.0 / jnp.sum(
            jnp.where(is_group, jnp.exp(logits - pick(N_EXPERTS + grp_ref[n])), 0.0),
            -1, keepdims=True)
        la = pick(ea_ref[n])
        lb = pick(eb_ref[n])

        def expert(wg_ref, wu_ref, wd_ref):
            a = _bdot(h, wg_ref[...])
            hid = (a * jax.nn.sigmoid(a)) * _bdot(h, wu_ref[...])
            return _bdot(hid.astype(BF16), wd_ref[...])

        y = (g_w * (1.0 / (1.0 + jnp.exp(lb - la)))) * expert(wga_ref, wua_ref, wda_ref)
        y = y + (g_w * (1.0 / (1.0 + jnp.exp(la - lb)))) * expert(wgb_ref, wub_ref, wdb_ref)
        o_ref[...] = _to_token_tiles(y)

    @pl.when(used_ref[n] == 0)
    def _():
        o_ref[...] = jnp.zeros_like(o_ref)


def _moe(hs, plan, w_gate, w_up, w_down, r_hi, r_b):
    n_slots = hs.shape[0]
    ne, d, de = w_gate.shape
    tm = SLOT_TILE
    slot = pl.BlockSpec((tm, 8, LANES), lambda n, *_: (n, 0, 0))
    slot_in = pl.BlockSpec((tm, 8, LANES), lambda n, blk, *_: (blk[n], 0, 0))
    first = lambda shp: pl.BlockSpec((None,) + shp, lambda n, blk, ea, eb, grp, used: (ea[n], 0, 0))
    second = lambda shp: pl.BlockSpec((None,) + shp, lambda n, blk, ea, eb, grp, used: (eb[n], 0, 0))
    const = lambda shp: pl.BlockSpec(shp, lambda n, *_: tuple(0 for _ in shp))
    return pl.pallas_call(
        _moe_kernel,
        grid_spec=pltpu.PrefetchScalarGridSpec(
            num_scalar_prefetch=5,
            grid=(n_slots // tm,),
            in_specs=[slot_in, first((d, de)), first((d, de)), first((de, d)),
                      second((d, de)), second((d, de)), second((de, d)),
                      const((d, LANES)), const((1, LANES))],
            out_specs=slot),
        out_shape=jax.ShapeDtypeStruct((n_slots, 8, LANES), F32),
        compiler_params=_params(("arbitrary",)),
        name="moe_experts",
    )(*plan, hs, w_gate, w_up, w_down, w_gate, w_up, w_down, r_hi, r_b)


def _combine_kernel(dest_ref, ys_ref, x_ref, mod_ref, fg_ref, o_ref, buf, sem):
    tm = x_ref.shape[0]
    _row_dma_loop(tm, lambda r: pltpu.make_async_copy(
        ys_ref.at[dest_ref[0, r]], buf.at[r], sem).start())
    pltpu.make_async_copy(ys_ref.at[pl.ds(0, tm)], buf, sem).wait()
    x = x_ref[...] + mod_ref[5:6, :] * _from_token_tiles(buf[...])
    o_ref[...] = x * lax.rsqrt(jnp.mean(x * x, -1, keepdims=True) + EPS) * fg_ref[...]


def _combine(ys, dest, x, mod, final_g):
    b, s, d = x.shape
    tm = TOKEN_TILE
    nt = s // tm
    return pl.pallas_call(
        _combine_kernel,
        grid=(b * nt,),
        in_specs=[pl.BlockSpec((None, 1, tm), lambda i: (i, 0, 0), memory_space=pltpu.SMEM),
                  pl.BlockSpec(memory_space=pl.ANY),
                  pl.BlockSpec((None, tm, d), lambda i: (i // nt, i % nt, 0)),
                  pl.BlockSpec((None, 6, d), lambda i: (i // nt, 0, 0)),
                  pl.BlockSpec((1, d), lambda i: (0, 0))],
        out_specs=pl.BlockSpec((None, tm, d), lambda i: (i // nt, i % nt, 0)),
        out_shape=jax.ShapeDtypeStruct((b, s, d), F32),
        scratch_shapes=[pltpu.VMEM((tm, 8, LANES), F32), pltpu.SemaphoreType.DMA(())],
        compiler_params=_params(("arbitrary",)),
        name="moe_combine",
    )(dest, ys, x, mod, final_g.reshape(1, d))


def _combine_in_kernel(dest_ref, next_ref, ys_ref, x_ref, mod_ref, g_ref, nmod_ref, w_ref,
                       xo_ref, o_ref, *rest, dilations):
    *rest, buf, sem = rest
    tm = x_ref.shape[0]
    i = pl.program_id(0)
    slot = i % 2

    def gather(idx_ref, to):
        _row_dma_loop(tm, lambda r: pltpu.make_async_copy(
            ys_ref.at[idx_ref[0, r]], buf.at[to].at[r], sem.at[to]).start())

    @pl.when(i == 0)
    def _():
        gather(dest_ref, 0)

    @pl.when(i + 1 < pl.num_programs(0))
    def _():
        gather(next_ref, 1 - slot)

    pltpu.make_async_copy(ys_ref.at[pl.ds(0, tm)], buf.at[slot], sem.at[slot]).wait()
    x = x_ref[...] + mod_ref[5:6, :] * _from_token_tiles(buf[slot])
    xo_ref[...] = x
    _in_body(x, g_ref, nmod_ref, w_ref, o_ref, rest, dilations)


def _combine_in_proj(ys, dest, x, mod, g_next, mod_next, w_in, dilations=()):
    b, s, d = x.shape
    n = w_in.shape[1]
    tm = TOKEN_TILE
    nt = s // tm
    last = b * nt - 1
    tile = lambda i: (i // nt, i % nt, 0)
    specs, shapes, scratch = _in_proj_outputs(b, s, n, tm, dilations, tile)
    outs = pl.pallas_call(
        functools.partial(_combine_in_kernel, dilations=tuple(dilations)),
        grid=(b * nt,),
        in_specs=[pl.BlockSpec((None, 1, tm), lambda i: (i, 0, 0), memory_space=pltpu.SMEM),
                  pl.BlockSpec((None, 1, tm), lambda i: (jnp.minimum(i + 1, last), 0, 0),
                               memory_space=pltpu.SMEM),
                  pl.BlockSpec(memory_space=pl.ANY),
                  pl.BlockSpec((None, tm, d), tile),
                  pl.BlockSpec((None, 6, d), lambda i: (i // nt, 0, 0)),
                  pl.BlockSpec((1, d), lambda i: (0, 0)),
                  pl.BlockSpec((None, 6, d), lambda i: (i // nt, 0, 0)),
                  pl.BlockSpec((d, n), lambda i: (0, 0))],
        out_specs=[pl.BlockSpec((None, tm, d), tile)] + specs,
        out_shape=[jax.ShapeDtypeStruct((b, s, d), F32)] + shapes,
        scratch_shapes=scratch + [pltpu.VMEM((2, tm, 8, LANES), F32), pltpu.SemaphoreType.DMA((2,))],
        compiler_params=_params(("arbitrary",)),
        name="moe_combine_in_proj",
    )(dest, dest, ys, x, mod, g_next.reshape(1, d), mod_next, w_in)
    return outs


def _routing_plan(cls, rank, counts, n_tokens):
    tm = SLOT_TILE
    n_tiles = n_tokens // tm + N_CLASSES
    counts = counts[:N_CLASSES, 0].astype(jnp.int32)
    tiles = (counts + tm - 1) // tm
    tile_end = jnp.cumsum(tiles)
    classes = jnp.arange(N_CLASSES, dtype=jnp.int32)
    first_slot = (tile_end - tiles) * tm
    dest = jnp.sum(jnp.where(cls[..., None] == classes, first_slot, 0), -1) + rank
    ids = jnp.arange(n_tiles, dtype=jnp.int32)
    used = ids < tile_end[-1]
    tile_cls = jnp.sum(
        (jnp.minimum(ids, tile_end[-1] - 1)[:, None] >= tile_end).astype(jnp.int32), -1)
    grp = tile_cls // PAIRS_PER_GROUP
    pair = tile_cls % PAIRS_PER_GROUP
    pair_a = jnp.asarray([0, 0, 0, 1, 1, 2], jnp.int32)
    pair_b = jnp.asarray([1, 2, 3, 2, 3, 3], jnp.int32)
    eb = grp * EXPERTS_PER_GROUP + pair_b[pair]
    ea = jnp.where(used, grp * EXPERTS_PER_GROUP + pair_a[pair], eb)
    blk = jnp.where(used, ids, 0)
    tail = jnp.minimum(tile_end[-1] * tm + classes * tm, (n_tiles - 1) * tm)
    zero_blocks = jnp.concatenate([first_slot + counts, tail])
    return dest, zero_blocks, (blk, ea, eb, grp, used.astype(jnp.int32))


@jax.jit
def kernel(x, c, rel_bias, norm1_g, norm2_g, w_ada, b_ada, ev_w_in, ev_w_out, ev_gmlp_ln_g, ev_gmlp_ln_b, ev_w_s, ev_b_s, od_w_in, od_w_out, od_w_dw, od_b_dw, od_conv_ln_g, od_conv_ln_b, moe_w_group, moe_b_group, moe_w_router, moe_b_router, moe_w_gate, moe_w_up, moe_w_down, final_norm_g):
    b, s, d = x.shape
    depth = w_ada.shape[0]
    mods = _ada(c, w_ada, b_ada).reshape(depth, b, 6, d)
    biases = [_branch_bias(rel_bias, dil) for _, dil in DILATED_BRANCHES]
    dils = [dil for _, dil in DILATED_BRANCHES]
    grouped_dils = [dil for dil in dils if dil > 1]

    def in_weights(layer):
        even = layer % 2 == 0
        return (ev_w_in if even else od_w_in)[layer // 2].astype(BF16), grouped_dils if even else []

    w_in, extra = in_weights(0)
    proj, *grouped = _in_proj(x, norm1_g[0], mods[0], w_in, extra)
    for layer in range(depth):
        j = layer // 2
        mod = mods[layer]
        if layer % 2 == 0:
            views = {1: (proj, proj.shape[-1] // WIDTH)}
            views.update({dil: (g, 3) for dil, g in zip(grouped_dils, grouped)})
            outs = [_dilated_branch(views[dil][0], bias, dil, views[dil][1])
                    for bias, dil in zip(biases, dils)]
            left = _combine_branches(outs, dils, s)
            right = _gmlp(proj, ev_gmlp_ln_g[j], ev_gmlp_ln_b[j], ev_w_s[j], ev_b_s[j])
            w_out = ev_w_out[j]
        else:
            left = _conv_module(proj, od_w_dw[j], od_b_dw[j], od_conv_ln_g[j], od_conv_ln_b[j])
            right = _stick_breaking(proj)
            w_out = od_w_out[j]
        r_hi, r_lo, r_b = _router_weights(moe_w_group[layer], moe_b_group[layer],
                                          moe_w_router[layer], moe_b_router[layer])
        x, h, cls, rank, counts = _out_proj(left, right, w_out.astype(BF16), x, mod,
                                            norm2_g[layer], r_hi, r_lo, r_b)
        dest, pad_start, plan = _routing_plan(cls, rank, counts, b * s)
        hs = _dispatch(h, dest, pad_start, (b * s // SLOT_TILE + N_CLASSES) * SLOT_TILE)
        ys = _moe(hs, plan, moe_w_gate[layer].astype(BF16), moe_w_up[layer].astype(BF16),
                  moe_w_down[layer].astype(BF16), r_hi, r_b)
        if layer == depth - 1:
            x = _combine(ys, dest, x, mod, final_norm_g)
        else:
            w_in, extra = in_weights(layer + 1)
            x, proj, *grouped = _combine_in_proj(ys, dest, x, mod, norm1_g[layer + 1],
                                                 mods[layer + 1], w_in, extra)
    return x
```

```python
import functools
import math

import numpy as np
import jax
import jax.numpy as jnp
from jax import lax
from jax.experimental import pallas as pl
from jax.experimental.pallas import tpu as pltpu

F32 = jnp.float32
BF16 = jnp.bfloat16

HEAD_DIM = 64
HEADS = 8
WIDTH = HEADS * HEAD_DIM
PAIR = 2 * HEAD_DIM
BLK = 128
DILATED_BRANCHES = ((128, 1), (512, 4), (2048, 16))
N_BUCKETS = 32
MAX_DISTANCE = 2048
CONV_WIDTH = 31
CONV_HALO = 32
CONV_TAP_UNROLL = 4
B_GROUPS = 4
N_GROUPS = 4
EXPERTS_PER_GROUP = 4
N_EXPERTS = N_GROUPS * EXPERTS_PER_GROUP
PAIRS_PER_GROUP = EXPERTS_PER_GROUP * (EXPERTS_PER_GROUP - 1) // 2
N_CLASSES = N_GROUPS * PAIRS_PER_GROUP
TOKEN_TILE = 512
SLOT_TILE = 512
EPS = 1e-6
NEG = -1e30
LOG2E = 1.4426950408889634
EXP_IS_ZERO_BELOW = -104.0
LANES = 128
SUBLANES = 8
VMEM_LIMIT = 48 * 1024 * 1024


def _params(sem):
    return pltpu.CompilerParams(dimension_semantics=sem, vmem_limit_bytes=VMEM_LIMIT)


def _bdot(a, b):
    return jnp.dot(a, b, preferred_element_type=F32)


def _bdot_nt(a, b):
    return lax.dot_general(a, b, (((1,), (1,)), ((), ())), preferred_element_type=F32)


def _split(a):
    hi = a.astype(BF16)
    lo = (a - hi.astype(F32)).astype(BF16)
    return hi, lo


def _dot3(a, w_hi, w_lo):
    a_hi, a_lo = _split(a)
    return _bdot(a_hi, w_hi) + (_bdot(a_hi, w_lo) + _bdot(a_lo, w_hi))


def _ada_kernel(c_ref, w_ref, b_ref, o_ref):
    c = c_ref[...]
    sc = c * jax.nn.sigmoid(c)
    w_hi, w_lo = _split(w_ref[...])
    o_ref[...] = _dot3(sc, w_hi, w_lo) + b_ref[...]


def _ada(c, w_ada, b_ada):
    depth, d, n = w_ada.shape
    b = c.shape[0]
    tn = 1024
    return pl.pallas_call(
        _ada_kernel,
        grid=(depth, n // tn),
        in_specs=[
            pl.BlockSpec((b, d), lambda l, j: (0, 0)),
            pl.BlockSpec((None, d, tn), lambda l, j: (l, 0, j)),
            pl.BlockSpec((None, 1, tn), lambda l, j: (l, 0, j)),
        ],
        out_specs=pl.BlockSpec((None, b, tn), lambda l, j: (l, 0, j)),
        out_shape=jax.ShapeDtypeStruct((depth, b, n), F32),
        compiler_params=_params(("arbitrary", "arbitrary")),
        name="ada_mod",
    )(c, w_ada, b_ada.reshape(depth, 1, n))


def _rms_mod(x, g, scale_row, shift_row):
    y = x * lax.rsqrt(jnp.mean(x * x, -1, keepdims=True) + EPS) * g
    return y * (1.0 + scale_row) + shift_row


def _in_body(x, g_ref, mod_ref, w_ref, o_ref, rest, dilations):
    m = mod_ref[...]
    h = _rms_mod(x, g_ref[...], m[1:2], m[0:1]).astype(BF16)
    tm = x.shape[0]
    n = w_ref.shape[1]
    per = WIDTH // LANES
    for j in range(n // WIDTH):
        cols = slice(j * WIDTH, (j + 1) * WIDTH)
        y = _bdot(h, w_ref[:, cols])
        o_ref[:, cols] = y.astype(o_ref.dtype)
        if dilations and j < 3:
            for c in range(per):
                rest[-1][j * per + c] = y[:, c * LANES:(c + 1) * LANES]
    for d, out in zip(dilations, rest):
        for r in range(d):
            for c in range(3 * per):
                lo = (r * 3 * per + c) * LANES
                out[:, lo:lo + LANES] = rest[-1][c, pl.ds(r, tm // d, stride=d), :].astype(out.dtype)


def _in_kernel(x_ref, g_ref, mod_ref, w_ref, o_ref, *rest, dilations):
    _in_body(x_ref[...], g_ref, mod_ref, w_ref, o_ref, rest, dilations)


def _in_proj_outputs(b, s, n, tm, dilations, index):
    qkv = 3 * WIDTH
    specs = [pl.BlockSpec((None, tm, n), index)] + [
        pl.BlockSpec((None, tm // dil, dil * qkv), index) for dil in dilations]
    shapes = [jax.ShapeDtypeStruct((b, s, n), BF16)] + [
        jax.ShapeDtypeStruct((b, s // dil, dil * qkv), BF16) for dil in dilations]
    scratch = [pltpu.VMEM((qkv // LANES, tm, LANES), F32)] if dilations else []
    return specs, shapes, scratch


def _in_proj(x, g, mod, w_in, dilations=()):
    b, s, d = x.shape
    n = w_in.shape[1]
    tm = TOKEN_TILE
    specs, shapes, scratch = _in_proj_outputs(b, s, n, tm, dilations, lambda i, t: (i, t, 0))
    outs = pl.pallas_call(
        functools.partial(_in_kernel, dilations=tuple(dilations)),
        grid=(b, s // tm),
        in_specs=[
            pl.BlockSpec((None, tm, d), lambda i, t: (i, t, 0)),
            pl.BlockSpec((1, d), lambda i, t: (0, 0)),
            pl.BlockSpec((None, 6, d), lambda i, t: (i, 0, 0)),
            pl.BlockSpec((d, n), lambda i, t: (0, 0)),
        ],
        out_specs=specs,
        out_shape=shapes,
        scratch_shapes=scratch,
        compiler_params=_params(("arbitrary", "arbitrary")),
        name="in_proj",
    )(x, g.reshape(1, d), mod, w_in)
    return outs if dilations else outs[0]


def _t5_bucket(dist):
    max_exact = N_BUCKETS // 2
    d = np.maximum(dist, 1).astype(np.float32)
    large = max_exact + (np.log(d / max_exact) / np.log(MAX_DISTANCE / max_exact)
                         * (N_BUCKETS - max_exact)).astype(np.int32)
    large = np.minimum(large, N_BUCKETS - 1)
    return np.where(dist < max_exact, dist, large).astype(np.int32)


def _branch_bias(rel_table, dilation):
    qi = np.arange(BLK)[:, None]
    kj = np.arange(2 * BLK)[None, :]
    bucket = _t5_bucket(np.clip(qi + BLK - kj, 0, None) * dilation)
    onehot = np.eye(N_BUCKETS, dtype=np.float32)[bucket]
    bias = jnp.einsum("qkb,bh->hqk", onehot, rel_table.astype(F32), precision=lax.Precision.HIGHEST)
    dist = qi + BLK - kj
    in_window = (dist >= 0) & (dist <= BLK)
    return jnp.where(in_window[None], bias, NEG)


def _dil_kernel(q_ref, kp_ref, kc_ref, vp_ref, vc_ref, bias_ref, o_ref, m_ref, l_ref,
                kbuf, vbuf, *, tq):
    n = pl.program_id(2)
    nsub = tq // BLK
    kbuf[0:BLK] = kp_ref[...]
    kbuf[BLK:] = kc_ref[...]
    vbuf[0:BLK] = vp_ref[...]
    vbuf[BLK:] = vc_ref[...]
    kj = lax.broadcasted_iota(jnp.int32, (BLK, 2 * BLK), 1)
    lane = lax.broadcasted_iota(jnp.int32, (BLK, PAIR), 1)
    scale = HEAD_DIM ** -0.5
    has_prev = kj >= jnp.where(n == 0, BLK, 0)
    for i in range(nsub):
        rows = slice(i * BLK, (i + 1) * BLK)
        krows = slice(i * BLK, (i + 2) * BLK)
        m_tile = jnp.zeros((BLK, LANES), F32)
        l_tile = jnp.ones((BLK, LANES), F32)
        for p in range(HEADS // 2):
            cols = slice(p * PAIR, (p + 1) * PAIR)
            q_pair = q_ref[rows, cols] * scale
            k_pair = kbuf[krows, cols]
            v_pair = vbuf[krows, cols]
            outs = []
            for hh in range(2):
                h = 2 * p + hh
                head_lanes = (lane >= HEAD_DIM) if hh else (lane < HEAD_DIM)
                qm = jnp.where(head_lanes, q_pair, jnp.zeros_like(q_pair))
                logit = _bdot_nt(qm, k_pair) + bias_ref[h]
                if i == 0:
                    logit = jnp.where(has_prev, logit, NEG)
                mx = jnp.max(logit, -1, keepdims=True)
                pr = jnp.exp(logit - mx)
                den = jnp.sum(pr, -1, keepdims=True)
                outs.append(_bdot(pr.astype(BF16), v_pair) / den)
                m_tile = jnp.where(lane == h, mx, m_tile)
                l_tile = jnp.where(lane == h, den, l_tile)
            o_ref[rows, cols] = jnp.where(lane < HEAD_DIM, outs[0], outs[1]).astype(o_ref.dtype)
        m_ref[rows, :] = m_tile
        l_ref[rows, :] = l_tile


def _dilated_branch(view, bias, dilation, ncol):
    b, sub, _ = view.shape
    tq = min(512, sub)
    nsub = tq // BLK
    cur = lambda c: pl.BlockSpec((None, tq, WIDTH), lambda i, r, t: (i, t, r * ncol + c))
    prev = lambda c: pl.BlockSpec(
        (None, BLK, WIDTH), lambda i, r, t: (i, jnp.maximum(t * nsub - 1, 0), r * ncol + c))
    return pl.pallas_call(
        functools.partial(_dil_kernel, tq=tq),
        grid=(b, dilation, sub // tq),
        in_specs=[cur(0), prev(1), cur(1), prev(2), cur(2),
                  pl.BlockSpec((HEADS, BLK, 2 * BLK), lambda i, r, t: (0, 0, 0))],
        out_specs=[
            pl.BlockSpec((None, tq, WIDTH), lambda i, r, t: (i, t, r)),
            pl.BlockSpec((None, tq, LANES), lambda i, r, t: (i, t, r)),
            pl.BlockSpec((None, tq, LANES), lambda i, r, t: (i, t, r)),
        ],
        out_shape=[
            jax.ShapeDtypeStruct((b, sub, dilation * WIDTH), BF16),
            jax.ShapeDtypeStruct((b, sub, dilation * LANES), F32),
            jax.ShapeDtypeStruct((b, sub, dilation * LANES), F32),
        ],
        scratch_shapes=[pltpu.VMEM((BLK + tq, WIDTH), BF16), pltpu.VMEM((BLK + tq, WIDTH), BF16)],
        compiler_params=_params(("arbitrary", "arbitrary", "arbitrary")),
        name=f"dilated_d{dilation}",
    )(view, view, view, view, view, bias)


def _comb_kernel(*refs, dilations):
    nb = len(dilations)
    o_refs, m_refs, l_refs = refs[:nb], refs[nb:2 * nb], refs[2 * nb:3 * nb]
    e_ref, out_ref, o_scr, s_scr = refs[3 * nb:]
    tm = out_ref.shape[0]
    per = WIDTH // LANES
    os, ms, ls = [], [], []
    for bi, d in enumerate(dilations):
        if d == 1:
            os.append([o_refs[bi][:, c * LANES:(c + 1) * LANES].astype(F32) for c in range(per)])
            ms.append(m_refs[bi][...])
            ls.append(l_refs[bi][...])
            continue
        rows = lambda r: pl.ds(r, tm // d, stride=d)
        for r in range(d):
            for c in range(per):
                lo = (r * per + c) * LANES
                o_scr[bi, c, rows(r), :] = o_refs[bi][:, lo:lo + LANES].astype(F32)
            s_scr[bi, 0, rows(r), :] = m_refs[bi][:, r * LANES:(r + 1) * LANES]
            s_scr[bi, 1, rows(r), :] = l_refs[bi][:, r * LANES:(r + 1) * LANES]
        os.append([o_scr[bi, c] for c in range(per)])
        ms.append(s_scr[bi, 0])
        ls.append(s_scr[bi, 1])
    m_all = functools.reduce(jnp.maximum, ms)
    es = [l * jnp.exp(m - m_all) for l, m in zip(ls, ms)]
    tot = functools.reduce(lambda x, y: x + y, es)
    acc = [None] * per
    for e, o in zip(es, os):
        w_hi, w_lo = _split(e / tot)
        w = _bdot(w_hi, e_ref[...]) + _bdot(w_lo, e_ref[...])
        for c in range(per):
            term = w[:, c * LANES:(c + 1) * LANES] * o[c]
            acc[c] = term if acc[c] is None else acc[c] + term
    for c in range(per):
        out_ref[:, c * LANES:(c + 1) * LANES] = acc[c].astype(out_ref.dtype)


def _combine_branches(outs, dilations, s):
    b = outs[0][0].shape[0]
    tm = TOKEN_TILE
    nb = len(dilations)
    expand = np.zeros((LANES, WIDTH), np.float32)
    for h in range(HEADS):
        expand[h, h * HEAD_DIM:(h + 1) * HEAD_DIM] = 1.0
    grouped = lambda w: [pl.BlockSpec((None, tm // d, d * w), lambda i, t: (i, t, 0)) for d in dilations]
    return pl.pallas_call(
        functools.partial(_comb_kernel, dilations=tuple(dilations)),
        grid=(b, s // tm),
        in_specs=grouped(WIDTH) + grouped(LANES) + grouped(LANES)
        + [pl.BlockSpec((LANES, WIDTH), lambda i, t: (0, 0))],
        out_specs=pl.BlockSpec((None, tm, WIDTH), lambda i, t: (i, t, 0)),
        out_shape=jax.ShapeDtypeStruct((b, s, WIDTH), BF16),
        scratch_shapes=[pltpu.VMEM((nb, WIDTH // LANES, tm, LANES), F32),
                        pltpu.VMEM((nb, 2, tm, LANES), F32)],
        compiler_params=_params(("arbitrary", "arbitrary")),
        name="dilated_combine",
    )(*[o for o, _, _ in outs], *[m for _, m, _ in outs], *[l for _, _, l in outs],
      jnp.asarray(expand, BF16))


def _gelu(x):
    return 0.5 * x * (1.0 + jnp.tanh(math.sqrt(2.0 / math.pi) * (x + 0.044715 * (x * x * x))))


def _gmlp_kernel(u_ref, v_ref, g_ref, b_ref, ws_ref, bs_ref, o_ref):
    tm = u_ref.shape[0]
    u = _gelu(u_ref[...].astype(F32))
    v = _gelu(v_ref[...].astype(F32))
    mu = jnp.mean(v, -1, keepdims=True)
    vc = v - mu
    var = jnp.mean(vc * vc, -1, keepdims=True)
    v = (vc * lax.rsqrt(var + EPS) * g_ref[...] + b_ref[...]).astype(BF16)
    r = lax.broadcasted_iota(jnp.int32, (BLK, BLK), 0)
    c = lax.broadcasted_iota(jnp.int32, (BLK, BLK), 1)
    for g in range(B_GROUPS):
        w = jnp.where(c <= r, ws_ref[g], 0.0).astype(BF16)
        cols = slice(g * BLK, (g + 1) * BLK)
        for ch in range(tm // BLK):
            rows = slice(ch * BLK, (ch + 1) * BLK)
            mixed = _bdot(w, v[rows, cols]) + bs_ref[g]
            o_ref[rows, cols] = (u[rows, cols] * mixed).astype(o_ref.dtype)


def _gmlp(proj, ln_g, ln_b, w_s, b_s):
    b, s, n = proj.shape
    tm = 512
    return pl.pallas_call(
        _gmlp_kernel,
        grid=(b, s // tm),
        in_specs=[
            pl.BlockSpec((None, tm, WIDTH), lambda i, t: (i, t, 3)),
            pl.BlockSpec((None, tm, WIDTH), lambda i, t: (i, t, 4)),
            pl.BlockSpec((1, WIDTH), lambda i, t: (0, 0)),
            pl.BlockSpec((1, WIDTH), lambda i, t: (0, 0)),
            pl.BlockSpec((B_GROUPS, BLK, BLK), lambda i, t: (0, 0, 0)),
            pl.BlockSpec((B_GROUPS, BLK, 1), lambda i, t: (0, 0, 0)),
        ],
        out_specs=pl.BlockSpec((None, tm, WIDTH), lambda i, t: (i, t, 0)),
        out_shape=jax.ShapeDtypeStruct((b, s, WIDTH), BF16),
        compiler_params=_params(("arbitrary", "arbitrary")),
        name="gmlp_gate",
    )(proj, proj, ln_g.reshape(1, WIDTH), ln_b.reshape(1, WIDTH), w_s,
      b_s.reshape(B_GROUPS, BLK, 1))


def _conv_kernel(a_ref, g_ref, ah_ref, gh_ref, w_ref, bdw_ref, lng_ref, lnb_ref, o_ref,
                 hbuf, wbuf, ybuf):
    tm = a_ref.shape[0]
    t = pl.program_id(1)
    halo = ah_ref[...].astype(F32) * jax.nn.sigmoid(gh_ref[...].astype(F32))
    hbuf[0, 0:CONV_HALO] = jnp.where(t == 0, 0.0, halo)
    hbuf[0, CONV_HALO:] = a_ref[...].astype(F32) * jax.nn.sigmoid(g_ref[...].astype(F32))
    n_shifted = tm + CONV_HALO - SUBLANES
    for s in range(1, SUBLANES):
        hbuf[s, 0:n_shifted] = hbuf[0, pl.ds(s, n_shifted), :]
    for k in range(CONV_WIDTH):
        wbuf[k] = jnp.broadcast_to(w_ref[k:k + 1, :], (SUBLANES, WIDTH))
    rc = 32
    off = CONV_HALO - (CONV_WIDTH - 1)

    def chunk(ci, carry):
        base = pl.multiple_of(ci * rc, rc)
        def tap(k, acc):
            q = off + k
            s = q & (SUBLANES - 1)
            start = pl.multiple_of(base + (q - s), SUBLANES)
            win = hbuf[s, pl.ds(start, rc), :].reshape(rc // SUBLANES, SUBLANES, WIDTH)
            return acc + (wbuf[k] * win).reshape(rc, WIDTH)

        acc = lax.fori_loop(0, CONV_WIDTH, tap, jnp.zeros((rc, WIDTH), F32) + bdw_ref[...],
                            unroll=CONV_TAP_UNROLL)
        ybuf[pl.ds(base, rc), :] = acc
        return carry

    lax.fori_loop(0, tm // rc, chunk, 0)
    acc = ybuf[...]
    mu = jnp.mean(acc, -1, keepdims=True)
    xc = acc - mu
    var = jnp.mean(xc * xc, -1, keepdims=True)
    y = xc * lax.rsqrt(var + EPS) * lng_ref[...] + lnb_ref[...]
    o_ref[...] = (y * jax.nn.sigmoid(y)).astype(o_ref.dtype)


def _conv_module(proj, w_dw, b_dw, ln_g, ln_b):
    b, s, n = proj.shape
    tm = 512
    hb = tm // CONV_HALO
    row = lambda v: v.reshape(1, WIDTH)
    halo = lambda c: pl.BlockSpec(
        (None, CONV_HALO, WIDTH), lambda i, t: (i, jnp.maximum(t * hb - 1, 0), c))
    return pl.pallas_call(
        _conv_kernel,
        grid=(b, s // tm),
        in_specs=[
            pl.BlockSpec((None, tm, WIDTH), lambda i, t: (i, t, 0)),
            pl.BlockSpec((None, tm, WIDTH), lambda i, t: (i, t, 1)),
            halo(0), halo(1),
            pl.BlockSpec((CONV_WIDTH, WIDTH), lambda i, t: (0, 0)),
            pl.BlockSpec((1, WIDTH), lambda i, t: (0, 0)),
            pl.BlockSpec((1, WIDTH), lambda i, t: (0, 0)),
            pl.BlockSpec((1, WIDTH), lambda i, t: (0, 0)),
        ],
        out_specs=pl.BlockSpec((None, tm, WIDTH), lambda i, t: (i, t, 0)),
        out_shape=jax.ShapeDtypeStruct((b, s, WIDTH), BF16),
        scratch_shapes=[pltpu.VMEM((SUBLANES, CONV_HALO + tm, WIDTH), F32),
                        pltpu.VMEM((CONV_WIDTH, SUBLANES, WIDTH), F32),
                        pltpu.VMEM((tm, WIDTH), F32)],
        compiler_params=_params(("arbitrary", "arbitrary")),
        name="conv_module",
    )(proj, proj, proj, proj, w_dw, row(b_dw), row(ln_g), row(ln_b))


def _stick_kernel(q_ref, k_ref, v_ref, mm_ref, o_ref, q2_ref, carry_ref, acc_ref):
    i = pl.program_id(1)
    npair = WIDTH // PAIR
    lane = lax.broadcasted_iota(jnp.int32, (BLK, PAIR), 1)
    lo_lanes = lane < HEAD_DIM
    row2 = lax.broadcasted_iota(jnp.int32, (2 * BLK, BLK), 0)
    col2 = lax.broadcasted_iota(jnp.int32, (2 * BLK, BLK), 1)
    causal2 = col2 < (row2 & (BLK - 1))
    scale = HEAD_DIM ** -0.5
    for p in range(npair):
        qp = q_ref[:, p * PAIR:(p + 1) * PAIR] * scale
        zero = jnp.zeros_like(qp)
        q2_ref[p] = jnp.concatenate(
            [jnp.where(lo_lanes, qp, zero), jnp.where(lo_lanes, zero, qp)], axis=0)
    carry_ref[...] = jnp.zeros_like(carry_ref)
    acc_ref[...] = jnp.zeros_like(acc_ref)

    def blocks(js, diag_first):
        cols = lambda p: slice(p * PAIR, (p + 1) * PAIR)
        rows = lambda j: pl.ds(pl.multiple_of(j * BLK, BLK), BLK)

        def scores(j, p):
            return _bdot_nt(q2_ref[p], k_ref[rows(j), cols(p)])

        def keep_sums(z, diag):
            soft = jnp.log(1.0 + jnp.exp2(jnp.abs(z) * -LOG2E))
            log_beta = jnp.minimum(z, 0.0) - soft
            log_keep = log_beta - z
            if diag:
                log_keep = jnp.where(causal2, log_keep, 0.0)
            return log_beta, _bdot(log_keep.astype(BF16), mm_ref[...])

        def weighted_values(j, p, diag, log_beta, sums):
            carry = carry_ref[p]
            att = jnp.exp(log_beta + (sums[:, :BLK] + carry))
            if diag:
                att = jnp.where(causal2, att, 0.0)
            carry_ref[p] = carry + sums[:, BLK:]
            att = att.astype(BF16)
            vb = v_ref[rows(j), cols(p)]
            zero = jnp.zeros_like(vb)
            v2 = jnp.concatenate([jnp.where(lo_lanes, vb, zero), jnp.where(lo_lanes, zero, vb)], axis=0)
            return _bdot(jnp.concatenate([att[:BLK], att[BLK:]], axis=1), v2)

        chains = [(j, p, diag_first and n == 0) for n, j in enumerate(js) for p in range(npair)]
        z, ks, pv = {}, {}, {}
        for step in range(len(chains) + 3):
            if step < len(chains):
                z[step] = scores(*chains[step][:2])
            if 0 <= step - 1 < len(chains):
                ks[step - 1] = keep_sums(z.pop(step - 1), chains[step - 1][2])
            if 0 <= step - 2 < len(chains):
                pv[step - 2] = weighted_values(*chains[step - 2], *ks.pop(step - 2))
            if 0 <= step - 3 < len(chains):
                acc_ref[chains[step - 3][1]] += pv.pop(step - 3)

    @pl.when(i >= 2)
    def _():
        blocks([i, i - 1, i - 2], True)

    @pl.when(i == 1)
    def _():
        blocks([i, i - 1], True)

    @pl.when(i == 0)
    def _():
        blocks([i], True)

    rest = i - jnp.minimum(i, 2)

    def live():
        c = carry_ref[...]
        worst = jnp.max(jnp.max(c, axis=0), axis=0, keepdims=True)
        return (worst[0, 0] >= EXP_IS_ZERO_BELOW).astype(jnp.int32)

    def two_blocks(state):
        t, _ = state
        j = rest - 1 - 2 * t
        blocks([j, j - 1], False)
        return t + 1, live()

    _, alive = lax.while_loop(lambda st: (st[0] < rest // 2) & (st[1] == 1), two_blocks,
                              (jnp.int32(0), live()))

    @pl.when((rest % 2 == 1) & (alive == 1))
    def _():
        blocks([0], False)

    for p in range(npair):
        o_ref[:, p * PAIR:(p + 1) * PAIR] = acc_ref[p].astype(o_ref.dtype)


def _suffix_sum_matrix():
    sp = np.arange(BLK)[:, None]
    sc = np.arange(BLK)[None, :]
    return jnp.asarray(
        np.concatenate([(sp > sc).astype(np.float32), np.ones((BLK, BLK), np.float32)], axis=1), BF16)


def _stick_breaking(proj):
    b, s, n = proj.shape
    npair = WIDTH // PAIR
    return pl.pallas_call(
        _stick_kernel,
        grid=(b, s // BLK),
        in_specs=[
            pl.BlockSpec((None, BLK, WIDTH), lambda i, t: (i, t, 2)),
            pl.BlockSpec((None, s, WIDTH), lambda i, t: (i, 0, 3)),
            pl.BlockSpec((None, s, WIDTH), lambda i, t: (i, 0, 4)),
            pl.BlockSpec((BLK, 2 * BLK), lambda i, t: (0, 0)),
        ],
        out_specs=pl.BlockSpec((None, BLK, WIDTH), lambda i, t: (i, t, 0)),
        out_shape=jax.ShapeDtypeStruct((b, s, WIDTH), BF16),
        scratch_shapes=[pltpu.VMEM((npair, 2 * BLK, PAIR), BF16),
                        pltpu.VMEM((npair, 2 * BLK, BLK), F32),
                        pltpu.VMEM((npair, BLK, PAIR), F32)],
        compiler_params=_params(("arbitrary", "arbitrary")),
        name="stick_breaking",
    )(proj, proj, proj, _suffix_sum_matrix())


def _lane_min_index(mask, lane):
    return jnp.min(jnp.where(mask, lane, float(LANES)), -1, keepdims=True)


def _out_kernel(l_ref, r_ref, w_ref, x_ref, mod_ref, g_ref, rh_ref, rl_ref, rb_ref, tri_ref,
                xo_ref, h_ref, cls_ref, rank_ref, cnt_ref, seen):
    tm = x_ref.shape[0]

    @pl.when((pl.program_id(0) == 0) & (pl.program_id(1) == 0))
    def _():
        seen[...] = jnp.zeros_like(seen)

    m = mod_ref[...]
    mix = _bdot(l_ref[...], w_ref[0:WIDTH, :]) + _bdot(r_ref[...], w_ref[WIDTH:, :])
    x = x_ref[...] + m[2:3] * mix
    xo_ref[...] = x
    h = _rms_mod(x, g_ref[...], m[4:5], m[3:4])
    h_ref[...] = h
    logits = _dot3(h, rh_ref[...], rl_ref[...]) + rb_ref[...]
    lane = lax.broadcasted_iota(jnp.int32, logits.shape, 1).astype(F32)
    is_group = (lane >= N_EXPERTS) & (lane < N_EXPERTS + N_GROUPS)
    glog = jnp.where(is_group, logits, -jnp.inf)
    gmax = jnp.max(glog, -1, keepdims=True)
    g_idx = _lane_min_index(glog == gmax, lane) - N_EXPERTS
    in_group = (lane >= g_idx * EXPERTS_PER_GROUP) & (lane < (g_idx + 1) * EXPERTS_PER_GROUP)
    ev = jnp.where(in_group, logits, -jnp.inf)
    v1 = jnp.max(ev, -1, keepdims=True)
    i1 = _lane_min_index(ev == v1, lane)
    ev2 = jnp.where(lane == i1, -jnp.inf, ev)
    v2 = jnp.max(ev2, -1, keepdims=True)
    i2 = _lane_min_index(ev2 == v2, lane)
    lo = jnp.minimum(i1, i2) - g_idx * EXPERTS_PER_GROUP
    hi = jnp.maximum(i1, i2) - g_idx * EXPERTS_PER_GROUP
    cls = g_idx * PAIRS_PER_GROUP + (lo * (7.0 - lo) * 0.5 + (hi - lo - 1.0))
    cls_t = jnp.transpose(jnp.broadcast_to(cls, (tm, LANES)))
    onehot_t = lax.broadcasted_iota(jnp.int32, (LANES, tm), 0).astype(F32) == cls_t
    ones_t = jnp.where(onehot_t, 1.0, 0.0)
    earlier = _bdot(ones_t.astype(BF16), tri_ref[...]) + seen[...]
    cls_ref[...] = cls_t[0:1, :].astype(jnp.int32)
    rank_ref[...] = jnp.sum(jnp.where(onehot_t, earlier, 0.0), 0, keepdims=True).astype(jnp.int32)
    seen[...] += jnp.sum(ones_t, 1, keepdims=True)
    cnt_ref[...] = seen[...]


def _out_proj(left, right, w_out, x, mod, g, r_hi, r_lo, r_b):
    b, s, d = x.shape
    tm = TOKEN_TILE
    nt = s // tm
    tile = lambda w: pl.BlockSpec((None, tm, w), lambda i, t: (i, t, 0))
    const = lambda shp: pl.BlockSpec(shp, lambda i, t: tuple(0 for _ in shp))
    row = pl.BlockSpec((None, 1, tm), lambda i, t: (i * nt + t, 0, 0))
    tri = np.arange(tm)[:, None] < np.arange(tm)[None, :]
    return pl.pallas_call(
        _out_kernel,
        grid=(b, nt),
        in_specs=[tile(WIDTH), tile(WIDTH), const((2 * WIDTH, d)), tile(d),
                  pl.BlockSpec((None, 6, d), lambda i, t: (i, 0, 0)),
                  const((1, d)), const((d, LANES)), const((d, LANES)), const((1, LANES)),
                  const((tm, tm))],
        out_specs=[tile(d), tile(d), row, row, const((LANES, 1))],
        out_shape=[jax.ShapeDtypeStruct((b, s, d), F32),
                   jax.ShapeDtypeStruct((b, s, d), F32),
                   jax.ShapeDtypeStruct((b * nt, 1, tm), jnp.int32),
                   jax.ShapeDtypeStruct((b * nt, 1, tm), jnp.int32),
                   jax.ShapeDtypeStruct((LANES, 1), F32)],
        scratch_shapes=[pltpu.VMEM((LANES, 1), F32)],
        compiler_params=_params(("arbitrary", "arbitrary")),
        name="out_proj_router",
    )(left, right, w_out, x, mod, g.reshape(1, d), r_hi, r_lo, r_b, jnp.asarray(tri, BF16))


def _router_weights(w_group, b_group, w_router, b_router):
    d = w_group.shape[0]
    w = jnp.concatenate([jnp.transpose(w_router, (1, 0, 2)).reshape(d, N_EXPERTS), w_group], axis=1)
    w = jnp.pad(w.astype(F32), ((0, 0), (0, LANES - w.shape[1])))
    bias = jnp.concatenate([b_router.reshape(N_EXPERTS), b_group]).astype(F32)
    bias = jnp.pad(bias, (0, LANES - bias.shape[0])).reshape(1, LANES)
    hi, lo = _split(w)
    return hi, lo, bias


def _to_token_tiles(x):
    n = x.shape[0]
    parts = [x[:, g * LANES:(g + 1) * LANES].reshape(n // 8, 8, LANES) for g in range(8)]
    return jnp.swapaxes(jnp.stack(parts, axis=1), 1, 2).reshape(n, 8, LANES)


def _from_token_tiles(v):
    n = v.shape[0]
    w = jnp.swapaxes(v.reshape(n // 8, 8, 8, LANES), 1, 2)
    return jnp.concatenate([w[:, g].reshape(n, LANES) for g in range(8)], axis=-1)


def _row_dma_loop(n, row_copy):
    def body(i, carry):
        row_copy(2 * i).start(priority=0)
        row_copy(2 * i + 1).start(priority=1)
        return carry
    lax.fori_loop(0, n // 2, body, 0, unroll=4)


def _dispatch_kernel(pad_ref, dest_ref, h_ref, hs_ref, buf, sem):
    tm = h_ref.shape[0]

    @pl.when(pl.program_id(0) == 0)
    def _():
        buf[...] = jnp.zeros_like(buf)
        for c in range(pad_ref.shape[0]):
            fill = pltpu.make_async_copy(buf, hs_ref.at[pl.ds(pad_ref[c], tm)], sem)
            fill.start()
            fill.wait()

    buf[...] = _to_token_tiles(h_ref[...])
    _row_dma_loop(tm, lambda r: pltpu.make_async_copy(
        buf.at[r], hs_ref.at[dest_ref[0, r]], sem))
    pltpu.make_async_copy(buf, hs_ref.at[pl.ds(0, tm)], sem).wait()


def _dispatch(h, dest, pad_start, n_slots):
    b, s, d = h.shape
    tm = TOKEN_TILE
    assert tm == SLOT_TILE
    nt = s // tm
    return pl.pallas_call(
        _dispatch_kernel,
        grid_spec=pltpu.PrefetchScalarGridSpec(
            num_scalar_prefetch=1,
            grid=(b * nt,),
            in_specs=[pl.BlockSpec((None, 1, tm), lambda i, pad: (i, 0, 0), memory_space=pltpu.SMEM),
                      pl.BlockSpec((None, tm, d), lambda i, pad: (i // nt, i % nt, 0))],
            out_specs=pl.BlockSpec(memory_space=pl.ANY),
            scratch_shapes=[pltpu.VMEM((tm, 8, LANES), F32), pltpu.SemaphoreType.DMA(())]),
        out_shape=jax.ShapeDtypeStruct((n_slots, 8, LANES), F32),
        compiler_params=_params(("arbitrary",)),
        name="moe_dispatch",
    )(pad_start, dest, h)


def _moe_kernel(blk_ref, ea_ref, eb_ref, grp_ref, used_ref, hs_ref, wga_ref, wua_ref, wda_ref,
                wgb_ref, wub_ref, wdb_ref, rh_ref, rb_ref, o_ref):
    n = pl.program_id(0)

    @pl.when(used_ref[n] == 1)
    def _():
        h = _from_token_tiles(hs_ref[...]).astype(BF16)
        logits = _bdot(h, rh_ref[...]) + rb_ref[...]
        lane = lax.broadcasted_iota(jnp.int32, logits.shape, 1)
        pick = lambda idx: jnp.sum(jnp.where(lane == idx, logits, 0.0), -1, keepdims=True)
        is_group = (lane >= N_EXPERTS) & (lane < N_EXPERTS + N_GROUPS)
        g_w = 1.0 / jnp.sum(
            jnp.where(is_group, jnp.exp(logits - pick(N_EXPERTS + grp_ref[n])), 0.0),
            -1, keepdims=True)
        la = pick(ea_ref[n])
        lb = pick(eb_ref[n])

        def expert(wg_ref, wu_ref, wd_ref):
            a = _bdot(h, wg_ref[...])
            hid = (a * jax.nn.sigmoid(a)) * _bdot(h, wu_ref[...])
            return _bdot(hid.astype(BF16), wd_ref[...])

        y = (g_w * (1.0 / (1.0 + jnp.exp(lb - la)))) * expert(wga_ref, wua_ref, wda_ref)
        y = y + (g_w * (1.0 / (1.0 + jnp.exp(la - lb)))) * expert(wgb_ref, wub_ref, wdb_ref)
        o_ref[...] = _to_token_tiles(y)

    @pl.when(used_ref[n] == 0)
    def _():
        o_ref[...] = jnp.zeros_like(o_ref)


def _moe(hs, plan, w_gate, w_up, w_down, r_hi, r_b):
    n_slots = hs.shape[0]
    ne, d, de = w_gate.shape
    tm = SLOT_TILE
    slot = pl.BlockSpec((tm, 8, LANES), lambda n, *_: (n, 0, 0))
    slot_in = pl.BlockSpec((tm, 8, LANES), lambda n, blk, *_: (blk[n], 0, 0))
    first = lambda shp: pl.BlockSpec((None,) + shp, lambda n, blk, ea, eb, grp, used: (ea[n], 0, 0))
    second = lambda shp: pl.BlockSpec((None,) + shp, lambda n, blk, ea, eb, grp, used: (eb[n], 0, 0))
    const = lambda shp: pl.BlockSpec(shp, lambda n, *_: tuple(0 for _ in shp))
    return pl.pallas_call(
        _moe_kernel,
        grid_spec=pltpu.PrefetchScalarGridSpec(
            num_scalar_prefetch=5,
            grid=(n_slots // tm,),
            in_specs=[slot_in, first((d, de)), first((d, de)), first((de, d)),
                      second((d, de)), second((d, de)), second((de, d)),
                      const((d, LANES)), const((1, LANES))],
            out_specs=slot),
        out_shape=jax.ShapeDtypeStruct((n_slots, 8, LANES), F32),
        compiler_params=_params(("arbitrary",)),
        name="moe_experts",
    )(*plan, hs, w_gate, w_up, w_down, w_gate, w_up, w_down, r_hi, r_b)


def _combine_kernel(dest_ref, ys_ref, x_ref, mod_ref, fg_ref, o_ref, buf, sem):
    tm = x_ref.shape[0]
    _row_dma_loop(tm, lambda r: pltpu.make_async_copy(
        ys_ref.at[dest_ref[0, r]], buf.at[r], sem))
    pltpu.make_async_copy(ys_ref.at[pl.ds(0, tm)], buf, sem).wait()
    x = x_ref[...] + mod_ref[5:6, :] * _from_token_tiles(buf[...])
    o_ref[...] = x * lax.rsqrt(jnp.mean(x * x, -1, keepdims=True) + EPS) * fg_ref[...]


def _combine(ys, dest, x, mod, final_g):
    b, s, d = x.shape
    tm = TOKEN_TILE
    nt = s // tm
    return pl.pallas_call(
        _combine_kernel,
        grid=(b * nt,),
        in_specs=[pl.BlockSpec((None, 1, tm), lambda i: (i, 0, 0), memory_space=pltpu.SMEM),
                  pl.BlockSpec(memory_space=pl.ANY),
                  pl.BlockSpec((None, tm, d), lambda i: (i // nt, i % nt, 0)),
                  pl.BlockSpec((None, 6, d), lambda i: (i // nt, 0, 0)),
                  pl.BlockSpec((1, d), lambda i: (0, 0))],
        out_specs=pl.BlockSpec((None, tm, d), lambda i: (i // nt, i % nt, 0)),
        out_shape=jax.ShapeDtypeStruct((b, s, d), F32),
        scratch_shapes=[pltpu.VMEM((tm, 8, LANES), F32), pltpu.SemaphoreType.DMA(())],
        compiler_params=_params(("arbitrary",)),
        name="moe_combine",
    )(dest, ys, x, mod, final_g.reshape(1, d))


def _combine_in_kernel(dest_ref, next_ref, ys_ref, x_ref, mod_ref, g_ref, nmod_ref, w_ref,
                       xo_ref, o_ref, *rest, dilations):
    *rest, buf, sem = rest
    tm = x_ref.shape[0]
    i = pl.program_id(0)
    slot = i % 2

    def gather(idx_ref, to):
        _row_dma_loop(tm, lambda r: pltpu.make_async_copy(
            ys_ref.at[idx_ref[0, r]], buf.at[to].at[r], sem.at[to]))

    @pl.when(i == 0)
    def _():
        gather(dest_ref, 0)

    @pl.when(i + 1 < pl.num_programs(0))
    def _():
        gather(next_ref, 1 - slot)

    pltpu.make_async_copy(ys_ref.at[pl.ds(0, tm)], buf.at[slot], sem.at[slot]).wait()
    x = x_ref[...] + mod_ref[5:6, :] * _from_token_tiles(buf[slot])
    xo_ref[...] = x
    _in_body(x, g_ref, nmod_ref, w_ref, o_ref, rest, dilations)


def _combine_in_proj(ys, dest, x, mod, g_next, mod_next, w_in, dilations=()):
    b, s, d = x.shape
    n = w_in.shape[1]
    tm = TOKEN_TILE
    nt = s // tm
    last = b * nt - 1
    tile = lambda i: (i // nt, i % nt, 0)
    specs, shapes, scratch = _in_proj_outputs(b, s, n, tm, dilations, tile)
    outs = pl.pallas_call(
        functools.partial(_combine_in_kernel, dilations=tuple(dilations)),
        grid=(b * nt,),
        in_specs=[pl.BlockSpec((None, 1, tm), lambda i: (i, 0, 0), memory_space=pltpu.SMEM),
                  pl.BlockSpec((None, 1, tm), lambda i: (jnp.minimum(i + 1, last), 0, 0),
                               memory_space=pltpu.SMEM),
                  pl.BlockSpec(memory_space=pl.ANY),
                  pl.BlockSpec((None, tm, d), tile),
                  pl.BlockSpec((None, 6, d), lambda i: (i // nt, 0, 0)),
                  pl.BlockSpec((1, d), lambda i: (0, 0)),
                  pl.BlockSpec((None, 6, d), lambda i: (i // nt, 0, 0)),
                  pl.BlockSpec((d, n), lambda i: (0, 0))],
        out_specs=[pl.BlockSpec((None, tm, d), tile)] + specs,
        out_shape=[jax.ShapeDtypeStruct((b, s, d), F32)] + shapes,
        scratch_shapes=scratch + [pltpu.VMEM((2, tm, 8, LANES), F32), pltpu.SemaphoreType.DMA((2,))],
        compiler_params=_params(("arbitrary",)),
        name="moe_combine_in_proj",
    )(dest, dest, ys, x, mod, g_next.reshape(1, d), mod_next, w_in)
    return outs


def _routing_plan(cls, rank, counts, n_tokens):
    tm = SLOT_TILE
    n_tiles = n_tokens // tm + N_CLASSES
    counts = counts[:N_CLASSES, 0].astype(jnp.int32)
    tiles = (counts + tm - 1) // tm
    tile_end = jnp.cumsum(tiles)
    classes = jnp.arange(N_CLASSES, dtype=jnp.int32)
    first_slot = (tile_end - tiles) * tm
    dest = jnp.sum(jnp.where(cls[..., None] == classes, first_slot, 0), -1) + rank
    ids = jnp.arange(n_tiles, dtype=jnp.int32)
    used = ids < tile_end[-1]
    tile_cls = jnp.sum(
        (jnp.minimum(ids, tile_end[-1] - 1)[:, None] >= tile_end).astype(jnp.int32), -1)
    grp = tile_cls // PAIRS_PER_GROUP
    pair = tile_cls % PAIRS_PER_GROUP
    pair_a = jnp.asarray([0, 0, 0, 1, 1, 2], jnp.int32)
    pair_b = jnp.asarray([1, 2, 3, 2, 3, 3], jnp.int32)
    eb = grp * EXPERTS_PER_GROUP + pair_b[pair]
    ea = jnp.where(used, grp * EXPERTS_PER_GROUP + pair_a[pair], eb)
    blk = jnp.where(used, ids, 0)
    tail = jnp.minimum(tile_end[-1] * tm + classes * tm, (n_tiles - 1) * tm)
    zero_blocks = jnp.concatenate([first_slot + counts, tail])
    return dest, zero_blocks, (blk, ea, eb, grp, used.astype(jnp.int32))


@jax.jit
def kernel(x, c, rel_bias, norm1_g, norm2_g, w_ada, b_ada, ev_w_in, ev_w_out, ev_gmlp_ln_g, ev_gmlp_ln_b, ev_w_s, ev_b_s, od_w_in, od_w_out, od_w_dw, od_b_dw, od_conv_ln_g, od_conv_ln_b, moe_w_group, moe_b_group, moe_w_router, moe_b_router, moe_w_gate, moe_w_up, moe_w_down, final_norm_g):
    b, s, d = x.shape
    depth = w_ada.shape[0]
    mods = _ada(c, w_ada, b_ada).reshape(depth, b, 6, d)
    biases = [_branch_bias(rel_bias, dil) for _, dil in DILATED_BRANCHES]
    dils = [dil for _, dil in DILATED_BRANCHES]
    grouped_dils = [dil for dil in dils if dil > 1]

    def in_weights(layer):
        even = layer % 2 == 0
        return (ev_w_in if even else od_w_in)[layer // 2].astype(BF16), grouped_dils if even else []

    w_in, extra = in_weights(0)
    proj, *grouped = _in_proj(x, norm1_g[0], mods[0], w_in, extra)
    for layer in range(depth):
        j = layer // 2
        mod = mods[layer]
        if layer % 2 == 0:
            views = {1: (proj, proj.shape[-1] // WIDTH)}
            views.update({dil: (g, 3) for dil, g in zip(grouped_dils, grouped)})
            outs = [_dilated_branch(views[dil][0], bias, dil, views[dil][1])
                    for bias, dil in zip(biases, dils)]
            left = _combine_branches(outs, dils, s)
            right = _gmlp(proj, ev_gmlp_ln_g[j], ev_gmlp_ln_b[j], ev_w_s[j], ev_b_s[j])
            w_out = ev_w_out[j]
        else:
            left = _conv_module(proj, od_w_dw[j], od_b_dw[j], od_conv_ln_g[j], od_conv_ln_b[j])
            right = _stick_breaking(proj)
            w_out = od_w_out[j]
        r_hi, r_lo, r_b = _router_weights(moe_w_group[layer], moe_b_group[layer],
                                          moe_w_router[layer], moe_b_router[layer])
        x, h, cls, rank, counts = _out_proj(left, right, w_out.astype(BF16), x, mod,
                                            norm2_g[layer], r_hi, r_lo, r_b)
        dest, pad_start, plan = _routing_plan(cls, rank, counts, b * s)
        hs = _dispatch(h, dest, pad_start, (b * s // SLOT_TILE + N_CLASSES) * SLOT_TILE)
        ys = _moe(hs, plan, moe_w_gate[layer].astype(BF16), moe_w_up[layer].astype(BF16),
                  moe_w_down[layer].astype(BF16), r_hi, r_b)
        if layer == depth - 1:
            x = _combine(ys, dest, x, mod, final_norm_g)
        else:
            w_in, extra = in_weights(layer + 1)
            x, proj, *grouped = _combine_in_proj(ys, dest, x, mod, norm1_g[layer + 1],
                                                 mods[layer + 1], w_in, extra)
    return x
```

```python
import functools
import math

import numpy as np
import jax
import jax.numpy as jnp
from jax import lax
from jax.experimental import pallas as pl
from jax.experimental.pallas import tpu as pltpu

F32 = jnp.float32
BF16 = jnp.bfloat16

HEAD_DIM = 64
HEADS = 8
WIDTH = HEADS * HEAD_DIM
PAIR = 2 * HEAD_DIM
BLK = 128
DILATED_BRANCHES = ((128, 1), (512, 4), (2048, 16))
N_BUCKETS = 32
MAX_DISTANCE = 2048
CONV_WIDTH = 31
CONV_HALO = 32
CONV_TAP_UNROLL = 4
STICK_STAGE_LAG = 2
DILATED_STAGE_LAG = 1
DILATED_QUERY_ROWS = 128
B_GROUPS = 4
N_GROUPS = 4
EXPERTS_PER_GROUP = 4
N_EXPERTS = N_GROUPS * EXPERTS_PER_GROUP
PAIRS_PER_GROUP = EXPERTS_PER_GROUP * (EXPERTS_PER_GROUP - 1) // 2
N_CLASSES = N_GROUPS * PAIRS_PER_GROUP
TOKEN_TILE = 512
SLOT_TILE = 512
EPS = 1e-6
NEG = -1e30
LOG2E = 1.4426950408889634
EXP_IS_ZERO_BELOW = -104.0
LANES = 128
SUBLANES = 8
VMEM_LIMIT = 48 * 1024 * 1024


def _params(sem):
    return pltpu.CompilerParams(dimension_semantics=sem, vmem_limit_bytes=VMEM_LIMIT)


def _bdot(a, b):
    return jnp.dot(a, b, preferred_element_type=F32)


def _bdot_nt(a, b):
    return lax.dot_general(a, b, (((1,), (1,)), ((), ())), preferred_element_type=F32)


def _split(a):
    hi = a.astype(BF16)
    lo = (a - hi.astype(F32)).astype(BF16)
    return hi, lo


def _dot3(a, w_hi, w_lo):
    a_hi, a_lo = _split(a)
    return _bdot(a_hi, w_hi) + (_bdot(a_hi, w_lo) + _bdot(a_lo, w_hi))


def _ada_kernel(c_ref, w_ref, b_ref, o_ref):
    c = c_ref[...]
    sc = c * jax.nn.sigmoid(c)
    w_hi, w_lo = _split(w_ref[...])
    o_ref[...] = _dot3(sc, w_hi, w_lo) + b_ref[...]


def _ada(c, w_ada, b_ada):
    depth, d, n = w_ada.shape
    b = c.shape[0]
    tn = 1024
    return pl.pallas_call(
        _ada_kernel,
        grid=(depth, n // tn),
        in_specs=[
            pl.BlockSpec((b, d), lambda l, j: (0, 0)),
            pl.BlockSpec((None, d, tn), lambda l, j: (l, 0, j)),
            pl.BlockSpec((None, 1, tn), lambda l, j: (l, 0, j)),
        ],
        out_specs=pl.BlockSpec((None, b, tn), lambda l, j: (l, 0, j)),
        out_shape=jax.ShapeDtypeStruct((depth, b, n), F32),
        compiler_params=_params(("arbitrary", "arbitrary")),
        name="ada_mod",
    )(c, w_ada, b_ada.reshape(depth, 1, n))


def _rms_mod(x, g, scale_row, shift_row):
    y = x * lax.rsqrt(jnp.mean(x * x, -1, keepdims=True) + EPS) * g
    return y * (1.0 + scale_row) + shift_row


def _in_body(x, g_ref, mod_ref, w_ref, o_ref, rest, dilations):
    m = mod_ref[...]
    h = _rms_mod(x, g_ref[...], m[1:2], m[0:1]).astype(BF16)
    tm = x.shape[0]
    n = w_ref.shape[1]
    per = WIDTH // LANES
    for j in range(n // WIDTH):
        cols = slice(j * WIDTH, (j + 1) * WIDTH)
        y = _bdot(h, w_ref[:, cols])
        o_ref[:, cols] = y.astype(o_ref.dtype)
        if dilations and j < 3:
            for c in range(per):
                rest[-1][j * per + c] = y[:, c * LANES:(c + 1) * LANES]
    for d, out in zip(dilations, rest):
        for r in range(d):
            for c in range(3 * per):
                lo = (r * 3 * per + c) * LANES
                out[:, lo:lo + LANES] = rest[-1][c, pl.ds(r, tm // d, stride=d), :].astype(out.dtype)


def _in_kernel(x_ref, g_ref, mod_ref, w_ref, o_ref, *rest, dilations):
    _in_body(x_ref[...], g_ref, mod_ref, w_ref, o_ref, rest, dilations)


def _in_proj_outputs(b, s, n, tm, dilations, index):
    qkv = 3 * WIDTH
    specs = [pl.BlockSpec((None, tm, n), index)] + [
        pl.BlockSpec((None, tm // dil, dil * qkv), index) for dil in dilations]
    shapes = [jax.ShapeDtypeStruct((b, s, n), BF16)] + [
        jax.ShapeDtypeStruct((b, s // dil, dil * qkv), BF16) for dil in dilations]
    scratch = [pltpu.VMEM((qkv // LANES, tm, LANES), F32)] if dilations else []
    return specs, shapes, scratch


def _in_proj(x, g, mod, w_in, dilations=()):
    b, s, d = x.shape
    n = w_in.shape[1]
    tm = TOKEN_TILE
    specs, shapes, scratch = _in_proj_outputs(b, s, n, tm, dilations, lambda i, t: (i, t, 0))
    outs = pl.pallas_call(
        functools.partial(_in_kernel, dilations=tuple(dilations)),
        grid=(b, s // tm),
        in_specs=[
            pl.BlockSpec((None, tm, d), lambda i, t: (i, t, 0)),
            pl.BlockSpec((1, d), lambda i, t: (0, 0)),
            pl.BlockSpec((None, 6, d), lambda i, t: (i, 0, 0)),
            pl.BlockSpec((d, n), lambda i, t: (0, 0)),
        ],
        out_specs=specs,
        out_shape=shapes,
        scratch_shapes=scratch,
        compiler_params=_params(("arbitrary", "arbitrary")),
        name="in_proj",
    )(x, g.reshape(1, d), mod, w_in)
    return outs if dilations else outs[0]


def _t5_bucket(dist):
    max_exact = N_BUCKETS // 2
    d = np.maximum(dist, 1).astype(np.float32)
    large = max_exact + (np.log(d / max_exact) / np.log(MAX_DISTANCE / max_exact)
                         * (N_BUCKETS - max_exact)).astype(np.int32)
    large = np.minimum(large, N_BUCKETS - 1)
    return np.where(dist < max_exact, dist, large).astype(np.int32)


def _branch_bias(rel_table, dilation):
    qi = np.arange(BLK)[:, None]
    kj = np.arange(2 * BLK)[None, :]
    bucket = _t5_bucket(np.clip(qi + BLK - kj, 0, None) * dilation)
    onehot = np.eye(N_BUCKETS, dtype=np.float32)[bucket]
    bias = jnp.einsum("qkb,bh->hqk", onehot, rel_table.astype(F32), precision=lax.Precision.HIGHEST)
    dist = qi + BLK - kj
    in_window = (dist >= 0) & (dist <= BLK)
    return jnp.where(in_window[None], bias, NEG)


def _dil_kernel(q_ref, kp_ref, kc_ref, vp_ref, vc_ref, bias_ref, o_ref, m_ref, l_ref,
                kbuf, vbuf, *, tq):
    n = pl.program_id(2)
    nsub = tq // BLK
    kbuf[0:BLK] = kp_ref[...]
    kbuf[BLK:] = kc_ref[...]
    vbuf[0:BLK] = vp_ref[...]
    vbuf[BLK:] = vc_ref[...]
    qb = DILATED_QUERY_ROWS
    kj = lax.broadcasted_iota(jnp.int32, (qb, 2 * BLK), 1)
    lane = lax.broadcasted_iota(jnp.int32, (qb, PAIR), 1)
    scale = HEAD_DIM ** -0.5
    has_prev = kj >= jnp.where(n == 0, BLK, 0)
    rows = lambda u: slice(u * qb, (u + 1) * qb)
    krows = lambda u: slice(u * qb // BLK * BLK, (u * qb // BLK + 2) * BLK)
    brows = lambda u: slice(u * qb % BLK, u * qb % BLK + qb)
    cols = lambda p: slice(p * PAIR, (p + 1) * PAIR)

    def logits(i, p, hh):
        q_pair = q_ref[rows(i), cols(p)] * scale
        head_lanes = (lane >= HEAD_DIM) if hh else (lane < HEAD_DIM)
        qm = jnp.where(head_lanes, q_pair, jnp.zeros_like(q_pair))
        logit = _bdot_nt(qm, kbuf[krows(i), cols(p)]) + bias_ref[2 * p + hh, brows(i), :]
        return jnp.where(has_prev, logit, NEG) if i * qb < BLK else logit

    def softmax_values(i, p, logit):
        mx = jnp.max(logit, -1, keepdims=True)
        pr = jnp.exp(logit - mx)
        den = jnp.sum(pr, -1, keepdims=True)
        return mx, den, _bdot(pr.astype(BF16), vbuf[krows(i), cols(p)])

    chains = [(i, p, hh) for i in range(tq // qb) for p in range(HEADS // 2) for hh in range(2)]
    lag = DILATED_STAGE_LAG
    lg, sv, first_head = {}, {}, {}
    m_tile, l_tile = {}, {}
    for step in range(len(chains) + 2 * lag):
        if step < len(chains):
            lg[step] = logits(*chains[step])
        if 0 <= step - lag < len(chains):
            i, p, _ = chains[step - lag]
            sv[step - lag] = softmax_values(i, p, lg.pop(step - lag))
        if 0 <= step - 2 * lag < len(chains):
            i, p, hh = chains[step - 2 * lag]
            mx, den, pv = sv.pop(step - 2 * lag)
            h = 2 * p + hh
            m_tile[i] = jnp.where(lane == h, mx, m_tile.get(i, jnp.zeros((qb, LANES), F32)))
            l_tile[i] = jnp.where(lane == h, den, l_tile.get(i, jnp.ones((qb, LANES), F32)))
            if hh == 0:
                first_head[i, p] = pv / den
            else:
                o_ref[rows(i), cols(p)] = jnp.where(
                    lane < HEAD_DIM, first_head.pop((i, p)), pv / den).astype(o_ref.dtype)
            if h == HEADS - 1:
                m_ref[rows(i), :] = m_tile.pop(i)
                l_ref[rows(i), :] = l_tile.pop(i)


def _dilated_branch(view, bias, dilation, ncol):
    b, sub, _ = view.shape
    tq = min(512, sub)
    nsub = tq // BLK
    cur = lambda c: pl.BlockSpec((None, tq, WIDTH), lambda i, r, t: (i, t, r * ncol + c))
    prev = lambda c: pl.BlockSpec(
        (None, BLK, WIDTH), lambda i, r, t: (i, jnp.maximum(t * nsub - 1, 0), r * ncol + c))
    return pl.pallas_call(
        functools.partial(_dil_kernel, tq=tq),
        grid=(b, dilation, sub // tq),
        in_specs=[cur(0), prev(1), cur(1), prev(2), cur(2),
                  pl.BlockSpec((HEADS, BLK, 2 * BLK), lambda i, r, t: (0, 0, 0))],
        out_specs=[
            pl.BlockSpec((None, tq, WIDTH), lambda i, r, t: (i, t, r)),
            pl.BlockSpec((None, tq, LANES), lambda i, r, t: (i, t, r)),
            pl.BlockSpec((None, tq, LANES), lambda i, r, t: (i, t, r)),
        ],
        out_shape=[
            jax.ShapeDtypeStruct((b, sub, dilation * WIDTH), BF16),
            jax.ShapeDtypeStruct((b, sub, dilation * LANES), F32),
            jax.ShapeDtypeStruct((b, sub, dilation * LANES), F32),
        ],
        scratch_shapes=[pltpu.VMEM((BLK + tq, WIDTH), BF16), pltpu.VMEM((BLK + tq, WIDTH), BF16)],
        compiler_params=_params(("arbitrary", "arbitrary", "arbitrary")),
        name=f"dilated_d{dilation}",
    )(view, view, view, view, view, bias)


def _comb_kernel(*refs, dilations):
    nb = len(dilations)
    o_refs, m_refs, l_refs = refs[:nb], refs[nb:2 * nb], refs[2 * nb:3 * nb]
    e_ref, out_ref, o_scr, s_scr = refs[3 * nb:]
    tm = out_ref.shape[0]
    per = WIDTH // LANES
    os, ms, ls = [], [], []
    for bi, d in enumerate(dilations):
        if d == 1:
            os.append([o_refs[bi][:, c * LANES:(c + 1) * LANES].astype(F32) for c in range(per)])
            ms.append(m_refs[bi][...])
            ls.append(l_refs[bi][...])
            continue
        rows = lambda r: pl.ds(r, tm // d, stride=d)
        for r in range(d):
            for c in range(per):
                lo = (r * per + c) * LANES
                o_scr[bi, c, rows(r), :] = o_refs[bi][:, lo:lo + LANES].astype(F32)
            s_scr[bi, 0, rows(r), :] = m_refs[bi][:, r * LANES:(r + 1) * LANES]
            s_scr[bi, 1, rows(r), :] = l_refs[bi][:, r * LANES:(r + 1) * LANES]
        os.append([o_scr[bi, c] for c in range(per)])
        ms.append(s_scr[bi, 0])
        ls.append(s_scr[bi, 1])
    m_all = functools.reduce(jnp.maximum, ms)
    es = [l * jnp.exp(m - m_all) for l, m in zip(ls, ms)]
    tot = functools.reduce(lambda x, y: x + y, es)
    acc = [None] * per
    for e, o in zip(es, os):
        w_hi, w_lo = _split(e / tot)
        w = _bdot(w_hi, e_ref[...]) + _bdot(w_lo, e_ref[...])
        for c in range(per):
            term = w[:, c * LANES:(c + 1) * LANES] * o[c]
            acc[c] = term if acc[c] is None else acc[c] + term
    for c in range(per):
        out_ref[:, c * LANES:(c + 1) * LANES] = acc[c].astype(out_ref.dtype)


def _combine_branches(outs, dilations, s):
    b = outs[0][0].shape[0]
    tm = TOKEN_TILE
    nb = len(dilations)
    expand = np.zeros((LANES, WIDTH), np.float32)
    for h in range(HEADS):
        expand[h, h * HEAD_DIM:(h + 1) * HEAD_DIM] = 1.0
    grouped = lambda w: [pl.BlockSpec((None, tm // d, d * w), lambda i, t: (i, t, 0)) for d in dilations]
    return pl.pallas_call(
        functools.partial(_comb_kernel, dilations=tuple(dilations)),
        grid=(b, s // tm),
        in_specs=grouped(WIDTH) + grouped(LANES) + grouped(LANES)
        + [pl.BlockSpec((LANES, WIDTH), lambda i, t: (0, 0))],
        out_specs=pl.BlockSpec((None, tm, WIDTH), lambda i, t: (i, t, 0)),
        out_shape=jax.ShapeDtypeStruct((b, s, WIDTH), BF16),
        scratch_shapes=[pltpu.VMEM((nb, WIDTH // LANES, tm, LANES), F32),
                        pltpu.VMEM((nb, 2, tm, LANES), F32)],
        compiler_params=_params(("arbitrary", "arbitrary")),
        name="dilated_combine",
    )(*[o for o, _, _ in outs], *[m for _, m, _ in outs], *[l for _, _, l in outs],
      jnp.asarray(expand, BF16))


def _gelu(x):
    return 0.5 * x * (1.0 + jnp.tanh(math.sqrt(2.0 / math.pi) * (x + 0.044715 * (x * x * x))))


def _gmlp_kernel(u_ref, v_ref, g_ref, b_ref, ws_ref, bs_ref, o_ref):
    tm = u_ref.shape[0]
    u = _gelu(u_ref[...].astype(F32))
    v = _gelu(v_ref[...].astype(F32))
    mu = jnp.mean(v, -1, keepdims=True)
    vc = v - mu
    var = jnp.mean(vc * vc, -1, keepdims=True)
    v = (vc * lax.rsqrt(var + EPS) * g_ref[...] + b_ref[...]).astype(BF16)
    r = lax.broadcasted_iota(jnp.int32, (BLK, BLK), 0)
    c = lax.broadcasted_iota(jnp.int32, (BLK, BLK), 1)
    for g in range(B_GROUPS):
        w = jnp.where(c <= r, ws_ref[g], 0.0).astype(BF16)
        cols = slice(g * BLK, (g + 1) * BLK)
        for ch in range(tm // BLK):
            rows = slice(ch * BLK, (ch + 1) * BLK)
            mixed = _bdot(w, v[rows, cols]) + bs_ref[g]
            o_ref[rows, cols] = (u[rows, cols] * mixed).astype(o_ref.dtype)


def _gmlp(proj, ln_g, ln_b, w_s, b_s):
    b, s, n = proj.shape
    tm = 512
    return pl.pallas_call(
        _gmlp_kernel,
        grid=(b, s // tm),
        in_specs=[
            pl.BlockSpec((None, tm, WIDTH), lambda i, t: (i, t, 3)),
            pl.BlockSpec((None, tm, WIDTH), lambda i, t: (i, t, 4)),
            pl.BlockSpec((1, WIDTH), lambda i, t: (0, 0)),
            pl.BlockSpec((1, WIDTH), lambda i, t: (0, 0)),
            pl.BlockSpec((B_GROUPS, BLK, BLK), lambda i, t: (0, 0, 0)),
            pl.BlockSpec((B_GROUPS, BLK, 1), lambda i, t: (0, 0, 0)),
        ],
        out_specs=pl.BlockSpec((None, tm, WIDTH), lambda i, t: (i, t, 0)),
        out_shape=jax.ShapeDtypeStruct((b, s, WIDTH), BF16),
        compiler_params=_params(("arbitrary", "arbitrary")),
        name="gmlp_gate",
    )(proj, proj, ln_g.reshape(1, WIDTH), ln_b.reshape(1, WIDTH), w_s,
      b_s.reshape(B_GROUPS, BLK, 1))


def _conv_kernel(a_ref, g_ref, ah_ref, gh_ref, w_ref, bdw_ref, lng_ref, lnb_ref, o_ref,
                 hbuf, wbuf, ybuf):
    tm = a_ref.shape[0]
    t = pl.program_id(1)
    halo = ah_ref[...].astype(F32) * jax.nn.sigmoid(gh_ref[...].astype(F32))
    hbuf[0, 0:CONV_HALO] = jnp.where(t == 0, 0.0, halo)
    hbuf[0, CONV_HALO:] = a_ref[...].astype(F32) * jax.nn.sigmoid(g_ref[...].astype(F32))
    n_shifted = tm + CONV_HALO - SUBLANES
    for s in range(1, SUBLANES):
        hbuf[s, 0:n_shifted] = hbuf[0, pl.ds(s, n_shifted), :]
    for k in range(CONV_WIDTH):
        wbuf[k] = jnp.broadcast_to(w_ref[k:k + 1, :], (SUBLANES, WIDTH))
    rc = 32
    off = CONV_HALO - (CONV_WIDTH - 1)

    def chunk(ci, carry):
        base = pl.multiple_of(ci * rc, rc)
        def tap(k, acc):
            q = off + k
            s = q & (SUBLANES - 1)
            start = pl.multiple_of(base + (q - s), SUBLANES)
            win = hbuf[s, pl.ds(start, rc), :].reshape(rc // SUBLANES, SUBLANES, WIDTH)
            return acc + (wbuf[k] * win).reshape(rc, WIDTH)

        acc = lax.fori_loop(0, CONV_WIDTH, tap, jnp.zeros((rc, WIDTH), F32) + bdw_ref[...],
                            unroll=CONV_TAP_UNROLL)
        ybuf[pl.ds(base, rc), :] = acc
        return carry

    lax.fori_loop(0, tm // rc, chunk, 0)
    acc = ybuf[...]
    mu = jnp.mean(acc, -1, keepdims=True)
    xc = acc - mu
    var = jnp.mean(xc * xc, -1, keepdims=True)
    y = xc * lax.rsqrt(var + EPS) * lng_ref[...] + lnb_ref[...]
    o_ref[...] = (y * jax.nn.sigmoid(y)).astype(o_ref.dtype)


def _conv_module(proj, w_dw, b_dw, ln_g, ln_b):
    b, s, n = proj.shape
    tm = 512
    hb = tm // CONV_HALO
    row = lambda v: v.reshape(1, WIDTH)
    halo = lambda c: pl.BlockSpec(
        (None, CONV_HALO, WIDTH), lambda i, t: (i, jnp.maximum(t * hb - 1, 0), c))
    return pl.pallas_call(
        _conv_kernel,
        grid=(b, s // tm),
        in_specs=[
            pl.BlockSpec((None, tm, WIDTH), lambda i, t: (i, t, 0)),
            pl.BlockSpec((None, tm, WIDTH), lambda i, t: (i, t, 1)),
            halo(0), halo(1),
            pl.BlockSpec((CONV_WIDTH, WIDTH), lambda i, t: (0, 0)),
            pl.BlockSpec((1, WIDTH), lambda i, t: (0, 0)),
            pl.BlockSpec((1, WIDTH), lambda i, t: (0, 0)),
            pl.BlockSpec((1, WIDTH), lambda i, t: (0, 0)),
        ],
        out_specs=pl.BlockSpec((None, tm, WIDTH), lambda i, t: (i, t, 0)),
        out_shape=jax.ShapeDtypeStruct((b, s, WIDTH), BF16),
        scratch_shapes=[pltpu.VMEM((SUBLANES, CONV_HALO + tm, WIDTH), F32),
                        pltpu.VMEM((CONV_WIDTH, SUBLANES, WIDTH), F32),
                        pltpu.VMEM((tm, WIDTH), F32)],
        compiler_params=_params(("arbitrary", "arbitrary")),
        name="conv_module",
    )(proj, proj, proj, proj, w_dw, row(b_dw), row(ln_g), row(ln_b))


def _stick_kernel(q_ref, k_ref, v_ref, mm_ref, o_ref, q2_ref, carry_ref, acc_ref):
    i = pl.program_id(1)
    npair = WIDTH // PAIR
    lane = lax.broadcasted_iota(jnp.int32, (BLK, PAIR), 1)
    lo_lanes = lane < HEAD_DIM
    row2 = lax.broadcasted_iota(jnp.int32, (2 * BLK, BLK), 0)
    col2 = lax.broadcasted_iota(jnp.int32, (2 * BLK, BLK), 1)
    causal2 = col2 < (row2 & (BLK - 1))
    scale = HEAD_DIM ** -0.5
    for p in range(npair):
        qp = q_ref[:, p * PAIR:(p + 1) * PAIR] * scale
        zero = jnp.zeros_like(qp)
        q2_ref[p] = jnp.concatenate(
            [jnp.where(lo_lanes, qp, zero), jnp.where(lo_lanes, zero, qp)], axis=0)
    carry_ref[...] = jnp.zeros_like(carry_ref)
    acc_ref[...] = jnp.zeros_like(acc_ref)

    def blocks(js, diag_first):
        cols = lambda p: slice(p * PAIR, (p + 1) * PAIR)
        rows = lambda j: pl.ds(pl.multiple_of(j * BLK, BLK), BLK)

        def scores(j, p):
            return _bdot_nt(q2_ref[p], k_ref[rows(j), cols(p)])

        def keep_sums(z, diag):
            soft = jnp.log(1.0 + jnp.exp2(jnp.abs(z) * -LOG2E))
            log_beta = jnp.minimum(z, 0.0) - soft
            log_keep = log_beta - z
            if diag:
                log_keep = jnp.where(causal2, log_keep, 0.0)
            return log_beta, _bdot(log_keep.astype(BF16), mm_ref[...])

        def weighted_values(j, p, diag, log_beta, sums):
            carry = carry_ref[p]
            att = jnp.exp(log_beta + (sums[:, :BLK] + carry))
            if diag:
                att = jnp.where(causal2, att, 0.0)
            carry_ref[p] = carry + sums[:, BLK:]
            att = att.astype(BF16)
            vb = v_ref[rows(j), cols(p)]
            zero = jnp.zeros_like(vb)
            v2 = jnp.concatenate([jnp.where(lo_lanes, vb, zero), jnp.where(lo_lanes, zero, vb)], axis=0)
            return _bdot(jnp.concatenate([att[:BLK], att[BLK:]], axis=1), v2)

        chains = [(j, p, diag_first and n == 0) for n, j in enumerate(js) for p in range(npair)]
        z, ks, pv = {}, {}, {}
        lag = STICK_STAGE_LAG
        for step in range(len(chains) + 3 * lag):
            if step < len(chains):
                z[step] = scores(*chains[step][:2])
            if 0 <= step - lag < len(chains):
                ks[step - lag] = keep_sums(z.pop(step - lag), chains[step - lag][2])
            if 0 <= step - 2 * lag < len(chains):
                pv[step - 2 * lag] = weighted_values(*chains[step - 2 * lag], *ks.pop(step - 2 * lag))
            if 0 <= step - 3 * lag < len(chains):
                acc_ref[chains[step - 3 * lag][1]] += pv.pop(step - 3 * lag)

    @pl.when(i >= 2)
    def _():
        blocks([i, i - 1, i - 2], True)

    @pl.when(i == 1)
    def _():
        blocks([i, i - 1], True)

    @pl.when(i == 0)
    def _():
        blocks([i], True)

    rest = i - jnp.minimum(i, 2)

    def live():
        c = carry_ref[...]
        worst = jnp.max(jnp.max(c, axis=0), axis=0, keepdims=True)
        return (worst[0, 0] >= EXP_IS_ZERO_BELOW).astype(jnp.int32)

    def two_blocks(state):
        t, _ = state
        j = rest - 1 - 2 * t
        blocks([j, j - 1], False)
        return t + 1, live()

    _, alive = lax.while_loop(lambda st: (st[0] < rest // 2) & (st[1] == 1), two_blocks,
                              (jnp.int32(0), live()))

    @pl.when((rest % 2 == 1) & (alive == 1))
    def _():
        blocks([0], False)

    for p in range(npair):
        o_ref[:, p * PAIR:(p + 1) * PAIR] = acc_ref[p].astype(o_ref.dtype)


def _suffix_sum_matrix():
    sp = np.arange(BLK)[:, None]
    sc = np.arange(BLK)[None, :]
    return jnp.asarray(
        np.concatenate([(sp > sc).astype(np.float32), np.ones((BLK, BLK), np.float32)], axis=1), BF16)


def _stick_breaking(proj):
    b, s, n = proj.shape
    npair = WIDTH // PAIR
    return pl.pallas_call(
        _stick_kernel,
        grid=(b, s // BLK),
        in_specs=[
            pl.BlockSpec((None, BLK, WIDTH), lambda i, t: (i, t, 2)),
            pl.BlockSpec((None, s, WIDTH), lambda i, t: (i, 0, 3)),
            pl.BlockSpec((None, s, WIDTH), lambda i, t: (i, 0, 4)),
            pl.BlockSpec((BLK, 2 * BLK), lambda i, t: (0, 0)),
        ],
        out_specs=pl.BlockSpec((None, BLK, WIDTH), lambda i, t: (i, t, 0)),
        out_shape=jax.ShapeDtypeStruct((b, s, WIDTH), BF16),
        scratch_shapes=[pltpu.VMEM((npair, 2 * BLK, PAIR), BF16),
                        pltpu.VMEM((npair, 2 * BLK, BLK), F32),
                        pltpu.VMEM((npair, BLK, PAIR), F32)],
        compiler_params=_params(("arbitrary", "arbitrary")),
        name="stick_breaking",
    )(proj, proj, proj, _suffix_sum_matrix())


def _lane_min_index(mask, lane):
    return jnp.min(jnp.where(mask, lane, float(LANES)), -1, keepdims=True)


def _out_kernel(l_ref, r_ref, w_ref, x_ref, mod_ref, g_ref, rh_ref, rl_ref, rb_ref, tri_ref,
                xo_ref, h_ref, cls_ref, rank_ref, cnt_ref, seen):
    tm = x_ref.shape[0]

    @pl.when((pl.program_id(0) == 0) & (pl.program_id(1) == 0))
    def _():
        seen[...] = jnp.zeros_like(seen)

    m = mod_ref[...]
    mix = _bdot(l_ref[...], w_ref[0:WIDTH, :]) + _bdot(r_ref[...], w_ref[WIDTH:, :])
    x = x_ref[...] + m[2:3] * mix
    xo_ref[...] = x
    h = _rms_mod(x, g_ref[...], m[4:5], m[3:4])
    h_ref[...] = h
    logits = _dot3(h, rh_ref[...], rl_ref[...]) + rb_ref[...]
    lane = lax.broadcasted_iota(jnp.int32, logits.shape, 1).astype(F32)
    is_group = (lane >= N_EXPERTS) & (lane < N_EXPERTS + N_GROUPS)
    glog = jnp.where(is_group, logits, -jnp.inf)
    gmax = jnp.max(glog, -1, keepdims=True)
    g_idx = _lane_min_index(glog == gmax, lane) - N_EXPERTS
    in_group = (lane >= g_idx * EXPERTS_PER_GROUP) & (lane < (g_idx + 1) * EXPERTS_PER_GROUP)
    ev = jnp.where(in_group, logits, -jnp.inf)
    v1 = jnp.max(ev, -1, keepdims=True)
    i1 = _lane_min_index(ev == v1, lane)
    ev2 = jnp.where(lane == i1, -jnp.inf, ev)
    v2 = jnp.max(ev2, -1, keepdims=True)
    i2 = _lane_min_index(ev2 == v2, lane)
    lo = jnp.minimum(i1, i2) - g_idx * EXPERTS_PER_GROUP
    hi = jnp.maximum(i1, i2) - g_idx * EXPERTS_PER_GROUP
    cls = g_idx * PAIRS_PER_GROUP + (lo * (7.0 - lo) * 0.5 + (hi - lo - 1.0))
    cls_t = jnp.transpose(jnp.broadcast_to(cls, (tm, LANES)))
    onehot_t = lax.broadcasted_iota(jnp.int32, (LANES, tm), 0).astype(F32) == cls_t
    ones_t = jnp.where(onehot_t, 1.0, 0.0)
    earlier = _bdot(ones_t.astype(BF16), tri_ref[...]) + seen[...]
    cls_ref[...] = cls_t[0:1, :].astype(jnp.int32)
    rank_ref[...] = jnp.sum(jnp.where(onehot_t, earlier, 0.0), 0, keepdims=True).astype(jnp.int32)
    seen[...] += jnp.sum(ones_t, 1, keepdims=True)
    cnt_ref[...] = seen[...]


def _out_proj(left, right, w_out, x, mod, g, r_hi, r_lo, r_b):
    b, s, d = x.shape
    tm = TOKEN_TILE
    nt = s // tm
    tile = lambda w: pl.BlockSpec((None, tm, w), lambda i, t: (i, t, 0))
    const = lambda shp: pl.BlockSpec(shp, lambda i, t: tuple(0 for _ in shp))
    row = pl.BlockSpec((None, 1, tm), lambda i, t: (i * nt + t, 0, 0))
    tri = np.arange(tm)[:, None] < np.arange(tm)[None, :]
    return pl.pallas_call(
        _out_kernel,
        grid=(b, nt),
        in_specs=[tile(WIDTH), tile(WIDTH), const((2 * WIDTH, d)), tile(d),
                  pl.BlockSpec((None, 6, d), lambda i, t: (i, 0, 0)),
                  const((1, d)), const((d, LANES)), const((d, LANES)), const((1, LANES)),
                  const((tm, tm))],
        out_specs=[tile(d), tile(d), row, row, const((LANES, 1))],
        out_shape=[jax.ShapeDtypeStruct((b, s, d), F32),
                   jax.ShapeDtypeStruct((b, s, d), F32),
                   jax.ShapeDtypeStruct((b * nt, 1, tm), jnp.int32),
                   jax.ShapeDtypeStruct((b * nt, 1, tm), jnp.int32),
                   jax.ShapeDtypeStruct((LANES, 1), F32)],
        scratch_shapes=[pltpu.VMEM((LANES, 1), F32)],
        compiler_params=_params(("arbitrary", "arbitrary")),
        name="out_proj_router",
    )(left, right, w_out, x, mod, g.reshape(1, d), r_hi, r_lo, r_b, jnp.asarray(tri, BF16))


def _router_weights(w_group, b_group, w_router, b_router):
    d = w_group.shape[0]
    w = jnp.concatenate([jnp.transpose(w_router, (1, 0, 2)).reshape(d, N_EXPERTS), w_group], axis=1)
    w = jnp.pad(w.astype(F32), ((0, 0), (0, LANES - w.shape[1])))
    bias = jnp.concatenate([b_router.reshape(N_EXPERTS), b_group]).astype(F32)
    bias = jnp.pad(bias, (0, LANES - bias.shape[0])).reshape(1, LANES)
    hi, lo = _split(w)
    return hi, lo, bias


def _to_token_tiles(x):
    n = x.shape[0]
    parts = [x[:, g * LANES:(g + 1) * LANES].reshape(n // 8, 8, LANES) for g in range(8)]
    return jnp.swapaxes(jnp.stack(parts, axis=1), 1, 2).reshape(n, 8, LANES)


def _from_token_tiles(v):
    n = v.shape[0]
    w = jnp.swapaxes(v.reshape(n // 8, 8, 8, LANES), 1, 2)
    return jnp.concatenate([w[:, g].reshape(n, LANES) for g in range(8)], axis=-1)


def _row_dma_loop(n, row_copy):
    def body(i, carry):
        row_copy(2 * i).start(priority=0)
        row_copy(2 * i + 1).start(priority=1)
        return carry
    lax.fori_loop(0, n // 2, body, 0, unroll=4)


def _dispatch_kernel(pad_ref, dest_ref, h_ref, hs_ref, buf, sem):
    tm = h_ref.shape[0]

    @pl.when(pl.program_id(0) == 0)
    def _():
        buf[...] = jnp.zeros_like(buf)
        for c in range(pad_ref.shape[0]):
            fill = pltpu.make_async_copy(buf, hs_ref.at[pl.ds(pad_ref[c], tm)], sem)
            fill.start()
            fill.wait()

    buf[...] = _to_token_tiles(h_ref[...])
    _row_dma_loop(tm, lambda r: pltpu.make_async_copy(
        buf.at[r], hs_ref.at[dest_ref[0, r]], sem))
    pltpu.make_async_copy(buf, hs_ref.at[pl.ds(0, tm)], sem).wait()


def _dispatch(h, dest, pad_start, n_slots):
    b, s, d = h.shape
    tm = TOKEN_TILE
    assert tm == SLOT_TILE
    nt = s // tm
    return pl.pallas_call(
        _dispatch_kernel,
        grid_spec=pltpu.PrefetchScalarGridSpec(
            num_scalar_prefetch=1,
            grid=(b * nt,),
            in_specs=[pl.BlockSpec((None, 1, tm), lambda i, pad: (i, 0, 0), memory_space=pltpu.SMEM),
                      pl.BlockSpec((None, tm, d), lambda i, pad: (i // nt, i % nt, 0))],
            out_specs=pl.BlockSpec(memory_space=pl.ANY),
            scratch_shapes=[pltpu.VMEM((tm, 8, LANES), F32), pltpu.SemaphoreType.DMA(())]),
        out_shape=jax.ShapeDtypeStruct((n_slots, 8, LANES), F32),
        compiler_params=_params(("arbitrary",)),
        name="moe_dispatch",
    )(pad_start, dest, h)


def _moe_kernel(blk_ref, ea_ref, eb_ref, grp_ref, used_ref, hs_ref, wga_ref, wua_ref, wda_ref,
                wgb_ref, wub_ref, wdb_ref, rh_ref, rb_ref, o_ref):
    n = pl.program_id(0)

    @pl.when(used_ref[n] == 1)
    def _():
        h = _from_token_tiles(hs_ref[...]).astype(BF16)
        logits = _bdot(h, rh_ref[...]) + rb_ref[...]
        lane = lax.broadcasted_iota(jnp.int32, logits.shape, 1)
        pick = lambda idx: jnp.sum(jnp.where(lane == idx, logits, 0.0), -1, keepdims=True)
        is_group = (lane >= N_EXPERTS) & (lane < N_EXPERTS + N_GROUPS)
        g_w = 1.0 / jnp.sum(
            jnp.where(is_group, jnp.exp(logits - pick(N_EXPERTS + grp_ref[n])), 0.0),
            -1, keepdims=True)
        la = pick(ea_ref[n])
        lb = pick(eb_ref[n])

        def expert(wg_ref, wu_ref, wd_ref):
            a = _bdot(h, wg_ref[...])
            hid = (a * jax.nn.sigmoid(a)) * _bdot(h, wu_ref[...])
            return _bdot(hid.astype(BF16), wd_ref[...])

        y = (g_w * (1.0 / (1.0 + jnp.exp(lb - la)))) * expert(wga_ref, wua_ref, wda_ref)
        y = y + (g_w * (1.0 / (1.0 + jnp.exp(la - lb)))) * expert(wgb_ref, wub_ref, wdb_ref)
        o_ref[...] = _to_token_tiles(y)

    @pl.when(used_ref[n] == 0)
    def _():
        o_ref[...] = jnp.zeros_like(o_ref)


def _moe(hs, plan, w_gate, w_up, w_down, r_hi, r_b):
    n_slots = hs.shape[0]
    ne, d, de = w_gate.shape
    tm = SLOT_TILE
    slot = pl.BlockSpec((tm, 8, LANES), lambda n, *_: (n, 0, 0))
    slot_in = pl.BlockSpec((tm, 8, LANES), lambda n, blk, *_: (blk[n], 0, 0))
    first = lambda shp: pl.BlockSpec((None,) + shp, lambda n, blk, ea, eb, grp, used: (ea[n], 0, 0))
    second = lambda shp: pl.BlockSpec((None,) + shp, lambda n, blk, ea, eb, grp, used: (eb[n], 0, 0))
    const = lambda shp: pl.BlockSpec(shp, lambda n, *_: tuple(0 for _ in shp))
    return pl.pallas_call(
        _moe_kernel,
        grid_spec=pltpu.PrefetchScalarGridSpec(
            num_scalar_prefetch=5,
            grid=(n_slots // tm,),
            in_specs=[slot_in, first((d, de)), first((d, de)), first((de, d)),
                      second((d, de)), second((d, de)), second((de, d)),
                      const((d, LANES)), const((1, LANES))],
            out_specs=slot),
        out_shape=jax.ShapeDtypeStruct((n_slots, 8, LANES), F32),
        compiler_params=_params(("arbitrary",)),
        name="moe_experts",
    )(*plan, hs, w_gate, w_up, w_down, w_gate, w_up, w_down, r_hi, r_b)


def _combine_kernel(dest_ref, ys_ref, x_ref, mod_ref, fg_ref, o_ref, buf, sem):
    tm = x_ref.shape[0]
    _row_dma_loop(tm, lambda r: pltpu.make_async_copy(
        ys_ref.at[dest_ref[0, r]], buf.at[r], sem))
    pltpu.make_async_copy(ys_ref.at[pl.ds(0, tm)], buf, sem).wait()
    x = x_ref[...] + mod_ref[5:6, :] * _from_token_tiles(buf[...])
    o_ref[...] = x * lax.rsqrt(jnp.mean(x * x, -1, keepdims=True) + EPS) * fg_ref[...]


def _combine(ys, dest, x, mod, final_g):
    b, s, d = x.shape
    tm = TOKEN_TILE
    nt = s // tm
    return pl.pallas_call(
        _combine_kernel,
        grid=(b * nt,),
        in_specs=[pl.BlockSpec((None, 1, tm), lambda i: (i, 0, 0), memory_space=pltpu.SMEM),
                  pl.BlockSpec(memory_space=pl.ANY),
                  pl.BlockSpec((None, tm, d), lambda i: (i // nt, i % nt, 0)),
                  pl.BlockSpec((None, 6, d), lambda i: (i // nt, 0, 0)),
                  pl.BlockSpec((1, d), lambda i: (0, 0))],
        out_specs=pl.BlockSpec((None, tm, d), lambda i: (i // nt, i % nt, 0)),
        out_shape=jax.ShapeDtypeStruct((b, s, d), F32),
        scratch_shapes=[pltpu.VMEM((tm, 8, LANES), F32), pltpu.SemaphoreType.DMA(())],
        compiler_params=_params(("arbitrary",)),
        name="moe_combine",
    )(dest, ys, x, mod, final_g.reshape(1, d))


def _combine_in_kernel(dest_ref, next_ref, ys_ref, x_ref, mod_ref, g_ref, nmod_ref, w_ref,
                       xo_ref, o_ref, *rest, dilations):
    *rest, buf, sem = rest
    tm = x_ref.shape[0]
    i = pl.program_id(0)
    slot = i % 2

    def gather(idx_ref, to):
        _row_dma_loop(tm, lambda r: pltpu.make_async_copy(
            ys_ref.at[idx_ref[0, r]], buf.at[to].at[r], sem.at[to]))

    @pl.when(i == 0)
    def _():
        gather(dest_ref, 0)

    @pl.when(i + 1 < pl.num_programs(0))
    def _():
        gather(next_ref, 1 - slot)

    pltpu.make_async_copy(ys_ref.at[pl.ds(0, tm)], buf.at[slot], sem.at[slot]).wait()
    x = x_ref[...] + mod_ref[5:6, :] * _from_token_tiles(buf[slot])
    xo_ref[...] = x
    _in_body(x, g_ref, nmod_ref, w_ref, o_ref, rest, dilations)


def _combine_in_proj(ys, dest, x, mod, g_next, mod_next, w_in, dilations=()):
    b, s, d = x.shape
    n = w_in.shape[1]
    tm = TOKEN_TILE
    nt = s // tm
    last = b * nt - 1
    tile = lambda i: (i // nt, i % nt, 0)
    specs, shapes, scratch = _in_proj_outputs(b, s, n, tm, dilations, tile)
    outs = pl.pallas_call(
        functools.partial(_combine_in_kernel, dilations=tuple(dilations)),
        grid=(b * nt,),
        in_specs=[pl.BlockSpec((None, 1, tm), lambda i: (i, 0, 0), memory_space=pltpu.SMEM),
                  pl.BlockSpec((None, 1, tm), lambda i: (jnp.minimum(i + 1, last), 0, 0),
                               memory_space=pltpu.SMEM),
                  pl.BlockSpec(memory_space=pl.ANY),
                  pl.BlockSpec((None, tm, d), tile),
                  pl.BlockSpec((None, 6, d), lambda i: (i // nt, 0, 0)),
                  pl.BlockSpec((1, d), lambda i: (0, 0)),
                  pl.BlockSpec((None, 6, d), lambda i: (i // nt, 0, 0)),
                  pl.BlockSpec((d, n), lambda i: (0, 0))],
        out_specs=[pl.BlockSpec((None, tm, d), tile)] + specs,
        out_shape=[jax.ShapeDtypeStruct((b, s, d), F32)] + shapes,
        scratch_shapes=scratch + [pltpu.VMEM((2, tm, 8, LANES), F32), pltpu.SemaphoreType.DMA((2,))],
        compiler_params=_params(("arbitrary",)),
        name="moe_combine_in_proj",
    )(dest, dest, ys, x, mod, g_next.reshape(1, d), mod_next, w_in)
    return outs


def _routing_plan(cls, rank, counts, n_tokens):
    tm = SLOT_TILE
    n_tiles = n_tokens // tm + N_CLASSES
    counts = counts[:N_CLASSES, 0].astype(jnp.int32)
    tiles = (counts + tm - 1) // tm
    tile_end = jnp.cumsum(tiles)
    classes = jnp.arange(N_CLASSES, dtype=jnp.int32)
    first_slot = (tile_end - tiles) * tm
    dest = jnp.sum(jnp.where(cls[..., None] == classes, first_slot, 0), -1) + rank
    ids = jnp.arange(n_tiles, dtype=jnp.int32)
    used = ids < tile_end[-1]
    tile_cls = jnp.sum(
        (jnp.minimum(ids, tile_end[-1] - 1)[:, None] >= tile_end).astype(jnp.int32), -1)
    grp = tile_cls // PAIRS_PER_GROUP
    pair = tile_cls % PAIRS_PER_GROUP
    pair_a = jnp.asarray([0, 0, 0, 1, 1, 2], jnp.int32)
    pair_b = jnp.asarray([1, 2, 3, 2, 3, 3], jnp.int32)
    eb = grp * EXPERTS_PER_GROUP + pair_b[pair]
    ea = jnp.where(used, grp * EXPERTS_PER_GROUP + pair_a[pair], eb)
    blk = jnp.where(used, ids, 0)
    tail = jnp.minimum(tile_end[-1] * tm + classes * tm, (n_tiles - 1) * tm)
    zero_blocks = jnp.concatenate([first_slot + counts, tail])
    return dest, zero_blocks, (blk, ea, eb, grp, used.astype(jnp.int32))


@jax.jit
def kernel(x, c, rel_bias, norm1_g, norm2_g, w_ada, b_ada, ev_w_in, ev_w_out, ev_gmlp_ln_g, ev_gmlp_ln_b, ev_w_s, ev_b_s, od_w_in, od_w_out, od_w_dw, od_b_dw, od_conv_ln_g, od_conv_ln_b, moe_w_group, moe_b_group, moe_w_router, moe_b_router, moe_w_gate, moe_w_up, moe_w_down, final_norm_g):
    b, s, d = x.shape
    depth = w_ada.shape[0]
    mods = _ada(c, w_ada, b_ada).reshape(depth, b, 6, d)
    biases = [_branch_bias(rel_bias, dil) for _, dil in DILATED_BRANCHES]
    dils = [dil for _, dil in DILATED_BRANCHES]
    grouped_dils = [dil for dil in dils if dil > 1]

    def in_weights(layer):
        even = layer % 2 == 0
        return (ev_w_in if even else od_w_in)[layer // 2].astype(BF16), grouped_dils if even else []

    w_in, extra = in_weights(0)
    proj, *grouped = _in_proj(x, norm1_g[0], mods[0], w_in, extra)
    for layer in range(depth):
        j = layer // 2
        mod = mods[layer]
        if layer % 2 == 0:
            views = {1: (proj, proj.shape[-1] // WIDTH)}
            views.update({dil: (g, 3) for dil, g in zip(grouped_dils, grouped)})
            outs = [_dilated_branch(views[dil][0], bias, dil, views[dil][1])
                    for bias, dil in zip(biases, dils)]
            left = _combine_branches(outs, dils, s)
            right = _gmlp(proj, ev_gmlp_ln_g[j], ev_gmlp_ln_b[j], ev_w_s[j], ev_b_s[j])
            w_out = ev_w_out[j]
        else:
            left = _conv_module(proj, od_w_dw[j], od_b_dw[j], od_conv_ln_g[j], od_conv_ln_b[j])
            right = _stick_breaking(proj)
            w_out = od_w_out[j]
        r_hi, r_lo, r_b = _router_weights(moe_w_group[layer], moe_b_group[layer],
                                          moe_w_router[layer], moe_b_router[layer])
        x, h, cls, rank, counts = _out_proj(left, right, w_out.astype(BF16), x, mod,
                                            norm2_g[layer], r_hi, r_lo, r_b)
        dest, pad_start, plan = _routing_plan(cls, rank, counts, b * s)
        hs = _dispatch(h, dest, pad_start, (b * s // SLOT_TILE + N_CLASSES) * SLOT_TILE)
        ys = _moe(hs, plan, moe_w_gate[layer].astype(BF16), moe_w_up[layer].astype(BF16),
                  moe_w_down[layer].astype(BF16), r_hi, r_b)
        if layer == depth - 1:
            x = _combine(ys, dest, x, mod, final_norm_g)
        else:
            w_in, extra = in_weights(layer + 1)
            x, proj, *grouped = _combine_in_proj(ys, dest, x, mod, norm1_g[layer + 1],
                                                 mods[layer + 1], w_in, extra)
    return x
```

```python
import functools
import math

import numpy as np
import jax
import jax.numpy as jnp
from jax import lax
from jax.experimental import pallas as pl
from jax.experimental.pallas import tpu as pltpu

F32 = jnp.float32
BF16 = jnp.bfloat16

HEAD_DIM = 64
HEADS = 8
WIDTH = HEADS * HEAD_DIM
PAIR = 2 * HEAD_DIM
BLK = 128
DILATED_BRANCHES = ((128, 1), (512, 4), (2048, 16))
N_BUCKETS = 32
MAX_DISTANCE = 2048
CONV_WIDTH = 31
CONV_HALO = 32
CONV_TAP_UNROLL = 8
STICK_STAGE_LAG = 2
DILATED_STAGE_LAG = 1
DILATED_QUERY_ROWS = 128
B_GROUPS = 4
N_GROUPS = 4
EXPERTS_PER_GROUP = 4
N_EXPERTS = N_GROUPS * EXPERTS_PER_GROUP
PAIRS_PER_GROUP = EXPERTS_PER_GROUP * (EXPERTS_PER_GROUP - 1) // 2
N_CLASSES = N_GROUPS * PAIRS_PER_GROUP
TOKEN_TILE = 512
SLOT_TILE = 512
EPS = 1e-6
NEG = -1e30
LOG2E = 1.4426950408889634
EXP_IS_ZERO_BELOW = -104.0
LANES = 128
SUBLANES = 8
ROW_TILE = 8
ADA_COL_TILE = 1024
VMEM_LIMIT = 48 * 1024 * 1024


def _params(sem):
    return pltpu.CompilerParams(dimension_semantics=sem, vmem_limit_bytes=VMEM_LIMIT)


def _bdot(a, b):
    return jnp.dot(a, b, preferred_element_type=F32)


def _bdot_nt(a, b):
    return lax.dot_general(a, b, (((1,), (1,)), ((), ())), preferred_element_type=F32)


def _split(a):
    hi = a.astype(BF16)
    lo = (a - hi.astype(F32)).astype(BF16)
    return hi, lo


def _dot3(a, w_hi, w_lo):
    a_hi, a_lo = _split(a)
    return _bdot(a_hi, w_hi) + (_bdot(a_hi, w_lo) + _bdot(a_lo, w_hi))


def _ada_kernel(c_ref, w_ref, b_ref, o_ref):
    c = c_ref[...]
    sc = c * jax.nn.sigmoid(c)
    w_hi, w_lo = _split(w_ref[...])
    o_ref[...] = _dot3(sc, w_hi, w_lo) + b_ref[...]


def _ada(c, w_ada, b_ada):
    depth, d, n = w_ada.shape
    b = c.shape[0]
    tn = ADA_COL_TILE
    return pl.pallas_call(
        _ada_kernel,
        grid=(depth, n // tn),
        in_specs=[
            pl.BlockSpec((b, d), lambda l, j: (0, 0)),
            pl.BlockSpec((None, d, tn), lambda l, j: (l, 0, j)),
            pl.BlockSpec((None, 1, tn), lambda l, j: (l, 0, j)),
        ],
        out_specs=pl.BlockSpec((None, b, tn), lambda l, j: (l, 0, j)),
        out_shape=jax.ShapeDtypeStruct((depth, b, n), F32),
        compiler_params=_params(("arbitrary", "arbitrary")),
        name="ada_mod",
    )(c, w_ada, b_ada.reshape(depth, 1, n))


def _rms_mod(x, g, scale_row, shift_row):
    y = x * lax.rsqrt(jnp.mean(x * x, -1, keepdims=True) + EPS) * g
    return y * (1.0 + scale_row) + shift_row


def _in_body(x, g_ref, mod_ref, w_ref, o_ref, rest, dilations):
    m = mod_ref[...]
    h = _rms_mod(x, g_ref[...], m[1:2], m[0:1]).astype(BF16)
    tm = x.shape[0]
    n = w_ref.shape[1]
    per = WIDTH // LANES
    for j in range(n // WIDTH):
        cols = slice(j * WIDTH, (j + 1) * WIDTH)
        y = _bdot(h, w_ref[:, cols])
        o_ref[:, cols] = y.astype(o_ref.dtype)
        if dilations and j < 3:
            for c in range(per):
                rest[-1][j * per + c] = y[:, c * LANES:(c + 1) * LANES]
    for d, out in zip(dilations, rest):
        for r in range(d):
            for c in range(3 * per):
                lo = (r * 3 * per + c) * LANES
                out[:, lo:lo + LANES] = rest[-1][c, pl.ds(r, tm // d, stride=d), :].astype(out.dtype)


def _in_kernel(x_ref, g_ref, mod_ref, w_ref, o_ref, *rest, dilations):
    _in_body(x_ref[...], g_ref, mod_ref, w_ref, o_ref, rest, dilations)


def _in_proj_outputs(b, s, n, tm, dilations, index):
    qkv = 3 * WIDTH
    specs = [pl.BlockSpec((None, tm, n), index)] + [
        pl.BlockSpec((None, tm // dil, dil * qkv), index) for dil in dilations]
    shapes = [jax.ShapeDtypeStruct((b, s, n), BF16)] + [
        jax.ShapeDtypeStruct((b, s // dil, dil * qkv), BF16) for dil in dilations]
    scratch = [pltpu.VMEM((qkv // LANES, tm, LANES), F32)] if dilations else []
    return specs, shapes, scratch


def _in_proj(x, g, mod, w_in, dilations=()):
    b, s, d = x.shape
    n = w_in.shape[1]
    tm = TOKEN_TILE
    specs, shapes, scratch = _in_proj_outputs(b, s, n, tm, dilations, lambda i, t: (i, t, 0))
    outs = pl.pallas_call(
        functools.partial(_in_kernel, dilations=tuple(dilations)),
        grid=(b, s // tm),
        in_specs=[
            pl.BlockSpec((None, tm, d), lambda i, t: (i, t, 0)),
            pl.BlockSpec((1, d), lambda i, t: (0, 0)),
            pl.BlockSpec((None, 6, d), lambda i, t: (i, 0, 0)),
            pl.BlockSpec((d, n), lambda i, t: (0, 0)),
        ],
        out_specs=specs,
        out_shape=shapes,
        scratch_shapes=scratch,
        compiler_params=_params(("arbitrary", "arbitrary")),
        name="in_proj",
    )(x, g.reshape(1, d), mod, w_in)
    return outs if dilations else outs[0]


def _t5_bucket(dist):
    max_exact = N_BUCKETS // 2
    d = np.maximum(dist, 1).astype(np.float32)
    large = max_exact + (np.log(d / max_exact) / np.log(MAX_DISTANCE / max_exact)
                         * (N_BUCKETS - max_exact)).astype(np.int32)
    large = np.minimum(large, N_BUCKETS - 1)
    return np.where(dist < max_exact, dist, large).astype(np.int32)


def _branch_bias(rel_table, dilation):
    qi = np.arange(BLK)[:, None]
    kj = np.arange(2 * BLK)[None, :]
    bucket = _t5_bucket(np.clip(qi + BLK - kj, 0, None) * dilation)
    onehot = np.eye(N_BUCKETS, dtype=np.float32)[bucket]
    bias = jnp.einsum("qkb,bh->hqk", onehot, rel_table.astype(F32), precision=lax.Precision.HIGHEST)
    dist = qi + BLK - kj
    in_window = (dist >= 0) & (dist <= BLK)
    return jnp.where(in_window[None], bias, NEG)


def _dil_kernel(q_ref, kp_ref, kc_ref, vp_ref, vc_ref, bias_ref, o_ref, m_ref, l_ref,
                kbuf, vbuf, *, tq):
    n = pl.program_id(2)
    nsub = tq // BLK
    kbuf[0:BLK] = kp_ref[...]
    kbuf[BLK:] = kc_ref[...]
    vbuf[0:BLK] = vp_ref[...]
    vbuf[BLK:] = vc_ref[...]
    qb = DILATED_QUERY_ROWS
    kj = lax.broadcasted_iota(jnp.int32, (qb, 2 * BLK), 1)
    lane = lax.broadcasted_iota(jnp.int32, (qb, PAIR), 1)
    scale = HEAD_DIM ** -0.5
    has_prev = kj >= jnp.where(n == 0, BLK, 0)
    rows = lambda u: slice(u * qb, (u + 1) * qb)
    krows = lambda u: slice(u * qb // BLK * BLK, (u * qb // BLK + 2) * BLK)
    brows = lambda u: slice(u * qb % BLK, u * qb % BLK + qb)
    cols = lambda p: slice(p * PAIR, (p + 1) * PAIR)

    def logits(i, p, hh):
        q_pair = q_ref[rows(i), cols(p)] * scale
        head_lanes = (lane >= HEAD_DIM) if hh else (lane < HEAD_DIM)
        qm = jnp.where(head_lanes, q_pair, jnp.zeros_like(q_pair))
        logit = _bdot_nt(qm, kbuf[krows(i), cols(p)]) + bias_ref[2 * p + hh, brows(i), :]
        return jnp.where(has_prev, logit, NEG) if i * qb < BLK else logit

    def softmax_values(i, p, logit):
        mx = jnp.max(logit, -1, keepdims=True)
        pr = jnp.exp(logit - mx)
        den = jnp.sum(pr, -1, keepdims=True)
        return mx, den, _bdot(pr.astype(BF16), vbuf[krows(i), cols(p)])

    chains = [(i, p, hh) for i in range(tq // qb) for p in range(HEADS // 2) for hh in range(2)]
    lag = DILATED_STAGE_LAG
    lg, sv, first_head = {}, {}, {}
    m_tile, l_tile = {}, {}
    for step in range(len(chains) + 2 * lag):
        if step < len(chains):
            lg[step] = logits(*chains[step])
        if 0 <= step - lag < len(chains):
            i, p, _ = chains[step - lag]
            sv[step - lag] = softmax_values(i, p, lg.pop(step - lag))
        if 0 <= step - 2 * lag < len(chains):
            i, p, hh = chains[step - 2 * lag]
            mx, den, pv = sv.pop(step - 2 * lag)
            h = 2 * p + hh
            m_tile[i] = jnp.where(lane == h, mx, m_tile.get(i, jnp.zeros((qb, LANES), F32)))
            l_tile[i] = jnp.where(lane == h, den, l_tile.get(i, jnp.ones((qb, LANES), F32)))
            if hh == 0:
                first_head[i, p] = pv / den
            else:
                o_ref[rows(i), cols(p)] = jnp.where(
                    lane < HEAD_DIM, first_head.pop((i, p)), pv / den).astype(o_ref.dtype)
            if h == HEADS - 1:
                m_ref[rows(i), :] = m_tile.pop(i)
                l_ref[rows(i), :] = l_tile.pop(i)


def _dilated_branch(view, bias, dilation, ncol):
    b, sub, _ = view.shape
    tq = min(TOKEN_TILE, sub)
    nsub = tq // BLK
    cur = lambda c: pl.BlockSpec((None, tq, WIDTH), lambda i, r, t: (i, t, r * ncol + c))
    prev = lambda c: pl.BlockSpec(
        (None, BLK, WIDTH), lambda i, r, t: (i, jnp.maximum(t * nsub - 1, 0), r * ncol + c))
    return pl.pallas_call(
        functools.partial(_dil_kernel, tq=tq),
        grid=(b, dilation, sub // tq),
        in_specs=[cur(0), prev(1), cur(1), prev(2), cur(2),
                  pl.BlockSpec((HEADS, BLK, 2 * BLK), lambda i, r, t: (0, 0, 0))],
        out_specs=[
            pl.BlockSpec((None, tq, WIDTH), lambda i, r, t: (i, t, r)),
            pl.BlockSpec((None, tq, LANES), lambda i, r, t: (i, t, r)),
            pl.BlockSpec((None, tq, LANES), lambda i, r, t: (i, t, r)),
        ],
        out_shape=[
            jax.ShapeDtypeStruct((b, sub, dilation * WIDTH), BF16),
            jax.ShapeDtypeStruct((b, sub, dilation * LANES), F32),
            jax.ShapeDtypeStruct((b, sub, dilation * LANES), F32),
        ],
        scratch_shapes=[pltpu.VMEM((BLK + tq, WIDTH), BF16), pltpu.VMEM((BLK + tq, WIDTH), BF16)],
        compiler_params=_params(("arbitrary", "arbitrary", "arbitrary")),
        name=f"dilated_d{dilation}",
    )(view, view, view, view, view, bias)


def _comb_kernel(*refs, dilations):
    nb = len(dilations)
    o_refs, m_refs, l_refs = refs[:nb], refs[nb:2 * nb], refs[2 * nb:3 * nb]
    e_ref, out_ref, o_scr, s_scr = refs[3 * nb:]
    tm = out_ref.shape[0]
    per = WIDTH // LANES
    os, ms, ls = [], [], []
    for bi, d in enumerate(dilations):
        if d == 1:
            os.append([o_refs[bi][:, c * LANES:(c + 1) * LANES].astype(F32) for c in range(per)])
            ms.append(m_refs[bi][...])
            ls.append(l_refs[bi][...])
            continue
        rows = lambda r: pl.ds(r, tm // d, stride=d)
        for r in range(d):
            for c in range(per):
                lo = (r * per + c) * LANES
                o_scr[bi, c, rows(r), :] = o_refs[bi][:, lo:lo + LANES].astype(F32)
            s_scr[bi, 0, rows(r), :] = m_refs[bi][:, r * LANES:(r + 1) * LANES]
            s_scr[bi, 1, rows(r), :] = l_refs[bi][:, r * LANES:(r + 1) * LANES]
        os.append([o_scr[bi, c] for c in range(per)])
        ms.append(s_scr[bi, 0])
        ls.append(s_scr[bi, 1])
    m_all = functools.reduce(jnp.maximum, ms)
    es = [l * jnp.exp(m - m_all) for l, m in zip(ls, ms)]
    tot = functools.reduce(lambda x, y: x + y, es)
    acc = [None] * per
    for e, o in zip(es, os):
        w_hi, w_lo = _split(e / tot)
        w = _bdot(w_hi, e_ref[...]) + _bdot(w_lo, e_ref[...])
        for c in range(per):
            term = w[:, c * LANES:(c + 1) * LANES] * o[c]
            acc[c] = term if acc[c] is None else acc[c] + term
    for c in range(per):
        out_ref[:, c * LANES:(c + 1) * LANES] = acc[c].astype(out_ref.dtype)


def _combine_branches(outs, dilations, s):
    b = outs[0][0].shape[0]
    tm = TOKEN_TILE
    nb = len(dilations)
    expand = np.zeros((LANES, WIDTH), np.float32)
    for h in range(HEADS):
        expand[h, h * HEAD_DIM:(h + 1) * HEAD_DIM] = 1.0
    grouped = lambda w: [pl.BlockSpec((None, tm // d, d * w), lambda i, t: (i, t, 0)) for d in dilations]
    return pl.pallas_call(
        functools.partial(_comb_kernel, dilations=tuple(dilations)),
        grid=(b, s // tm),
        in_specs=grouped(WIDTH) + grouped(LANES) + grouped(LANES)
        + [pl.BlockSpec((LANES, WIDTH), lambda i, t: (0, 0))],
        out_specs=pl.BlockSpec((None, tm, WIDTH), lambda i, t: (i, t, 0)),
        out_shape=jax.ShapeDtypeStruct((b, s, WIDTH), BF16),
        scratch_shapes=[pltpu.VMEM((nb, WIDTH // LANES, tm, LANES), F32),
                        pltpu.VMEM((nb, 2, tm, LANES), F32)],
        compiler_params=_params(("arbitrary", "arbitrary")),
        name="dilated_combine",
    )(*[o for o, _, _ in outs], *[m for _, m, _ in outs], *[l for _, _, l in outs],
      jnp.asarray(expand, BF16))


def _gelu(x):
    return 0.5 * x * (1.0 + jnp.tanh(math.sqrt(2.0 / math.pi) * (x + 0.044715 * (x * x * x))))


def _gmlp_kernel(u_ref, v_ref, g_ref, b_ref, ws_ref, bs_ref, o_ref):
    tm = u_ref.shape[0]
    u = _gelu(u_ref[...].astype(F32))
    v = _gelu(v_ref[...].astype(F32))
    mu = jnp.mean(v, -1, keepdims=True)
    vc = v - mu
    var = jnp.mean(vc * vc, -1, keepdims=True)
    v = (vc * lax.rsqrt(var + EPS) * g_ref[...] + b_ref[...]).astype(BF16)
    r = lax.broadcasted_iota(jnp.int32, (BLK, BLK), 0)
    c = lax.broadcasted_iota(jnp.int32, (BLK, BLK), 1)
    for g in range(B_GROUPS):
        w = jnp.where(c <= r, ws_ref[g], 0.0).astype(BF16)
        cols = slice(g * BLK, (g + 1) * BLK)
        for ch in range(tm // BLK):
            rows = slice(ch * BLK, (ch + 1) * BLK)
            mixed = _bdot(w, v[rows, cols]) + bs_ref[g]
            o_ref[rows, cols] = (u[rows, cols] * mixed).astype(o_ref.dtype)


def _gmlp(proj, ln_g, ln_b, w_s, b_s):
    b, s, n = proj.shape
    tm = TOKEN_TILE
    return pl.pallas_call(
        _gmlp_kernel,
        grid=(b, s // tm),
        in_specs=[
            pl.BlockSpec((None, tm, WIDTH), lambda i, t: (i, t, 3)),
            pl.BlockSpec((None, tm, WIDTH), lambda i, t: (i, t, 4)),
            pl.BlockSpec((1, WIDTH), lambda i, t: (0, 0)),
            pl.BlockSpec((1, WIDTH), lambda i, t: (0, 0)),
            pl.BlockSpec((B_GROUPS, BLK, BLK), lambda i, t: (0, 0, 0)),
            pl.BlockSpec((B_GROUPS, BLK, 1), lambda i, t: (0, 0, 0)),
        ],
        out_specs=pl.BlockSpec((None, tm, WIDTH), lambda i, t: (i, t, 0)),
        out_shape=jax.ShapeDtypeStruct((b, s, WIDTH), BF16),
        compiler_params=_params(("arbitrary", "arbitrary")),
        name="gmlp_gate",
    )(proj, proj, ln_g.reshape(1, WIDTH), ln_b.reshape(1, WIDTH), w_s,
      b_s.reshape(B_GROUPS, BLK, 1))


def _conv_kernel(a_ref, g_ref, ah_ref, gh_ref, w_ref, bdw_ref, lng_ref, lnb_ref, o_ref,
                 hbuf, wbuf, ybuf):
    tm = a_ref.shape[0]
    t = pl.program_id(1)
    halo = ah_ref[...].astype(F32) * jax.nn.sigmoid(gh_ref[...].astype(F32))
    hbuf[0, 0:CONV_HALO] = jnp.where(t == 0, 0.0, halo)
    hbuf[0, CONV_HALO:] = a_ref[...].astype(F32) * jax.nn.sigmoid(g_ref[...].astype(F32))
    n_shifted = tm + CONV_HALO - SUBLANES
    for s in range(1, SUBLANES):
        hbuf[s, 0:n_shifted] = hbuf[0, pl.ds(s, n_shifted), :]
    for k in range(CONV_WIDTH):
        wbuf[k] = jnp.broadcast_to(w_ref[k:k + 1, :], (SUBLANES, WIDTH))
    rc = 32
    off = CONV_HALO - (CONV_WIDTH - 1)

    def chunk(ci, carry):
        base = pl.multiple_of(ci * rc, rc)
        def tap(k, acc):
            q = off + k
            s = q & (SUBLANES - 1)
            start = pl.multiple_of(base + (q - s), SUBLANES)
            win = hbuf[s, pl.ds(start, rc), :].reshape(rc // SUBLANES, SUBLANES, WIDTH)
            return acc + (wbuf[k] * win).reshape(rc, WIDTH)

        acc = lax.fori_loop(0, CONV_WIDTH, tap, jnp.zeros((rc, WIDTH), F32) + bdw_ref[...],
                            unroll=CONV_TAP_UNROLL)
        ybuf[pl.ds(base, rc), :] = acc
        return carry

    lax.fori_loop(0, tm // rc, chunk, 0)
    acc = ybuf[...]
    mu = jnp.mean(acc, -1, keepdims=True)
    xc = acc - mu
    var = jnp.mean(xc * xc, -1, keepdims=True)
    y = xc * lax.rsqrt(var + EPS) * lng_ref[...] + lnb_ref[...]
    o_ref[...] = (y * jax.nn.sigmoid(y)).astype(o_ref.dtype)


def _conv_module(proj, w_dw, b_dw, ln_g, ln_b):
    b, s, n = proj.shape
    tm = TOKEN_TILE
    hb = tm // CONV_HALO
    row = lambda v: v.reshape(1, WIDTH)
    halo = lambda c: pl.BlockSpec(
        (None, CONV_HALO, WIDTH), lambda i, t: (i, jnp.maximum(t * hb - 1, 0), c))
    return pl.pallas_call(
        _conv_kernel,
        grid=(b, s // tm),
        in_specs=[
            pl.BlockSpec((None, tm, WIDTH), lambda i, t: (i, t, 0)),
            pl.BlockSpec((None, tm, WIDTH), lambda i, t: (i, t, 1)),
            halo(0), halo(1),
            pl.BlockSpec((CONV_WIDTH, WIDTH), lambda i, t: (0, 0)),
            pl.BlockSpec((1, WIDTH), lambda i, t: (0, 0)),
            pl.BlockSpec((1, WIDTH), lambda i, t: (0, 0)),
            pl.BlockSpec((1, WIDTH), lambda i, t: (0, 0)),
        ],
        out_specs=pl.BlockSpec((None, tm, WIDTH), lambda i, t: (i, t, 0)),
        out_shape=jax.ShapeDtypeStruct((b, s, WIDTH), BF16),
        scratch_shapes=[pltpu.VMEM((SUBLANES, CONV_HALO + tm, WIDTH), F32),
                        pltpu.VMEM((CONV_WIDTH, SUBLANES, WIDTH), F32),
                        pltpu.VMEM((tm, WIDTH), F32)],
        compiler_params=_params(("arbitrary", "arbitrary")),
        name="conv_module",
    )(proj, proj, proj, proj, w_dw, row(b_dw), row(ln_g), row(ln_b))


def _stick_kernel(q_ref, k_ref, v_ref, mm_ref, o_ref, q2_ref, carry_ref, acc_ref):
    i = pl.program_id(1)
    npair = WIDTH // PAIR
    lane = lax.broadcasted_iota(jnp.int32, (BLK, PAIR), 1)
    lo_lanes = lane < HEAD_DIM
    row2 = lax.broadcasted_iota(jnp.int32, (2 * BLK, BLK), 0)
    col2 = lax.broadcasted_iota(jnp.int32, (2 * BLK, BLK), 1)
    causal2 = col2 < (row2 & (BLK - 1))
    scale = HEAD_DIM ** -0.5
    for p in range(npair):
        qp = q_ref[:, p * PAIR:(p + 1) * PAIR] * scale
        zero = jnp.zeros_like(qp)
        q2_ref[p] = jnp.concatenate(
            [jnp.where(lo_lanes, qp, zero), jnp.where(lo_lanes, zero, qp)], axis=0)
    carry_ref[...] = jnp.zeros_like(carry_ref)
    acc_ref[...] = jnp.zeros_like(acc_ref)

    def blocks(js, diag_first):
        cols = lambda p: slice(p * PAIR, (p + 1) * PAIR)
        rows = lambda j: pl.ds(pl.multiple_of(j * BLK, BLK), BLK)

        def scores(j, p):
            return _bdot_nt(q2_ref[p], k_ref[rows(j), cols(p)])

        def keep_sums(z, diag):
            soft = jnp.log(1.0 + jnp.exp2(jnp.abs(z) * -LOG2E))
            log_beta = jnp.minimum(z, 0.0) - soft
            log_keep = log_beta - z
            if diag:
                log_keep = jnp.where(causal2, log_keep, 0.0)
            return log_beta, _bdot(log_keep.astype(BF16), mm_ref[...])

        def weighted_values(j, p, diag, log_beta, sums):
            carry = carry_ref[p]
            att = jnp.exp(log_beta + (sums[:, :BLK] + carry))
            if diag:
                att = jnp.where(causal2, att, 0.0)
            carry_ref[p] = carry + sums[:, BLK:]
            att = att.astype(BF16)
            vb = v_ref[rows(j), cols(p)]
            zero = jnp.zeros_like(vb)
            v2 = jnp.concatenate([jnp.where(lo_lanes, vb, zero), jnp.where(lo_lanes, zero, vb)], axis=0)
            return _bdot(jnp.concatenate([att[:BLK], att[BLK:]], axis=1), v2)

        chains = [(j, p, diag_first and n == 0) for n, j in enumerate(js) for p in range(npair)]
        z, ks, pv = {}, {}, {}
        lag = STICK_STAGE_LAG
        for step in range(len(chains) + 3 * lag):
            if step < len(chains):
                z[step] = scores(*chains[step][:2])
            if 0 <= step - lag < len(chains):
                ks[step - lag] = keep_sums(z.pop(step - lag), chains[step - lag][2])
            if 0 <= step - 2 * lag < len(chains):
                pv[step - 2 * lag] = weighted_values(*chains[step - 2 * lag], *ks.pop(step - 2 * lag))
            if 0 <= step - 3 * lag < len(chains):
                acc_ref[chains[step - 3 * lag][1]] += pv.pop(step - 3 * lag)

    @pl.when(i >= 2)
    def _():
        blocks([i, i - 1, i - 2], True)

    @pl.when(i == 1)
    def _():
        blocks([i, i - 1], True)

    @pl.when(i == 0)
    def _():
        blocks([i], True)

    rest = i - jnp.minimum(i, 2)

    def live():
        c = carry_ref[...]
        worst = jnp.max(jnp.max(c, axis=0), axis=0, keepdims=True)
        return (worst[0, 0] >= EXP_IS_ZERO_BELOW).astype(jnp.int32)

    def two_blocks(state):
        t, _ = state
        j = rest - 1 - 2 * t
        blocks([j, j - 1], False)
        return t + 1, live()

    _, alive = lax.while_loop(lambda st: (st[0] < rest // 2) & (st[1] == 1), two_blocks,
                              (jnp.int32(0), live()))

    @pl.when((rest % 2 == 1) & (alive == 1))
    def _():
        blocks([0], False)

    for p in range(npair):
        o_ref[:, p * PAIR:(p + 1) * PAIR] = acc_ref[p].astype(o_ref.dtype)


def _suffix_sum_matrix():
    sp = np.arange(BLK)[:, None]
    sc = np.arange(BLK)[None, :]
    return jnp.asarray(
        np.concatenate([(sp > sc).astype(np.float32), np.ones((BLK, BLK), np.float32)], axis=1), BF16)


def _stick_breaking(proj):
    b, s, n = proj.shape
    npair = WIDTH // PAIR
    return pl.pallas_call(
        _stick_kernel,
        grid=(b, s // BLK),
        in_specs=[
            pl.BlockSpec((None, BLK, WIDTH), lambda i, t: (i, t, 2)),
            pl.BlockSpec((None, s, WIDTH), lambda i, t: (i, 0, 3)),
            pl.BlockSpec((None, s, WIDTH), lambda i, t: (i, 0, 4)),
            pl.BlockSpec((BLK, 2 * BLK), lambda i, t: (0, 0)),
        ],
        out_specs=pl.BlockSpec((None, BLK, WIDTH), lambda i, t: (i, t, 0)),
        out_shape=jax.ShapeDtypeStruct((b, s, WIDTH), BF16),
        scratch_shapes=[pltpu.VMEM((npair, 2 * BLK, PAIR), BF16),
                        pltpu.VMEM((npair, 2 * BLK, BLK), F32),
                        pltpu.VMEM((npair, BLK, PAIR), F32)],
        compiler_params=_params(("arbitrary", "arbitrary")),
        name="stick_breaking",
    )(proj, proj, proj, _suffix_sum_matrix())


def _lane_min_index(mask, lane):
    return jnp.min(jnp.where(mask, lane, float(LANES)), -1, keepdims=True)


def _out_kernel(l_ref, r_ref, w_ref, x_ref, mod_ref, g_ref, rh_ref, rl_ref, rb_ref, tri_ref,
                xo_ref, h_ref, cls_ref, rank_ref, cnt_ref, seen):
    tm = x_ref.shape[0]

    @pl.when((pl.program_id(0) == 0) & (pl.program_id(1) == 0))
    def _():
        seen[...] = jnp.zeros_like(seen)

    m = mod_ref[...]
    mix = _bdot(l_ref[...], w_ref[0:WIDTH, :]) + _bdot(r_ref[...], w_ref[WIDTH:, :])
    x = x_ref[...] + m[2:3] * mix
    xo_ref[...] = x
    h = _rms_mod(x, g_ref[...], m[4:5], m[3:4])
    h_ref[...] = h
    logits = _dot3(h, rh_ref[...], rl_ref[...]) + rb_ref[...]
    lane = lax.broadcasted_iota(jnp.int32, logits.shape, 1).astype(F32)
    is_group = (lane >= N_EXPERTS) & (lane < N_EXPERTS + N_GROUPS)
    glog = jnp.where(is_group, logits, -jnp.inf)
    gmax = jnp.max(glog, -1, keepdims=True)
    g_idx = _lane_min_index(glog == gmax, lane) - N_EXPERTS
    in_group = (lane >= g_idx * EXPERTS_PER_GROUP) & (lane < (g_idx + 1) * EXPERTS_PER_GROUP)
    ev = jnp.where(in_group, logits, -jnp.inf)
    v1 = jnp.max(ev, -1, keepdims=True)
    i1 = _lane_min_index(ev == v1, lane)
    ev2 = jnp.where(lane == i1, -jnp.inf, ev)
    v2 = jnp.max(ev2, -1, keepdims=True)
    i2 = _lane_min_index(ev2 == v2, lane)
    lo = jnp.minimum(i1, i2) - g_idx * EXPERTS_PER_GROUP
    hi = jnp.maximum(i1, i2) - g_idx * EXPERTS_PER_GROUP
    cls = g_idx * PAIRS_PER_GROUP + (lo * (7.0 - lo) * 0.5 + (hi - lo - 1.0))
    cls_t = jnp.transpose(jnp.broadcast_to(cls, (tm, LANES)))
    onehot_t = lax.broadcasted_iota(jnp.int32, (LANES, tm), 0).astype(F32) == cls_t
    ones_t = jnp.where(onehot_t, 1.0, 0.0)
    earlier = _bdot(ones_t.astype(BF16), tri_ref[...]) + seen[...]
    cls_ref[...] = cls_t[0:1, :].astype(jnp.int32)
    rank_ref[...] = jnp.sum(jnp.where(onehot_t, earlier, 0.0), 0, keepdims=True).astype(jnp.int32)
    seen[...] += jnp.sum(ones_t, 1, keepdims=True)
    cnt_ref[...] = seen[...]


def _out_proj(left, right, w_out, x, mod, g, r_hi, r_lo, r_b):
    b, s, d = x.shape
    tm = TOKEN_TILE
    nt = s // tm
    tile = lambda w: pl.BlockSpec((None, tm, w), lambda i, t: (i, t, 0))
    const = lambda shp: pl.BlockSpec(shp, lambda i, t: tuple(0 for _ in shp))
    row = pl.BlockSpec((None, 1, tm), lambda i, t: (i * nt + t, 0, 0))
    tri = np.arange(tm)[:, None] < np.arange(tm)[None, :]
    return pl.pallas_call(
        _out_kernel,
        grid=(b, nt),
        in_specs=[tile(WIDTH), tile(WIDTH), const((2 * WIDTH, d)), tile(d),
                  pl.BlockSpec((None, 6, d), lambda i, t: (i, 0, 0)),
                  const((1, d)), const((d, LANES)), const((d, LANES)), const((1, LANES)),
                  const((tm, tm))],
        out_specs=[tile(d), tile(d), row, row, const((LANES, 1))],
        out_shape=[jax.ShapeDtypeStruct((b, s, d), F32),
                   jax.ShapeDtypeStruct((b, s, d), F32),
                   jax.ShapeDtypeStruct((b * nt, 1, tm), jnp.int32),
                   jax.ShapeDtypeStruct((b * nt, 1, tm), jnp.int32),
                   jax.ShapeDtypeStruct((LANES, 1), F32)],
        scratch_shapes=[pltpu.VMEM((LANES, 1), F32)],
        compiler_params=_params(("arbitrary", "arbitrary")),
        name="out_proj_router",
    )(left, right, w_out, x, mod, g.reshape(1, d), r_hi, r_lo, r_b, jnp.asarray(tri, BF16))


def _router_weights(w_group, b_group, w_router, b_router):
    d = w_group.shape[0]
    w = jnp.concatenate([jnp.transpose(w_router, (1, 0, 2)).reshape(d, N_EXPERTS), w_group], axis=1)
    w = jnp.pad(w.astype(F32), ((0, 0), (0, LANES - w.shape[1])))
    bias = jnp.concatenate([b_router.reshape(N_EXPERTS), b_group]).astype(F32)
    bias = jnp.pad(bias, (0, LANES - bias.shape[0])).reshape(1, LANES)
    hi, lo = _split(w)
    return hi, lo, bias


def _to_token_tiles(x):
    n = x.shape[0]
    parts = [x[:, g * LANES:(g + 1) * LANES].reshape(n // SUBLANES, SUBLANES, LANES)
             for g in range(x.shape[1] // LANES)]
    return jnp.swapaxes(jnp.stack(parts, axis=1), 1, 2).reshape(n, len(parts), LANES)


def _from_token_tiles(v):
    n = v.shape[0]
    groups = v.shape[1]
    w = jnp.swapaxes(v.reshape(n // SUBLANES, SUBLANES, groups, LANES), 1, 2)
    return jnp.concatenate([w[:, g].reshape(n, LANES) for g in range(groups)], axis=-1)


def _row_dma_loop(n, row_copy):
    def body(i, carry):
        row_copy(2 * i).start(priority=0)
        row_copy(2 * i + 1).start(priority=1)
        return carry
    lax.fori_loop(0, n // 2, body, 0, unroll=4)


def _dispatch_kernel(pad_ref, dest_ref, h_ref, hs_ref, buf, sem):
    tm = h_ref.shape[0]

    @pl.when(pl.program_id(0) == 0)
    def _():
        buf[...] = jnp.zeros_like(buf)
        for c in range(pad_ref.shape[0]):
            fill = pltpu.make_async_copy(buf, hs_ref.at[pl.ds(pad_ref[c], tm)], sem)
            fill.start()
            fill.wait()

    buf[...] = _to_token_tiles(h_ref[...])
    _row_dma_loop(tm, lambda r: pltpu.make_async_copy(
        buf.at[r], hs_ref.at[dest_ref[0, r]], sem))
    pltpu.make_async_copy(buf, hs_ref.at[pl.ds(0, tm)], sem).wait()


def _dispatch(h, dest, pad_start, n_slots):
    b, s, d = h.shape
    tm = TOKEN_TILE
    assert tm == SLOT_TILE
    nt = s // tm
    return pl.pallas_call(
        _dispatch_kernel,
        grid_spec=pltpu.PrefetchScalarGridSpec(
            num_scalar_prefetch=1,
            grid=(b * nt,),
            in_specs=[pl.BlockSpec((None, 1, tm), lambda i, pad: (i, 0, 0), memory_space=pltpu.SMEM),
                      pl.BlockSpec((None, tm, d), lambda i, pad: (i // nt, i % nt, 0))],
            out_specs=pl.BlockSpec(memory_space=pl.ANY),
            scratch_shapes=[pltpu.VMEM((tm, ROW_TILE, LANES), F32), pltpu.SemaphoreType.DMA(())]),
        out_shape=jax.ShapeDtypeStruct((n_slots, ROW_TILE, LANES), F32),
        compiler_params=_params(("arbitrary",)),
        name="moe_dispatch",
    )(pad_start, dest, h)


def _moe_kernel(blk_ref, ea_ref, eb_ref, grp_ref, used_ref, hs_ref, wga_ref, wua_ref, wda_ref,
                wgb_ref, wub_ref, wdb_ref, rh_ref, rb_ref, o_ref):
    n = pl.program_id(0)

    @pl.when(used_ref[n] == 1)
    def _():
        h = _from_token_tiles(hs_ref[...]).astype(BF16)
        logits = _bdot(h, rh_ref[...]) + rb_ref[...]
        lane = lax.broadcasted_iota(jnp.int32, logits.shape, 1)
        pick = lambda idx: jnp.sum(jnp.where(lane == idx, logits, 0.0), -1, keepdims=True)
        is_group = (lane >= N_EXPERTS) & (lane < N_EXPERTS + N_GROUPS)
        g_w = 1.0 / jnp.sum(
            jnp.where(is_group, jnp.exp(logits - pick(N_EXPERTS + grp_ref[n])), 0.0),
            -1, keepdims=True)
        la = pick(ea_ref[n])
        lb = pick(eb_ref[n])

        def expert(wg_ref, wu_ref, wd_ref):
            a = _bdot(h, wg_ref[...])
            hid = (a * jax.nn.sigmoid(a)) * _bdot(h, wu_ref[...])
            return _bdot(hid.astype(BF16), wd_ref[...])

        y = (g_w * (1.0 / (1.0 + jnp.exp(lb - la)))) * expert(wga_ref, wua_ref, wda_ref)
        y = y + (g_w * (1.0 / (1.0 + jnp.exp(la - lb)))) * expert(wgb_ref, wub_ref, wdb_ref)
        o_ref[...] = _to_token_tiles(y)

    @pl.when(used_ref[n] == 0)
    def _():
        o_ref[...] = jnp.zeros_like(o_ref)


def _moe(hs, plan, w_gate, w_up, w_down, r_hi, r_b):
    n_slots = hs.shape[0]
    ne, d, de = w_gate.shape
    tm = SLOT_TILE
    slot = pl.BlockSpec((tm, ROW_TILE, LANES), lambda n, *_: (n, 0, 0))
    slot_in = pl.BlockSpec((tm, ROW_TILE, LANES), lambda n, blk, *_: (blk[n], 0, 0))
    first = lambda shp: pl.BlockSpec((None,) + shp, lambda n, blk, ea, eb, grp, used: (ea[n], 0, 0))
    second = lambda shp: pl.BlockSpec((None,) + shp, lambda n, blk, ea, eb, grp, used: (eb[n], 0, 0))
    const = lambda shp: pl.BlockSpec(shp, lambda n, *_: tuple(0 for _ in shp))
    return pl.pallas_call(
        _moe_kernel,
        grid_spec=pltpu.PrefetchScalarGridSpec(
            num_scalar_prefetch=5,
            grid=(n_slots // tm,),
            in_specs=[slot_in, first((d, de)), first((d, de)), first((de, d)),
                      second((d, de)), second((d, de)), second((de, d)),
                      const((d, LANES)), const((1, LANES))],
            out_specs=slot),
        out_shape=jax.ShapeDtypeStruct((n_slots, ROW_TILE, LANES), F32),
        compiler_params=_params(("arbitrary",)),
        name="moe_experts",
    )(*plan, hs, w_gate, w_up, w_down, w_gate, w_up, w_down, r_hi, r_b)


def _combine_kernel(dest_ref, ys_ref, x_ref, mod_ref, fg_ref, o_ref, buf, sem):
    tm = x_ref.shape[0]
    _row_dma_loop(tm, lambda r: pltpu.make_async_copy(
        ys_ref.at[dest_ref[0, r]], buf.at[r], sem))
    pltpu.make_async_copy(ys_ref.at[pl.ds(0, tm)], buf, sem).wait()
    x = x_ref[...] + mod_ref[5:6, :] * _from_token_tiles(buf[...])
    o_ref[...] = x * lax.rsqrt(jnp.mean(x * x, -1, keepdims=True) + EPS) * fg_ref[...]


def _combine(ys, dest, x, mod, final_g):
    b, s, d = x.shape
    tm = TOKEN_TILE
    nt = s // tm
    return pl.pallas_call(
        _combine_kernel,
        grid=(b * nt,),
        in_specs=[pl.BlockSpec((None, 1, tm), lambda i: (i, 0, 0), memory_space=pltpu.SMEM),
                  pl.BlockSpec(memory_space=pl.ANY),
                  pl.BlockSpec((None, tm, d), lambda i: (i // nt, i % nt, 0)),
                  pl.BlockSpec((None, 6, d), lambda i: (i // nt, 0, 0)),
                  pl.BlockSpec((1, d), lambda i: (0, 0))],
        out_specs=pl.BlockSpec((None, tm, d), lambda i: (i // nt, i % nt, 0)),
        out_shape=jax.ShapeDtypeStruct((b, s, d), F32),
        scratch_shapes=[pltpu.VMEM((tm, ROW_TILE, LANES), F32), pltpu.SemaphoreType.DMA(())],
        compiler_params=_params(("arbitrary",)),
        name="moe_combine",
    )(dest, ys, x, mod, final_g.reshape(1, d))


def _combine_in_kernel(dest_ref, next_ref, ys_ref, x_ref, mod_ref, g_ref, nmod_ref, w_ref,
                       xo_ref, o_ref, *rest, dilations):
    *rest, buf, sem = rest
    tm = x_ref.shape[0]
    i = pl.program_id(0)
    slot = i % 2

    def gather(idx_ref, to):
        _row_dma_loop(tm, lambda r: pltpu.make_async_copy(
            ys_ref.at[idx_ref[0, r]], buf.at[to].at[r], sem.at[to]))

    @pl.when(i == 0)
    def _():
        gather(dest_ref, 0)

    @pl.when(i + 1 < pl.num_programs(0))
    def _():
        gather(next_ref, 1 - slot)

    pltpu.make_async_copy(ys_ref.at[pl.ds(0, tm)], buf.at[slot], sem.at[slot]).wait()
    x = x_ref[...] + mod_ref[5:6, :] * _from_token_tiles(buf[slot])
    xo_ref[...] = x
    _in_body(x, g_ref, nmod_ref, w_ref, o_ref, rest, dilations)


def _combine_in_proj(ys, dest, x, mod, g_next, mod_next, w_in, dilations=()):
    b, s, d = x.shape
    n = w_in.shape[1]
    tm = TOKEN_TILE
    nt = s // tm
    last = b * nt - 1
    tile = lambda i: (i // nt, i % nt, 0)
    specs, shapes, scratch = _in_proj_outputs(b, s, n, tm, dilations, tile)
    outs = pl.pallas_call(
        functools.partial(_combine_in_kernel, dilations=tuple(dilations)),
        grid=(b * nt,),
        in_specs=[pl.BlockSpec((None, 1, tm), lambda i: (i, 0, 0), memory_space=pltpu.SMEM),
                  pl.BlockSpec((None, 1, tm), lambda i: (jnp.minimum(i + 1, last), 0, 0),
                               memory_space=pltpu.SMEM),
                  pl.BlockSpec(memory_space=pl.ANY),
                  pl.BlockSpec((None, tm, d), tile),
                  pl.BlockSpec((None, 6, d), lambda i: (i // nt, 0, 0)),
                  pl.BlockSpec((1, d), lambda i: (0, 0)),
                  pl.BlockSpec((None, 6, d), lambda i: (i // nt, 0, 0)),
                  pl.BlockSpec((d, n), lambda i: (0, 0))],
        out_specs=[pl.BlockSpec((None, tm, d), tile)] + specs,
        out_shape=[jax.ShapeDtypeStruct((b, s, d), F32)] + shapes,
        scratch_shapes=scratch + [pltpu.VMEM((2, tm, ROW_TILE, LANES), F32), pltpu.SemaphoreType.DMA((2,))],
        compiler_params=_params(("arbitrary",)),
        name="moe_combine_in_proj",
    )(dest, dest, ys, x, mod, g_next.reshape(1, d), mod_next, w_in)
    return outs


def _routing_plan(cls, rank, counts, n_tokens):
    tm = SLOT_TILE
    n_tiles = n_tokens // tm + N_CLASSES
    counts = counts[:N_CLASSES, 0].astype(jnp.int32)
    tiles = (counts + tm - 1) // tm
    tile_end = jnp.cumsum(tiles)
    classes = jnp.arange(N_CLASSES, dtype=jnp.int32)
    first_slot = (tile_end - tiles) * tm
    dest = jnp.sum(jnp.where(cls[..., None] == classes, first_slot, 0), -1) + rank
    ids = jnp.arange(n_tiles, dtype=jnp.int32)
    used = ids < tile_end[-1]
    tile_cls = jnp.sum(
        (jnp.minimum(ids, tile_end[-1] - 1)[:, None] >= tile_end).astype(jnp.int32), -1)
    grp = tile_cls // PAIRS_PER_GROUP
    pair = tile_cls % PAIRS_PER_GROUP
    pair_a = jnp.asarray([0, 0, 0, 1, 1, 2], jnp.int32)
    pair_b = jnp.asarray([1, 2, 3, 2, 3, 3], jnp.int32)
    eb = grp * EXPERTS_PER_GROUP + pair_b[pair]
    ea = jnp.where(used, grp * EXPERTS_PER_GROUP + pair_a[pair], eb)
    blk = jnp.where(used, ids, 0)
    tail = jnp.minimum(tile_end[-1] * tm + classes * tm, (n_tiles - 1) * tm)
    zero_blocks = jnp.concatenate([first_slot + counts, tail])
    return dest, zero_blocks, (blk, ea, eb, grp, used.astype(jnp.int32))


@jax.jit
def kernel(x, c, rel_bias, norm1_g, norm2_g, w_ada, b_ada, ev_w_in, ev_w_out, ev_gmlp_ln_g, ev_gmlp_ln_b, ev_w_s, ev_b_s, od_w_in, od_w_out, od_w_dw, od_b_dw, od_conv_ln_g, od_conv_ln_b, moe_w_group, moe_b_group, moe_w_router, moe_b_router, moe_w_gate, moe_w_up, moe_w_down, final_norm_g):
    b, s, d = x.shape
    depth = w_ada.shape[0]
    mods = _ada(c, w_ada, b_ada).reshape(depth, b, 6, d)
    biases = [_branch_bias(rel_bias, dil) for _, dil in DILATED_BRANCHES]
    dils = [dil for _, dil in DILATED_BRANCHES]
    grouped_dils = [dil for dil in dils if dil > 1]

    def in_weights(layer):
        even = layer % 2 == 0
        return (ev_w_in if even else od_w_in)[layer // 2].astype(BF16), grouped_dils if even else []

    w_in, extra = in_weights(0)
    proj, *grouped = _in_proj(x, norm1_g[0], mods[0], w_in, extra)
    for layer in range(depth):
        j = layer // 2
        mod = mods[layer]
        if layer % 2 == 0:
            views = {1: (proj, proj.shape[-1] // WIDTH)}
            views.update({dil: (g, 3) for dil, g in zip(grouped_dils, grouped)})
            outs = [_dilated_branch(views[dil][0], bias, dil, views[dil][1])
                    for bias, dil in zip(biases, dils)]
            left = _combine_branches(outs, dils, s)
            right = _gmlp(proj, ev_gmlp_ln_g[j], ev_gmlp_ln_b[j], ev_w_s[j], ev_b_s[j])
            w_out = ev_w_out[j]
        else:
            left = _conv_module(proj, od_w_dw[j], od_b_dw[j], od_conv_ln_g[j], od_conv_ln_b[j])
            right = _stick_breaking(proj)
            w_out = od_w_out[j]
        r_hi, r_lo, r_b = _router_weights(moe_w_group[layer], moe_b_group[layer],
                                          moe_w_router[layer], moe_b_router[layer])
        x, h, cls, rank, counts = _out_proj(left, right, w_out.astype(BF16), x, mod,
                                            norm2_g[layer], r_hi, r_lo, r_b)
        dest, pad_start, plan = _routing_plan(cls, rank, counts, b * s)
        hs = _dispatch(h, dest, pad_start, (b * s // SLOT_TILE + N_CLASSES) * SLOT_TILE)
        ys = _moe(hs, plan, moe_w_gate[layer].astype(BF16), moe_w_up[layer].astype(BF16),
                  moe_w_down[layer].astype(BF16), r_hi, r_b)
        if layer == depth - 1:
            x = _combine(ys, dest, x, mod, final_norm_g)
        else:
            w_in, extra = in_weights(layer + 1)
            x, proj, *grouped = _combine_in_proj(ys, dest, x, mod, norm1_g[layer + 1],
                                                 mods[layer + 1], w_in, extra)
    return x
```

```python
import functools
import math

import numpy as np
import jax
import jax.numpy as jnp
from jax import lax
from jax.experimental import pallas as pl
from jax.experimental.pallas import tpu as pltpu

F32 = jnp.float32
BF16 = jnp.bfloat16

HEAD_DIM = 64
HEADS = 8
WIDTH = HEADS * HEAD_DIM
PAIR = 2 * HEAD_DIM
BLK = 128
DILATED_BRANCHES = ((128, 1), (512, 4), (2048, 16))
N_BUCKETS = 32
MAX_DISTANCE = 2048
CONV_WIDTH = 31
CONV_HALO = 32
CONV_TAP_UNROLL = 8
STICK_STAGE_LAG = 2
DILATED_STAGE_LAG = 1
DILATED_QUERY_ROWS = 128
B_GROUPS = 4
N_GROUPS = 4
EXPERTS_PER_GROUP = 4
N_EXPERTS = N_GROUPS * EXPERTS_PER_GROUP
PAIRS_PER_GROUP = EXPERTS_PER_GROUP * (EXPERTS_PER_GROUP - 1) // 2
N_CLASSES = N_GROUPS * PAIRS_PER_GROUP
TOKEN_TILE = 512
SLOT_TILE = 512
EPS = 1e-6
NEG = -1e30
LOG2E = 1.4426950408889634
EXP_IS_ZERO_BELOW = -104.0
LANES = 128
SUBLANES = 8
ROW_TILE = 8
ADA_COL_TILE = 1024
VMEM_LIMIT = 48 * 1024 * 1024


def _params(sem):
    return pltpu.CompilerParams(dimension_semantics=sem, vmem_limit_bytes=VMEM_LIMIT)


def _bdot(a, b):
    return jnp.dot(a, b, preferred_element_type=F32)


def _bdot_nt(a, b):
    return lax.dot_general(a, b, (((1,), (1,)), ((), ())), preferred_element_type=F32)


def _split(a):
    hi = a.astype(BF16)
    lo = (a - hi.astype(F32)).astype(BF16)
    return hi, lo


def _dot3(a, w_hi, w_lo):
    a_hi, a_lo = _split(a)
    return _bdot(a_hi, w_hi) + (_bdot(a_hi, w_lo) + _bdot(a_lo, w_hi))


def _ada_kernel(c_ref, w_ref, b_ref, o_ref):
    c = c_ref[...]
    sc = c * jax.nn.sigmoid(c)
    w_hi, w_lo = _split(w_ref[...])
    o_ref[...] = _dot3(sc, w_hi, w_lo) + b_ref[...]


def _ada(c, w_ada, b_ada):
    depth, d, n = w_ada.shape
    b = c.shape[0]
    tn = ADA_COL_TILE
    return pl.pallas_call(
        _ada_kernel,
        grid=(depth, n // tn),
        in_specs=[
            pl.BlockSpec((b, d), lambda l, j: (0, 0)),
            pl.BlockSpec((None, d, tn), lambda l, j: (l, 0, j)),
            pl.BlockSpec((None, 1, tn), lambda l, j: (l, 0, j)),
        ],
        out_specs=pl.BlockSpec((None, b, tn), lambda l, j: (l, 0, j)),
        out_shape=jax.ShapeDtypeStruct((depth, b, n), F32),
        compiler_params=_params(("arbitrary", "arbitrary")),
        name="ada_mod",
    )(c, w_ada, b_ada.reshape(depth, 1, n))


def _rms_mod(x, g, scale_row, shift_row):
    y = x * lax.rsqrt(jnp.mean(x * x, -1, keepdims=True) + EPS) * g
    return y * (1.0 + scale_row) + shift_row


def _in_body(x, g_ref, mod_ref, w_ref, o_ref, rest, dilations):
    m = mod_ref[...]
    h = _rms_mod(x, g_ref[...], m[1:2], m[0:1]).astype(BF16)
    tm = x.shape[0]
    n = w_ref.shape[1]
    per = WIDTH // LANES
    for j in range(n // WIDTH):
        cols = slice(j * WIDTH, (j + 1) * WIDTH)
        y = _bdot(h, w_ref[:, cols])
        o_ref[:, cols] = y.astype(o_ref.dtype)
        if dilations and j < 3:
            for c in range(per):
                rest[-1][j * per + c] = y[:, c * LANES:(c + 1) * LANES]
    for d, out in zip(dilations, rest):
        for r in range(d):
            for c in range(3 * per):
                lo = (r * 3 * per + c) * LANES
                out[:, lo:lo + LANES] = rest[-1][c, pl.ds(r, tm // d, stride=d), :].astype(out.dtype)


def _in_kernel(x_ref, g_ref, mod_ref, w_ref, o_ref, *rest, dilations):
    _in_body(x_ref[...], g_ref, mod_ref, w_ref, o_ref, rest, dilations)


def _in_proj_outputs(b, s, n, tm, dilations, index):
    qkv = 3 * WIDTH
    specs = [pl.BlockSpec((None, tm, n), index)] + [
        pl.BlockSpec((None, tm // dil, dil * qkv), index) for dil in dilations]
    shapes = [jax.ShapeDtypeStruct((b, s, n), BF16)] + [
        jax.ShapeDtypeStruct((b, s // dil, dil * qkv), BF16) for dil in dilations]
    scratch = [pltpu.VMEM((qkv // LANES, tm, LANES), F32)] if dilations else []
    return specs, shapes, scratch


def _in_proj(x, g, mod, w_in, dilations=()):
    b, s, d = x.shape
    n = w_in.shape[1]
    tm = TOKEN_TILE
    specs, shapes, scratch = _in_proj_outputs(b, s, n, tm, dilations, lambda i, t: (i, t, 0))
    outs = pl.pallas_call(
        functools.partial(_in_kernel, dilations=tuple(dilations)),
        grid=(b, s // tm),
        in_specs=[
            pl.BlockSpec((None, tm, d), lambda i, t: (i, t, 0)),
            pl.BlockSpec((1, d), lambda i, t: (0, 0)),
            pl.BlockSpec((None, 6, d), lambda i, t: (i, 0, 0)),
            pl.BlockSpec((d, n), lambda i, t: (0, 0)),
        ],
        out_specs=specs,
        out_shape=shapes,
        scratch_shapes=scratch,
        compiler_params=_params(("arbitrary", "arbitrary")),
        name="in_proj",
    )(x, g.reshape(1, d), mod, w_in)
    return outs if dilations else outs[0]


def _t5_bucket(dist):
    max_exact = N_BUCKETS // 2
    d = np.maximum(dist, 1).astype(np.float32)
    large = max_exact + (np.log(d / max_exact) / np.log(MAX_DISTANCE / max_exact)
                         * (N_BUCKETS - max_exact)).astype(np.int32)
    large = np.minimum(large, N_BUCKETS - 1)
    return np.where(dist < max_exact, dist, large).astype(np.int32)


def _branch_bias(rel_table, dilation):
    qi = np.arange(BLK)[:, None]
    kj = np.arange(2 * BLK)[None, :]
    bucket = _t5_bucket(np.clip(qi + BLK - kj, 0, None) * dilation)
    onehot = np.eye(N_BUCKETS, dtype=np.float32)[bucket]
    bias = jnp.einsum("qkb,bh->hqk", onehot, rel_table.astype(F32), precision=lax.Precision.HIGHEST)
    dist = qi + BLK - kj
    in_window = (dist >= 0) & (dist <= BLK)
    return jnp.where(in_window[None], bias, NEG)


def _dil_kernel(q_ref, kp_ref, kc_ref, vp_ref, vc_ref, bias_ref, o_ref, m_ref, l_ref,
                kbuf, vbuf, *, tq):
    n = pl.program_id(2)
    nsub = tq // BLK
    kbuf[0:BLK] = kp_ref[...]
    kbuf[BLK:] = kc_ref[...]
    vbuf[0:BLK] = vp_ref[...]
    vbuf[BLK:] = vc_ref[...]
    qb = DILATED_QUERY_ROWS
    kj = lax.broadcasted_iota(jnp.int32, (qb, 2 * BLK), 1)
    lane = lax.broadcasted_iota(jnp.int32, (qb, PAIR), 1)
    scale = HEAD_DIM ** -0.5
    has_prev = kj >= jnp.where(n == 0, BLK, 0)
    rows = lambda u: slice(u * qb, (u + 1) * qb)
    krows = lambda u: slice(u * qb // BLK * BLK, (u * qb // BLK + 2) * BLK)
    brows = lambda u: slice(u * qb % BLK, u * qb % BLK + qb)
    cols = lambda p: slice(p * PAIR, (p + 1) * PAIR)

    def logits(i, p, hh):
        q_pair = q_ref[rows(i), cols(p)] * scale
        head_lanes = (lane >= HEAD_DIM) if hh else (lane < HEAD_DIM)
        qm = jnp.where(head_lanes, q_pair, jnp.zeros_like(q_pair))
        logit = _bdot_nt(qm, kbuf[krows(i), cols(p)]) + bias_ref[2 * p + hh, brows(i), :]
        return jnp.where(has_prev, logit, NEG) if i * qb < BLK else logit

    def softmax_values(i, p, logit):
        mx = jnp.max(logit, -1, keepdims=True)
        pr = jnp.exp(logit - mx)
        den = jnp.sum(pr, -1, keepdims=True)
        return mx, den, _bdot(pr.astype(BF16), vbuf[krows(i), cols(p)])

    chains = [(i, p, hh) for i in range(tq // qb) for p in range(HEADS // 2) for hh in range(2)]
    lag = DILATED_STAGE_LAG
    lg, sv, first_head = {}, {}, {}
    m_tile, l_tile = {}, {}
    for step in range(len(chains) + 2 * lag):
        if step < len(chains):
            lg[step] = logits(*chains[step])
        if 0 <= step - lag < len(chains):
            i, p, _ = chains[step - lag]
            sv[step - lag] = softmax_values(i, p, lg.pop(step - lag))
        if 0 <= step - 2 * lag < len(chains):
            i, p, hh = chains[step - 2 * lag]
            mx, den, pv = sv.pop(step - 2 * lag)
            h = 2 * p + hh
            m_tile[i] = jnp.where(lane == h, mx, m_tile.get(i, jnp.zeros((qb, LANES), F32)))
            l_tile[i] = jnp.where(lane == h, den, l_tile.get(i, jnp.ones((qb, LANES), F32)))
            if hh == 0:
                first_head[i, p] = pv / den
            else:
                o_ref[rows(i), cols(p)] = jnp.where(
                    lane < HEAD_DIM, first_head.pop((i, p)), pv / den).astype(o_ref.dtype)
            if h == HEADS - 1:
                m_ref[rows(i), :] = m_tile.pop(i)
                l_ref[rows(i), :] = l_tile.pop(i)


def _dilated_branch(view, bias, dilation, ncol):
    b, sub, _ = view.shape
    tq = min(TOKEN_TILE, sub)
    nsub = tq // BLK
    cur = lambda c: pl.BlockSpec((None, tq, WIDTH), lambda i, r, t: (i, t, r * ncol + c))
    prev = lambda c: pl.BlockSpec(
        (None, BLK, WIDTH), lambda i, r, t: (i, jnp.maximum(t * nsub - 1, 0), r * ncol + c))
    return pl.pallas_call(
        functools.partial(_dil_kernel, tq=tq),
        grid=(b, dilation, sub // tq),
        in_specs=[cur(0), prev(1), cur(1), prev(2), cur(2),
                  pl.BlockSpec((HEADS, BLK, 2 * BLK), lambda i, r, t: (0, 0, 0))],
        out_specs=[
            pl.BlockSpec((None, tq, WIDTH), lambda i, r, t: (i, t, r)),
            pl.BlockSpec((None, tq, LANES), lambda i, r, t: (i, t, r)),
            pl.BlockSpec((None, tq, LANES), lambda i, r, t: (i, t, r)),
        ],
        out_shape=[
            jax.ShapeDtypeStruct((b, sub, dilation * WIDTH), BF16),
            jax.ShapeDtypeStruct((b, sub, dilation * LANES), F32),
            jax.ShapeDtypeStruct((b, sub, dilation * LANES), F32),
        ],
        scratch_shapes=[pltpu.VMEM((BLK + tq, WIDTH), BF16), pltpu.VMEM((BLK + tq, WIDTH), BF16)],
        compiler_params=_params(("arbitrary", "arbitrary", "arbitrary")),
        name=f"dilated_d{dilation}",
    )(view, view, view, view, view, bias)


def _comb_body(o_refs, m_refs, l_refs, e_ref, out_ref, o_scr, s_scr, dilations):
    tm = out_ref.shape[0]
    per = WIDTH // LANES
    os, ms, ls = [], [], []
    for bi, d in enumerate(dilations):
        if d == 1:
            os.append([o_refs[bi][:, c * LANES:(c + 1) * LANES].astype(F32) for c in range(per)])
            ms.append(m_refs[bi][...])
            ls.append(l_refs[bi][...])
            continue
        rows = lambda r: pl.ds(r, tm // d, stride=d)
        for r in range(d):
            for c in range(per):
                lo = (r * per + c) * LANES
                o_scr[bi, c, rows(r), :] = o_refs[bi][:, lo:lo + LANES].astype(F32)
            s_scr[bi, 0, rows(r), :] = m_refs[bi][:, r * LANES:(r + 1) * LANES]
            s_scr[bi, 1, rows(r), :] = l_refs[bi][:, r * LANES:(r + 1) * LANES]
        os.append([o_scr[bi, c] for c in range(per)])
        ms.append(s_scr[bi, 0])
        ls.append(s_scr[bi, 1])
    m_all = functools.reduce(jnp.maximum, ms)
    es = [l * jnp.exp(m - m_all) for l, m in zip(ls, ms)]
    tot = functools.reduce(lambda x, y: x + y, es)
    acc = [None] * per
    for e, o in zip(es, os):
        w_hi, w_lo = _split(e / tot)
        w = _bdot(w_hi, e_ref[...]) + _bdot(w_lo, e_ref[...])
        for c in range(per):
            term = w[:, c * LANES:(c + 1) * LANES] * o[c]
            acc[c] = term if acc[c] is None else acc[c] + term
    for c in range(per):
        out_ref[:, c * LANES:(c + 1) * LANES] = acc[c].astype(out_ref.dtype)


def _head_expander():
    expand = np.zeros((LANES, WIDTH), np.float32)
    for h in range(HEADS):
        expand[h, h * HEAD_DIM:(h + 1) * HEAD_DIM] = 1.0
    return jnp.asarray(expand, BF16)


def _gelu(x):
    return 0.5 * x * (1.0 + jnp.tanh(math.sqrt(2.0 / math.pi) * (x + 0.044715 * (x * x * x))))


def _gmlp_body(u_ref, v_ref, g_ref, b_ref, ws_ref, bs_ref, o_ref):
    tm = u_ref.shape[0]
    u = _gelu(u_ref[...].astype(F32))
    v = _gelu(v_ref[...].astype(F32))
    mu = jnp.mean(v, -1, keepdims=True)
    vc = v - mu
    var = jnp.mean(vc * vc, -1, keepdims=True)
    v = (vc * lax.rsqrt(var + EPS) * g_ref[...] + b_ref[...]).astype(BF16)
    r = lax.broadcasted_iota(jnp.int32, (BLK, BLK), 0)
    c = lax.broadcasted_iota(jnp.int32, (BLK, BLK), 1)
    for g in range(B_GROUPS):
        w = jnp.where(c <= r, ws_ref[g], 0.0).astype(BF16)
        cols = slice(g * BLK, (g + 1) * BLK)
        for ch in range(tm // BLK):
            rows = slice(ch * BLK, (ch + 1) * BLK)
            mixed = _bdot(w, v[rows, cols]) + bs_ref[g]
            o_ref[rows, cols] = (u[rows, cols] * mixed).astype(o_ref.dtype)


def _conv_kernel(a_ref, g_ref, ah_ref, gh_ref, w_ref, bdw_ref, lng_ref, lnb_ref, o_ref,
                 hbuf, wbuf, ybuf):
    tm = a_ref.shape[0]
    t = pl.program_id(1)
    halo = ah_ref[...].astype(F32) * jax.nn.sigmoid(gh_ref[...].astype(F32))
    hbuf[0, 0:CONV_HALO] = jnp.where(t == 0, 0.0, halo)
    hbuf[0, CONV_HALO:] = a_ref[...].astype(F32) * jax.nn.sigmoid(g_ref[...].astype(F32))
    n_shifted = tm + CONV_HALO - SUBLANES
    for s in range(1, SUBLANES):
        hbuf[s, 0:n_shifted] = hbuf[0, pl.ds(s, n_shifted), :]
    for k in range(CONV_WIDTH):
        wbuf[k] = jnp.broadcast_to(w_ref[k:k + 1, :], (SUBLANES, WIDTH))
    rc = 32
    off = CONV_HALO - (CONV_WIDTH - 1)

    def chunk(ci, carry):
        base = pl.multiple_of(ci * rc, rc)
        def tap(k, acc):
            q = off + k
            s = q & (SUBLANES - 1)
            start = pl.multiple_of(base + (q - s), SUBLANES)
            win = hbuf[s, pl.ds(start, rc), :].reshape(rc // SUBLANES, SUBLANES, WIDTH)
            return acc + (wbuf[k] * win).reshape(rc, WIDTH)

        acc = lax.fori_loop(0, CONV_WIDTH, tap, jnp.zeros((rc, WIDTH), F32) + bdw_ref[...],
                            unroll=CONV_TAP_UNROLL)
        ybuf[pl.ds(base, rc), :] = acc
        return carry

    lax.fori_loop(0, tm // rc, chunk, 0)
    acc = ybuf[...]
    mu = jnp.mean(acc, -1, keepdims=True)
    xc = acc - mu
    var = jnp.mean(xc * xc, -1, keepdims=True)
    y = xc * lax.rsqrt(var + EPS) * lng_ref[...] + lnb_ref[...]
    o_ref[...] = (y * jax.nn.sigmoid(y)).astype(o_ref.dtype)


def _conv_module(proj, w_dw, b_dw, ln_g, ln_b):
    b, s, n = proj.shape
    tm = TOKEN_TILE
    hb = tm // CONV_HALO
    row = lambda v: v.reshape(1, WIDTH)
    halo = lambda c: pl.BlockSpec(
        (None, CONV_HALO, WIDTH), lambda i, t: (i, jnp.maximum(t * hb - 1, 0), c))
    return pl.pallas_call(
        _conv_kernel,
        grid=(b, s // tm),
        in_specs=[
            pl.BlockSpec((None, tm, WIDTH), lambda i, t: (i, t, 0)),
            pl.BlockSpec((None, tm, WIDTH), lambda i, t: (i, t, 1)),
            halo(0), halo(1),
            pl.BlockSpec((CONV_WIDTH, WIDTH), lambda i, t: (0, 0)),
            pl.BlockSpec((1, WIDTH), lambda i, t: (0, 0)),
            pl.BlockSpec((1, WIDTH), lambda i, t: (0, 0)),
            pl.BlockSpec((1, WIDTH), lambda i, t: (0, 0)),
        ],
        out_specs=pl.BlockSpec((None, tm, WIDTH), lambda i, t: (i, t, 0)),
        out_shape=jax.ShapeDtypeStruct((b, s, WIDTH), BF16),
        scratch_shapes=[pltpu.VMEM((SUBLANES, CONV_HALO + tm, WIDTH), F32),
                        pltpu.VMEM((CONV_WIDTH, SUBLANES, WIDTH), F32),
                        pltpu.VMEM((tm, WIDTH), F32)],
        compiler_params=_params(("arbitrary", "arbitrary")),
        name="conv_module",
    )(proj, proj, proj, proj, w_dw, row(b_dw), row(ln_g), row(ln_b))


def _stick_kernel(q_ref, k_ref, v_ref, mm_ref, o_ref, q2_ref, carry_ref, acc_ref):
    i = pl.program_id(1)
    npair = WIDTH // PAIR
    lane = lax.broadcasted_iota(jnp.int32, (BLK, PAIR), 1)
    lo_lanes = lane < HEAD_DIM
    row2 = lax.broadcasted_iota(jnp.int32, (2 * BLK, BLK), 0)
    col2 = lax.broadcasted_iota(jnp.int32, (2 * BLK, BLK), 1)
    causal2 = col2 < (row2 & (BLK - 1))
    scale = HEAD_DIM ** -0.5
    for p in range(npair):
        qp = q_ref[:, p * PAIR:(p + 1) * PAIR] * scale
        zero = jnp.zeros_like(qp)
        q2_ref[p] = jnp.concatenate(
            [jnp.where(lo_lanes, qp, zero), jnp.where(lo_lanes, zero, qp)], axis=0)
    carry_ref[...] = jnp.zeros_like(carry_ref)
    acc_ref[...] = jnp.zeros_like(acc_ref)

    def blocks(js, diag_first):
        cols = lambda p: slice(p * PAIR, (p + 1) * PAIR)
        rows = lambda j: pl.ds(pl.multiple_of(j * BLK, BLK), BLK)

        def scores(j, p):
            return _bdot_nt(q2_ref[p], k_ref[rows(j), cols(p)])

        def keep_sums(z, diag):
            soft = jnp.log(1.0 + jnp.exp2(jnp.abs(z) * -LOG2E))
            log_beta = jnp.minimum(z, 0.0) - soft
            log_keep = log_beta - z
            if diag:
                log_keep = jnp.where(causal2, log_keep, 0.0)
            return log_beta, _bdot(log_keep.astype(BF16), mm_ref[...])

        def weighted_values(j, p, diag, log_beta, sums):
            carry = carry_ref[p]
            att = jnp.exp(log_beta + (sums[:, :BLK] + carry))
            if diag:
                att = jnp.where(causal2, att, 0.0)
            carry_ref[p] = carry + sums[:, BLK:]
            att = att.astype(BF16)
            vb = v_ref[rows(j), cols(p)]
            zero = jnp.zeros_like(vb)
            v2 = jnp.concatenate([jnp.where(lo_lanes, vb, zero), jnp.where(lo_lanes, zero, vb)], axis=0)
            return _bdot(jnp.concatenate([att[:BLK], att[BLK:]], axis=1), v2)

        chains = [(j, p, diag_first and n == 0) for n, j in enumerate(js) for p in range(npair)]
        z, ks, pv = {}, {}, {}
        lag = STICK_STAGE_LAG
        for step in range(len(chains) + 3 * lag):
            if step < len(chains):
                z[step] = scores(*chains[step][:2])
            if 0 <= step - lag < len(chains):
                ks[step - lag] = keep_sums(z.pop(step - lag), chains[step - lag][2])
            if 0 <= step - 2 * lag < len(chains):
                pv[step - 2 * lag] = weighted_values(*chains[step - 2 * lag], *ks.pop(step - 2 * lag))
            if 0 <= step - 3 * lag < len(chains):
                acc_ref[chains[step - 3 * lag][1]] += pv.pop(step - 3 * lag)

    @pl.when(i >= 2)
    def _():
        blocks([i, i - 1, i - 2], True)

    @pl.when(i == 1)
    def _():
        blocks([i, i - 1], True)

    @pl.when(i == 0)
    def _():
        blocks([i], True)

    rest = i - jnp.minimum(i, 2)

    def live():
        c = carry_ref[...]
        worst = jnp.max(jnp.max(c, axis=0), axis=0, keepdims=True)
        return (worst[0, 0] >= EXP_IS_ZERO_BELOW).astype(jnp.int32)

    def two_blocks(state):
        t, _ = state
        j = rest - 1 - 2 * t
        blocks([j, j - 1], False)
        return t + 1, live()

    _, alive = lax.while_loop(lambda st: (st[0] < rest // 2) & (st[1] == 1), two_blocks,
                              (jnp.int32(0), live()))

    @pl.when((rest % 2 == 1) & (alive == 1))
    def _():
        blocks([0], False)

    for p in range(npair):
        o_ref[:, p * PAIR:(p + 1) * PAIR] = acc_ref[p].astype(o_ref.dtype)


def _suffix_sum_matrix():
    sp = np.arange(BLK)[:, None]
    sc = np.arange(BLK)[None, :]
    return jnp.asarray(
        np.concatenate([(sp > sc).astype(np.float32), np.ones((BLK, BLK), np.float32)], axis=1), BF16)


def _stick_breaking(proj):
    b, s, n = proj.shape
    npair = WIDTH // PAIR
    return pl.pallas_call(
        _stick_kernel,
        grid=(b, s // BLK),
        in_specs=[
            pl.BlockSpec((None, BLK, WIDTH), lambda i, t: (i, t, 2)),
            pl.BlockSpec((None, s, WIDTH), lambda i, t: (i, 0, 3)),
            pl.BlockSpec((None, s, WIDTH), lambda i, t: (i, 0, 4)),
            pl.BlockSpec((BLK, 2 * BLK), lambda i, t: (0, 0)),
        ],
        out_specs=pl.BlockSpec((None, BLK, WIDTH), lambda i, t: (i, t, 0)),
        out_shape=jax.ShapeDtypeStruct((b, s, WIDTH), BF16),
        scratch_shapes=[pltpu.VMEM((npair, 2 * BLK, PAIR), BF16),
                        pltpu.VMEM((npair, 2 * BLK, BLK), F32),
                        pltpu.VMEM((npair, BLK, PAIR), F32)],
        compiler_params=_params(("arbitrary", "arbitrary")),
        name="stick_breaking",
    )(proj, proj, proj, _suffix_sum_matrix())


def _lane_min_index(mask, lane):
    return jnp.min(jnp.where(mask, lane, float(LANES)), -1, keepdims=True)


def _out_core(left, right, w_ref, x_ref, mod_ref, g_ref, rh_ref, rl_ref, rb_ref, tri_ref,
              xo_ref, h_ref, cls_ref, rank_ref, cnt_ref, seen):
    tm = x_ref.shape[0]

    @pl.when((pl.program_id(0) == 0) & (pl.program_id(1) == 0))
    def _():
        seen[...] = jnp.zeros_like(seen)

    m = mod_ref[...]
    mix = _bdot(left, w_ref[0:WIDTH, :]) + _bdot(right, w_ref[WIDTH:, :])
    x = x_ref[...] + m[2:3] * mix
    xo_ref[...] = x
    h = _rms_mod(x, g_ref[...], m[4:5], m[3:4])
    h_ref[...] = h
    logits = _dot3(h, rh_ref[...], rl_ref[...]) + rb_ref[...]
    lane = lax.broadcasted_iota(jnp.int32, logits.shape, 1).astype(F32)
    is_group = (lane >= N_EXPERTS) & (lane < N_EXPERTS + N_GROUPS)
    glog = jnp.where(is_group, logits, -jnp.inf)
    gmax = jnp.max(glog, -1, keepdims=True)
    g_idx = _lane_min_index(glog == gmax, lane) - N_EXPERTS
    in_group = (lane >= g_idx * EXPERTS_PER_GROUP) & (lane < (g_idx + 1) * EXPERTS_PER_GROUP)
    ev = jnp.where(in_group, logits, -jnp.inf)
    v1 = jnp.max(ev, -1, keepdims=True)
    i1 = _lane_min_index(ev == v1, lane)
    ev2 = jnp.where(lane == i1, -jnp.inf, ev)
    v2 = jnp.max(ev2, -1, keepdims=True)
    i2 = _lane_min_index(ev2 == v2, lane)
    lo = jnp.minimum(i1, i2) - g_idx * EXPERTS_PER_GROUP
    hi = jnp.maximum(i1, i2) - g_idx * EXPERTS_PER_GROUP
    cls = g_idx * PAIRS_PER_GROUP + (lo * (7.0 - lo) * 0.5 + (hi - lo - 1.0))
    cls_t = jnp.transpose(jnp.broadcast_to(cls, (tm, LANES)))
    onehot_t = lax.broadcasted_iota(jnp.int32, (LANES, tm), 0).astype(F32) == cls_t
    ones_t = jnp.where(onehot_t, 1.0, 0.0)
    earlier = _bdot(ones_t.astype(BF16), tri_ref[...]) + seen[...]
    cls_ref[...] = cls_t[0:1, :].astype(jnp.int32)
    rank_ref[...] = jnp.sum(jnp.where(onehot_t, earlier, 0.0), 0, keepdims=True).astype(jnp.int32)
    seen[...] += jnp.sum(ones_t, 1, keepdims=True)
    cnt_ref[...] = seen[...]


def _out_kernel(l_ref, r_ref, *rest):
    _out_core(l_ref[...], r_ref[...], *rest)


def _out_even_kernel(*refs, dilations):
    nb = len(dilations)
    o_refs, m_refs, l_refs = refs[:nb], refs[nb:2 * nb], refs[2 * nb:3 * nb]
    e_ref, u_ref, v_ref, lng_ref, lnb_ref, ws_ref, bs_ref = refs[3 * nb:3 * nb + 7]
    *rest, o_scr, s_scr, left, right = refs[3 * nb + 7:]
    _comb_body(o_refs, m_refs, l_refs, e_ref, left, o_scr, s_scr, dilations)
    _gmlp_body(u_ref, v_ref, lng_ref, lnb_ref, ws_ref, bs_ref, right)
    _out_core(left[...], right[...], *rest)


def _out_proj(mixers, w_out, x, mod, g, r_hi, r_lo, r_b):
    b, s, d = x.shape
    tm = TOKEN_TILE
    nt = s // tm
    tile = lambda w: pl.BlockSpec((None, tm, w), lambda i, t: (i, t, 0))
    const = lambda shp: pl.BlockSpec(shp, lambda i, t: tuple(0 for _ in shp))
    row = pl.BlockSpec((None, 1, tm), lambda i, t: (i * nt + t, 0, 0))
    tri = np.arange(tm)[:, None] < np.arange(tm)[None, :]
    scratch = [pltpu.VMEM((LANES, 1), F32)]
    if mixers[0] == "halves":
        kern = _out_kernel
        specs, args = [tile(WIDTH), tile(WIDTH)], list(mixers[1:])
    else:
        _, outs, dilations, proj, ln_g, ln_b, w_s, b_s = mixers
        nb = len(dilations)
        kern = functools.partial(_out_even_kernel, dilations=tuple(dilations))
        grouped = lambda w: [pl.BlockSpec((None, tm // dil, dil * w), lambda i, t: (i, t, 0))
                             for dil in dilations]
        specs = (grouped(WIDTH) + grouped(LANES) + grouped(LANES) + [const((LANES, WIDTH))]
                 + [pl.BlockSpec((None, tm, WIDTH), lambda i, t: (i, t, 3)),
                    pl.BlockSpec((None, tm, WIDTH), lambda i, t: (i, t, 4)),
                    const((1, WIDTH)), const((1, WIDTH)),
                    const((B_GROUPS, BLK, BLK)), const((B_GROUPS, BLK, 1))])
        args = ([o for o, _, _ in outs] + [m for _, m, _ in outs] + [l for _, _, l in outs]
                + [_head_expander(), proj, proj, ln_g.reshape(1, WIDTH), ln_b.reshape(1, WIDTH),
                   w_s, b_s.reshape(B_GROUPS, BLK, 1)])
        scratch += [pltpu.VMEM((nb, WIDTH // LANES, tm, LANES), F32),
                    pltpu.VMEM((nb, 2, tm, LANES), F32),
                    pltpu.VMEM((tm, WIDTH), BF16), pltpu.VMEM((tm, WIDTH), BF16)]
    return pl.pallas_call(
        kern,
        grid=(b, nt),
        in_specs=specs + [const((2 * WIDTH, d)), tile(d),
                          pl.BlockSpec((None, 6, d), lambda i, t: (i, 0, 0)),
                          const((1, d)), const((d, LANES)), const((d, LANES)), const((1, LANES)),
                          const((tm, tm))],
        out_specs=[tile(d), tile(d), row, row, const((LANES, 1))],
        out_shape=[jax.ShapeDtypeStruct((b, s, d), F32),
                   jax.ShapeDtypeStruct((b, s, d), F32),
                   jax.ShapeDtypeStruct((b * nt, 1, tm), jnp.int32),
                   jax.ShapeDtypeStruct((b * nt, 1, tm), jnp.int32),
                   jax.ShapeDtypeStruct((LANES, 1), F32)],
        scratch_shapes=scratch,
        compiler_params=_params(("arbitrary", "arbitrary")),
        name="out_proj_router",
    )(*args, w_out, x, mod, g.reshape(1, d), r_hi, r_lo, r_b, jnp.asarray(tri, BF16))


def _router_weights(w_group, b_group, w_router, b_router):
    d = w_group.shape[0]
    w = jnp.concatenate([jnp.transpose(w_router, (1, 0, 2)).reshape(d, N_EXPERTS), w_group], axis=1)
    w = jnp.pad(w.astype(F32), ((0, 0), (0, LANES - w.shape[1])))
    bias = jnp.concatenate([b_router.reshape(N_EXPERTS), b_group]).astype(F32)
    bias = jnp.pad(bias, (0, LANES - bias.shape[0])).reshape(1, LANES)
    hi, lo = _split(w)
    return hi, lo, bias


def _to_token_tiles(x):
    n = x.shape[0]
    parts = [x[:, g * LANES:(g + 1) * LANES].reshape(n // SUBLANES, SUBLANES, LANES)
             for g in range(x.shape[1] // LANES)]
    return jnp.swapaxes(jnp.stack(parts, axis=1), 1, 2).reshape(n, len(parts), LANES)


def _from_token_tiles(v):
    n = v.shape[0]
    groups = v.shape[1]
    w = jnp.swapaxes(v.reshape(n // SUBLANES, SUBLANES, groups, LANES), 1, 2)
    return jnp.concatenate([w[:, g].reshape(n, LANES) for g in range(groups)], axis=-1)


def _row_dma_loop(n, row_copy):
    def body(i, carry):
        row_copy(2 * i).start(priority=0)
        row_copy(2 * i + 1).start(priority=1)
        return carry
    lax.fori_loop(0, n // 2, body, 0, unroll=4)


def _dispatch_kernel(pad_ref, dest_ref, h_ref, hs_ref, buf, sem):
    tm = h_ref.shape[0]

    @pl.when(pl.program_id(0) == 0)
    def _():
        buf[...] = jnp.zeros_like(buf)
        for c in range(pad_ref.shape[0]):
            fill = pltpu.make_async_copy(buf, hs_ref.at[pl.ds(pad_ref[c], tm)], sem)
            fill.start()
            fill.wait()

    buf[...] = _to_token_tiles(h_ref[...])
    _row_dma_loop(tm, lambda r: pltpu.make_async_copy(
        buf.at[r], hs_ref.at[dest_ref[0, r]], sem))
    pltpu.make_async_copy(buf, hs_ref.at[pl.ds(0, tm)], sem).wait()


def _dispatch(h, dest, pad_start, n_slots):
    b, s, d = h.shape
    tm = TOKEN_TILE
    assert tm == SLOT_TILE
    nt = s // tm
    return pl.pallas_call(
        _dispatch_kernel,
        grid_spec=pltpu.PrefetchScalarGridSpec(
            num_scalar_prefetch=1,
            grid=(b * nt,),
            in_specs=[pl.BlockSpec((None, 1, tm), lambda i, pad: (i, 0, 0), memory_space=pltpu.SMEM),
                      pl.BlockSpec((None, tm, d), lambda i, pad: (i // nt, i % nt, 0))],
            out_specs=pl.BlockSpec(memory_space=pl.ANY),
            scratch_shapes=[pltpu.VMEM((tm, ROW_TILE, LANES), F32), pltpu.SemaphoreType.DMA(())]),
        out_shape=jax.ShapeDtypeStruct((n_slots, ROW_TILE, LANES), F32),
        compiler_params=_params(("arbitrary",)),
        name="moe_dispatch",
    )(pad_start, dest, h)


def _moe_kernel(blk_ref, ea_ref, eb_ref, grp_ref, used_ref, hs_ref, wga_ref, wua_ref, wda_ref,
                wgb_ref, wub_ref, wdb_ref, rh_ref, rb_ref, o_ref):
    n = pl.program_id(0)

    @pl.when(used_ref[n] == 1)
    def _():
        h = _from_token_tiles(hs_ref[...]).astype(BF16)
        logits = _bdot(h, rh_ref[...]) + rb_ref[...]
        lane = lax.broadcasted_iota(jnp.int32, logits.shape, 1)
        pick = lambda idx: jnp.sum(jnp.where(lane == idx, logits, 0.0), -1, keepdims=True)
        is_group = (lane >= N_EXPERTS) & (lane < N_EXPERTS + N_GROUPS)
        g_w = 1.0 / jnp.sum(
            jnp.where(is_group, jnp.exp(logits - pick(N_EXPERTS + grp_ref[n])), 0.0),
            -1, keepdims=True)
        la = pick(ea_ref[n])
        lb = pick(eb_ref[n])

        def expert(wg_ref, wu_ref, wd_ref):
            a = _bdot(h, wg_ref[...])
            hid = (a * jax.nn.sigmoid(a)) * _bdot(h, wu_ref[...])
            return _bdot(hid.astype(BF16), wd_ref[...])

        y = (g_w * (1.0 / (1.0 + jnp.exp(lb - la)))) * expert(wga_ref, wua_ref, wda_ref)
        y = y + (g_w * (1.0 / (1.0 + jnp.exp(la - lb)))) * expert(wgb_ref, wub_ref, wdb_ref)
        o_ref[...] = _to_token_tiles(y)

    @pl.when(used_ref[n] == 0)
    def _():
        o_ref[...] = jnp.zeros_like(o_ref)


def _moe(hs, plan, w_gate, w_up, w_down, r_hi, r_b):
    n_slots = hs.shape[0]
    ne, d, de = w_gate.shape
    tm = SLOT_TILE
    slot = pl.BlockSpec((tm, ROW_TILE, LANES), lambda n, *_: (n, 0, 0))
    slot_in = pl.BlockSpec((tm, ROW_TILE, LANES), lambda n, blk, *_: (blk[n], 0, 0))
    first = lambda shp: pl.BlockSpec((None,) + shp, lambda n, blk, ea, eb, grp, used: (ea[n], 0, 0))
    second = lambda shp: pl.BlockSpec((None,) + shp, lambda n, blk, ea, eb, grp, used: (eb[n], 0, 0))
    const = lambda shp: pl.BlockSpec(shp, lambda n, *_: tuple(0 for _ in shp))
    return pl.pallas_call(
        _moe_kernel,
        grid_spec=pltpu.PrefetchScalarGridSpec(
            num_scalar_prefetch=5,
            grid=(n_slots // tm,),
            in_specs=[slot_in, first((d, de)), first((d, de)), first((de, d)),
                      second((d, de)), second((d, de)), second((de, d)),
                      const((d, LANES)), const((1, LANES))],
            out_specs=slot),
        out_shape=jax.ShapeDtypeStruct((n_slots, ROW_TILE, LANES), F32),
        compiler_params=_params(("arbitrary",)),
        name="moe_experts",
    )(*plan, hs, w_gate, w_up, w_down, w_gate, w_up, w_down, r_hi, r_b)


def _combine_kernel(dest_ref, ys_ref, x_ref, mod_ref, fg_ref, o_ref, buf, sem):
    tm = x_ref.shape[0]
    _row_dma_loop(tm, lambda r: pltpu.make_async_copy(
        ys_ref.at[dest_ref[0, r]], buf.at[r], sem))
    pltpu.make_async_copy(ys_ref.at[pl.ds(0, tm)], buf, sem).wait()
    x = x_ref[...] + mod_ref[5:6, :] * _from_token_tiles(buf[...])
    o_ref[...] = x * lax.rsqrt(jnp.mean(x * x, -1, keepdims=True) + EPS) * fg_ref[...]


def _combine(ys, dest, x, mod, final_g):
    b, s, d = x.shape
    tm = TOKEN_TILE
    nt = s // tm
    return pl.pallas_call(
        _combine_kernel,
        grid=(b * nt,),
        in_specs=[pl.BlockSpec((None, 1, tm), lambda i: (i, 0, 0), memory_space=pltpu.SMEM),
                  pl.BlockSpec(memory_space=pl.ANY),
                  pl.BlockSpec((None, tm, d), lambda i: (i // nt, i % nt, 0)),
                  pl.BlockSpec((None, 6, d), lambda i: (i // nt, 0, 0)),
                  pl.BlockSpec((1, d), lambda i: (0, 0))],
        out_specs=pl.BlockSpec((None, tm, d), lambda i: (i // nt, i % nt, 0)),
        out_shape=jax.ShapeDtypeStruct((b, s, d), F32),
        scratch_shapes=[pltpu.VMEM((tm, ROW_TILE, LANES), F32), pltpu.SemaphoreType.DMA(())],
        compiler_params=_params(("arbitrary",)),
        name="moe_combine",
    )(dest, ys, x, mod, final_g.reshape(1, d))


def _combine_in_kernel(dest_ref, next_ref, ys_ref, x_ref, mod_ref, g_ref, nmod_ref, w_ref,
                       xo_ref, o_ref, *rest, dilations):
    *rest, buf, sem = rest
    tm = x_ref.shape[0]
    i = pl.program_id(0)
    slot = i % 2

    def gather(idx_ref, to):
        _row_dma_loop(tm, lambda r: pltpu.make_async_copy(
            ys_ref.at[idx_ref[0, r]], buf.at[to].at[r], sem.at[to]))

    @pl.when(i == 0)
    def _():
        gather(dest_ref, 0)

    @pl.when(i + 1 < pl.num_programs(0))
    def _():
        gather(next_ref, 1 - slot)

    pltpu.make_async_copy(ys_ref.at[pl.ds(0, tm)], buf.at[slot], sem.at[slot]).wait()
    x = x_ref[...] + mod_ref[5:6, :] * _from_token_tiles(buf[slot])
    xo_ref[...] = x
    _in_body(x, g_ref, nmod_ref, w_ref, o_ref, rest, dilations)


def _combine_in_proj(ys, dest, x, mod, g_next, mod_next, w_in, dilations=()):
    b, s, d = x.shape
    n = w_in.shape[1]
    tm = TOKEN_TILE
    nt = s // tm
    last = b * nt - 1
    tile = lambda i: (i // nt, i % nt, 0)
    specs, shapes, scratch = _in_proj_outputs(b, s, n, tm, dilations, tile)
    outs = pl.pallas_call(
        functools.partial(_combine_in_kernel, dilations=tuple(dilations)),
        grid=(b * nt,),
        in_specs=[pl.BlockSpec((None, 1, tm), lambda i: (i, 0, 0), memory_space=pltpu.SMEM),
                  pl.BlockSpec((None, 1, tm), lambda i: (jnp.minimum(i + 1, last), 0, 0),
                               memory_space=pltpu.SMEM),
                  pl.BlockSpec(memory_space=pl.ANY),
                  pl.BlockSpec((None, tm, d), tile),
                  pl.BlockSpec((None, 6, d), lambda i: (i // nt, 0, 0)),
                  pl.BlockSpec((1, d), lambda i: (0, 0)),
                  pl.BlockSpec((None, 6, d), lambda i: (i // nt, 0, 0)),
                  pl.BlockSpec((d, n), lambda i: (0, 0))],
        out_specs=[pl.BlockSpec((None, tm, d), tile)] + specs,
        out_shape=[jax.ShapeDtypeStruct((b, s, d), F32)] + shapes,
        scratch_shapes=scratch + [pltpu.VMEM((2, tm, ROW_TILE, LANES), F32), pltpu.SemaphoreType.DMA((2,))],
        compiler_params=_params(("arbitrary",)),
        name="moe_combine_in_proj",
    )(dest, dest, ys, x, mod, g_next.reshape(1, d), mod_next, w_in)
    return outs


def _routing_plan(cls, rank, counts, n_tokens):
    tm = SLOT_TILE
    n_tiles = n_tokens // tm + N_CLASSES
    counts = counts[:N_CLASSES, 0].astype(jnp.int32)
    tiles = (counts + tm - 1) // tm
    tile_end = jnp.cumsum(tiles)
    classes = jnp.arange(N_CLASSES, dtype=jnp.int32)
    first_slot = (tile_end - tiles) * tm
    dest = jnp.sum(jnp.where(cls[..., None] == classes, first_slot, 0), -1) + rank
    ids = jnp.arange(n_tiles, dtype=jnp.int32)
    used = ids < tile_end[-1]
    tile_cls = jnp.sum(
        (jnp.minimum(ids, tile_end[-1] - 1)[:, None] >= tile_end).astype(jnp.int32), -1)
    grp = tile_cls // PAIRS_PER_GROUP
    pair = tile_cls % PAIRS_PER_GROUP
    pair_a = jnp.asarray([0, 0, 0, 1, 1, 2], jnp.int32)
    pair_b = jnp.asarray([1, 2, 3, 2, 3, 3], jnp.int32)
    eb = grp * EXPERTS_PER_GROUP + pair_b[pair]
    ea = jnp.where(used, grp * EXPERTS_PER_GROUP + pair_a[pair], eb)
    blk = jnp.where(used, ids, 0)
    tail = jnp.minimum(tile_end[-1] * tm + classes * tm, (n_tiles - 1) * tm)
    zero_blocks = jnp.concatenate([first_slot + counts, tail])
    return dest, zero_blocks, (blk, ea, eb, grp, used.astype(jnp.int32))


@jax.jit
def kernel(x, c, rel_bias, norm1_g, norm2_g, w_ada, b_ada, ev_w_in, ev_w_out, ev_gmlp_ln_g, ev_gmlp_ln_b, ev_w_s, ev_b_s, od_w_in, od_w_out, od_w_dw, od_b_dw, od_conv_ln_g, od_conv_ln_b, moe_w_group, moe_b_group, moe_w_router, moe_b_router, moe_w_gate, moe_w_up, moe_w_down, final_norm_g):
    b, s, d = x.shape
    depth = w_ada.shape[0]
    mods = _ada(c, w_ada, b_ada).reshape(depth, b, 6, d)
    biases = [_branch_bias(rel_bias, dil) for _, dil in DILATED_BRANCHES]
    dils = [dil for _, dil in DILATED_BRANCHES]
    grouped_dils = [dil for dil in dils if dil > 1]

    def in_weights(layer):
        even = layer % 2 == 0
        return (ev_w_in if even else od_w_in)[layer // 2].astype(BF16), grouped_dils if even else []

    w_in, extra = in_weights(0)
    proj, *grouped = _in_proj(x, norm1_g[0], mods[0], w_in, extra)
    for layer in range(depth):
        j = layer // 2
        mod = mods[layer]
        if layer % 2 == 0:
            views = {1: (proj, proj.shape[-1] // WIDTH)}
            views.update({dil: (g, 3) for dil, g in zip(grouped_dils, grouped)})
            outs = [_dilated_branch(views[dil][0], bias, dil, views[dil][1])
                    for bias, dil in zip(biases, dils)]
            mixers = ("even", outs, dils, proj, ev_gmlp_ln_g[j], ev_gmlp_ln_b[j], ev_w_s[j], ev_b_s[j])
            w_out = ev_w_out[j]
        else:
            mixers = ("halves",
                      _conv_module(proj, od_w_dw[j], od_b_dw[j], od_conv_ln_g[j], od_conv_ln_b[j]),
                      _stick_breaking(proj))
            w_out = od_w_out[j]
        r_hi, r_lo, r_b = _router_weights(moe_w_group[layer], moe_b_group[layer],
                                          moe_w_router[layer], moe_b_router[layer])
        x, h, cls, rank, counts = _out_proj(mixers, w_out.astype(BF16), x, mod,
                                            norm2_g[layer], r_hi, r_lo, r_b)
        dest, pad_start, plan = _routing_plan(cls, rank, counts, b * s)
        hs = _dispatch(h, dest, pad_start, (b * s // SLOT_TILE + N_CLASSES) * SLOT_TILE)
        ys = _moe(hs, plan, moe_w_gate[layer].astype(BF16), moe_w_up[layer].astype(BF16),
                  moe_w_down[layer].astype(BF16), r_hi, r_b)
        if layer == depth - 1:
            x = _combine(ys, dest, x, mod, final_norm_g)
        else:
            w_in, extra = in_weights(layer + 1)
            x, proj, *grouped = _combine_in_proj(ys, dest, x, mod, norm1_g[layer + 1],
                                                 mods[layer + 1], w_in, extra)
    return x
```

```python
import functools
import math

import numpy as np
import jax
import jax.numpy as jnp
from jax import lax
from jax.experimental import pallas as pl
from jax.experimental.pallas import tpu as pltpu

F32 = jnp.float32
BF16 = jnp.bfloat16

HEAD_DIM = 64
HEADS = 8
WIDTH = HEADS * HEAD_DIM
PAIR = 2 * HEAD_DIM
BLK = 128
DILATED_BRANCHES = ((128, 1), (512, 4), (2048, 16))
N_BUCKETS = 32
MAX_DISTANCE = 2048
CONV_WIDTH = 31
CONV_HALO = 32
CONV_TAP_UNROLL = 8
STICK_STAGE_LAG = 2
DILATED_STAGE_LAG = 1
DILATED_QUERY_ROWS = 128
B_GROUPS = 4
N_GROUPS = 4
EXPERTS_PER_GROUP = 4
N_EXPERTS = N_GROUPS * EXPERTS_PER_GROUP
PAIRS_PER_GROUP = EXPERTS_PER_GROUP * (EXPERTS_PER_GROUP - 1) // 2
N_CLASSES = N_GROUPS * PAIRS_PER_GROUP
TOKEN_TILE = 512
SLOT_TILE = 512
EPS = 1e-6
NEG = -1e30
LOG2E = 1.4426950408889634
EXP_IS_ZERO_BELOW = -104.0
LANES = 128
SUBLANES = 8
ROW_TILE = 8
ADA_COL_TILE = 1024
VMEM_LIMIT = 48 * 1024 * 1024


def _params(sem):
    return pltpu.CompilerParams(dimension_semantics=sem, vmem_limit_bytes=VMEM_LIMIT)


def _bdot(a, b):
    return jnp.dot(a, b, preferred_element_type=F32)


def _bdot_nt(a, b):
    return lax.dot_general(a, b, (((1,), (1,)), ((), ())), preferred_element_type=F32)


def _split(a):
    hi = a.astype(BF16)
    lo = (a - hi.astype(F32)).astype(BF16)
    return hi, lo


def _dot3(a, w_hi, w_lo):
    a_hi, a_lo = _split(a)
    return _bdot(a_hi, w_hi) + (_bdot(a_hi, w_lo) + _bdot(a_lo, w_hi))


def _ada_kernel(c_ref, w_ref, b_ref, o_ref):
    c = c_ref[...]
    sc = c * jax.nn.sigmoid(c)
    w_hi, w_lo = _split(w_ref[...])
    o_ref[...] = _dot3(sc, w_hi, w_lo) + b_ref[...]


def _ada(c, w_ada, b_ada):
    depth, d, n = w_ada.shape
    b = c.shape[0]
    tn = ADA_COL_TILE
    return pl.pallas_call(
        _ada_kernel,
        grid=(depth, n // tn),
        in_specs=[
            pl.BlockSpec((b, d), lambda l, j: (0, 0)),
            pl.BlockSpec((None, d, tn), lambda l, j: (l, 0, j)),
            pl.BlockSpec((None, 1, tn), lambda l, j: (l, 0, j)),
        ],
        out_specs=pl.BlockSpec((None, b, tn), lambda l, j: (l, 0, j)),
        out_shape=jax.ShapeDtypeStruct((depth, b, n), F32),
        compiler_params=_params(("arbitrary", "arbitrary")),
        name="ada_mod",
    )(c, w_ada, b_ada.reshape(depth, 1, n))


def _rms_mod(x, g, scale_row, shift_row):
    y = x * lax.rsqrt(jnp.mean(x * x, -1, keepdims=True) + EPS) * g
    return y * (1.0 + scale_row) + shift_row


def _in_body(x, g_ref, mod_ref, w_ref, o_ref, rest, dilations):
    m = mod_ref[...]
    h = _rms_mod(x, g_ref[...], m[1:2], m[0:1]).astype(BF16)
    tm = x.shape[0]
    n = w_ref.shape[1]
    per = WIDTH // LANES
    for j in range(n // WIDTH):
        cols = slice(j * WIDTH, (j + 1) * WIDTH)
        y = _bdot(h, w_ref[:, cols])
        o_ref[:, cols] = y.astype(o_ref.dtype)
        if dilations and j < 3:
            for c in range(per):
                rest[-1][j * per + c] = y[:, c * LANES:(c + 1) * LANES]
    for d, out in zip(dilations, rest):
        for r in range(d):
            for c in range(3 * per):
                lo = (r * 3 * per + c) * LANES
                out[:, lo:lo + LANES] = rest[-1][c, pl.ds(r, tm // d, stride=d), :].astype(out.dtype)


def _in_kernel(x_ref, g_ref, mod_ref, w_ref, o_ref, *rest, dilations):
    _in_body(x_ref[...], g_ref, mod_ref, w_ref, o_ref, rest, dilations)


def _in_proj_outputs(b, s, n, tm, dilations, index):
    qkv = 3 * WIDTH
    specs = [pl.BlockSpec((None, tm, n), index)] + [
        pl.BlockSpec((None, tm // dil, dil * qkv), index) for dil in dilations]
    shapes = [jax.ShapeDtypeStruct((b, s, n), BF16)] + [
        jax.ShapeDtypeStruct((b, s // dil, dil * qkv), BF16) for dil in dilations]
    scratch = [pltpu.VMEM((qkv // LANES, tm, LANES), F32)] if dilations else []
    return specs, shapes, scratch


def _in_proj(x, g, mod, w_in, dilations=()):
    b, s, d = x.shape
    n = w_in.shape[1]
    tm = TOKEN_TILE
    specs, shapes, scratch = _in_proj_outputs(b, s, n, tm, dilations, lambda i, t: (i, t, 0))
    outs = pl.pallas_call(
        functools.partial(_in_kernel, dilations=tuple(dilations)),
        grid=(b, s // tm),
        in_specs=[
            pl.BlockSpec((None, tm, d), lambda i, t: (i, t, 0)),
            pl.BlockSpec((1, d), lambda i, t: (0, 0)),
            pl.BlockSpec((None, 6, d), lambda i, t: (i, 0, 0)),
            pl.BlockSpec((d, n), lambda i, t: (0, 0)),
        ],
        out_specs=specs,
        out_shape=shapes,
        scratch_shapes=scratch,
        compiler_params=_params(("arbitrary", "arbitrary")),
        name="in_proj",
    )(x, g.reshape(1, d), mod, w_in)
    return outs if dilations else outs[0]


def _t5_bucket(dist):
    max_exact = N_BUCKETS // 2
    d = np.maximum(dist, 1).astype(np.float32)
    large = max_exact + (np.log(d / max_exact) / np.log(MAX_DISTANCE / max_exact)
                         * (N_BUCKETS - max_exact)).astype(np.int32)
    large = np.minimum(large, N_BUCKETS - 1)
    return np.where(dist < max_exact, dist, large).astype(np.int32)


def _branch_bias(rel_table, dilation):
    qi = np.arange(BLK)[:, None]
    kj = np.arange(2 * BLK)[None, :]
    bucket = _t5_bucket(np.clip(qi + BLK - kj, 0, None) * dilation)
    onehot = np.eye(N_BUCKETS, dtype=np.float32)[bucket]
    bias = jnp.einsum("qkb,bh->hqk", onehot, rel_table.astype(F32), precision=lax.Precision.HIGHEST)
    dist = qi + BLK - kj
    in_window = (dist >= 0) & (dist <= BLK)
    return jnp.where(in_window[None], bias, NEG)


def _dil_kernel(q_ref, kp_ref, kc_ref, vp_ref, vc_ref, bias_ref, o_ref, m_ref, l_ref,
                kbuf, vbuf, *, tq):
    n = pl.program_id(2)
    nsub = tq // BLK
    kbuf[0:BLK] = kp_ref[...]
    kbuf[BLK:] = kc_ref[...]
    vbuf[0:BLK] = vp_ref[...]
    vbuf[BLK:] = vc_ref[...]
    qb = DILATED_QUERY_ROWS
    kj = lax.broadcasted_iota(jnp.int32, (qb, 2 * BLK), 1)
    lane = lax.broadcasted_iota(jnp.int32, (qb, PAIR), 1)
    scale = HEAD_DIM ** -0.5
    has_prev = kj >= jnp.where(n == 0, BLK, 0)
    rows = lambda u: slice(u * qb, (u + 1) * qb)
    krows = lambda u: slice(u * qb // BLK * BLK, (u * qb // BLK + 2) * BLK)
    brows = lambda u: slice(u * qb % BLK, u * qb % BLK + qb)
    cols = lambda p: slice(p * PAIR, (p + 1) * PAIR)

    def logits(i, p, hh):
        q_pair = q_ref[rows(i), cols(p)] * scale
        head_lanes = (lane >= HEAD_DIM) if hh else (lane < HEAD_DIM)
        qm = jnp.where(head_lanes, q_pair, jnp.zeros_like(q_pair))
        logit = _bdot_nt(qm, kbuf[krows(i), cols(p)]) + bias_ref[2 * p + hh, brows(i), :]
        return jnp.where(has_prev, logit, NEG) if i * qb < BLK else logit

    def softmax_values(i, p, logit):
        mx = jnp.max(logit, -1, keepdims=True)
        pr = jnp.exp(logit - mx)
        den = jnp.sum(pr, -1, keepdims=True)
        return mx, den, _bdot(pr.astype(BF16), vbuf[krows(i), cols(p)])

    chains = [(i, p, hh) for i in range(tq // qb) for p in range(HEADS // 2) for hh in range(2)]
    lag = DILATED_STAGE_LAG
    lg, sv, first_head = {}, {}, {}
    m_tile, l_tile = {}, {}
    for step in range(len(chains) + 2 * lag):
        if step < len(chains):
            lg[step] = logits(*chains[step])
        if 0 <= step - lag < len(chains):
            i, p, _ = chains[step - lag]
            sv[step - lag] = softmax_values(i, p, lg.pop(step - lag))
        if 0 <= step - 2 * lag < len(chains):
            i, p, hh = chains[step - 2 * lag]
            mx, den, pv = sv.pop(step - 2 * lag)
            h = 2 * p + hh
            m_tile[i] = jnp.where(lane == h, mx, m_tile.get(i, jnp.zeros((qb, LANES), F32)))
            l_tile[i] = jnp.where(lane == h, den, l_tile.get(i, jnp.ones((qb, LANES), F32)))
            if hh == 0:
                first_head[i, p] = pv / den
            else:
                o_ref[rows(i), cols(p)] = jnp.where(
                    lane < HEAD_DIM, first_head.pop((i, p)), pv / den).astype(o_ref.dtype)
            if h == HEADS - 1:
                m_ref[rows(i), :] = m_tile.pop(i)
                l_ref[rows(i), :] = l_tile.pop(i)


def _dilated_branch(view, bias, dilation, ncol):
    b, sub, _ = view.shape
    tq = min(TOKEN_TILE, sub)
    nsub = tq // BLK
    cur = lambda c: pl.BlockSpec((None, tq, WIDTH), lambda i, r, t: (i, t, r * ncol + c))
    prev = lambda c: pl.BlockSpec(
        (None, BLK, WIDTH), lambda i, r, t: (i, jnp.maximum(t * nsub - 1, 0), r * ncol + c))
    return pl.pallas_call(
        functools.partial(_dil_kernel, tq=tq),
        grid=(b, dilation, sub // tq),
        in_specs=[cur(0), prev(1), cur(1), prev(2), cur(2),
                  pl.BlockSpec((HEADS, BLK, 2 * BLK), lambda i, r, t: (0, 0, 0))],
        out_specs=[
            pl.BlockSpec((None, tq, WIDTH), lambda i, r, t: (i, t, r)),
            pl.BlockSpec((None, tq, LANES), lambda i, r, t: (i, t, r)),
            pl.BlockSpec((None, tq, LANES), lambda i, r, t: (i, t, r)),
        ],
        out_shape=[
            jax.ShapeDtypeStruct((b, sub, dilation * WIDTH), BF16),
            jax.ShapeDtypeStruct((b, sub, dilation * LANES), F32),
            jax.ShapeDtypeStruct((b, sub, dilation * LANES), F32),
        ],
        scratch_shapes=[pltpu.VMEM((BLK + tq, WIDTH), BF16), pltpu.VMEM((BLK + tq, WIDTH), BF16)],
        compiler_params=_params(("arbitrary", "arbitrary", "arbitrary")),
        name=f"dilated_d{dilation}",
    )(view, view, view, view, view, bias)


def _comb_body(o_refs, m_refs, l_refs, e_ref, out_ref, o_scr, s_scr, dilations):
    tm = out_ref.shape[0]
    per = WIDTH // LANES
    os, ms, ls = [], [], []
    for bi, d in enumerate(dilations):
        if d == 1:
            os.append([o_refs[bi][:, c * LANES:(c + 1) * LANES].astype(F32) for c in range(per)])
            ms.append(m_refs[bi][...])
            ls.append(l_refs[bi][...])
            continue
        rows = lambda r: pl.ds(r, tm // d, stride=d)
        for r in range(d):
            for c in range(per):
                lo = (r * per + c) * LANES
                o_scr[bi, c, rows(r), :] = o_refs[bi][:, lo:lo + LANES].astype(F32)
            s_scr[bi, 0, rows(r), :] = m_refs[bi][:, r * LANES:(r + 1) * LANES]
            s_scr[bi, 1, rows(r), :] = l_refs[bi][:, r * LANES:(r + 1) * LANES]
        os.append([o_scr[bi, c] for c in range(per)])
        ms.append(s_scr[bi, 0])
        ls.append(s_scr[bi, 1])
    m_all = functools.reduce(jnp.maximum, ms)
    es = [l * jnp.exp(m - m_all) for l, m in zip(ls, ms)]
    tot = functools.reduce(lambda x, y: x + y, es)
    acc = [None] * per
    for e, o in zip(es, os):
        w_hi, w_lo = _split(e / tot)
        w = _bdot(w_hi, e_ref[...]) + _bdot(w_lo, e_ref[...])
        for c in range(per):
            term = w[:, c * LANES:(c + 1) * LANES] * o[c]
            acc[c] = term if acc[c] is None else acc[c] + term
    for c in range(per):
        out_ref[:, c * LANES:(c + 1) * LANES] = acc[c].astype(out_ref.dtype)


def _head_expander():
    expand = np.zeros((LANES, WIDTH), np.float32)
    for h in range(HEADS):
        expand[h, h * HEAD_DIM:(h + 1) * HEAD_DIM] = 1.0
    return jnp.asarray(expand, BF16)


def _gelu(x):
    return 0.5 * x * (1.0 + jnp.tanh(math.sqrt(2.0 / math.pi) * (x + 0.044715 * (x * x * x))))


def _gmlp_body(u_ref, v_ref, g_ref, b_ref, ws_ref, bs_ref, o_ref):
    tm = u_ref.shape[0]
    u = _gelu(u_ref[...].astype(F32))
    v = _gelu(v_ref[...].astype(F32))
    mu = jnp.mean(v, -1, keepdims=True)
    vc = v - mu
    var = jnp.mean(vc * vc, -1, keepdims=True)
    v = (vc * lax.rsqrt(var + EPS) * g_ref[...] + b_ref[...]).astype(BF16)
    r = lax.broadcasted_iota(jnp.int32, (BLK, BLK), 0)
    c = lax.broadcasted_iota(jnp.int32, (BLK, BLK), 1)
    for g in range(B_GROUPS):
        w = jnp.where(c <= r, ws_ref[g], 0.0).astype(BF16)
        cols = slice(g * BLK, (g + 1) * BLK)
        for ch in range(tm // BLK):
            rows = slice(ch * BLK, (ch + 1) * BLK)
            mixed = _bdot(w, v[rows, cols]) + bs_ref[g]
            o_ref[rows, cols] = (u[rows, cols] * mixed).astype(o_ref.dtype)


def _conv_kernel(a_ref, g_ref, ah_ref, gh_ref, w_ref, bdw_ref, lng_ref, lnb_ref, o_ref,
                 hbuf, wbuf, ybuf):
    tm = a_ref.shape[0]
    t = pl.program_id(1)
    halo = ah_ref[...].astype(F32) * jax.nn.sigmoid(gh_ref[...].astype(F32))
    hbuf[0, 0:CONV_HALO] = jnp.where(t == 0, 0.0, halo)
    hbuf[0, CONV_HALO:] = a_ref[...].astype(F32) * jax.nn.sigmoid(g_ref[...].astype(F32))
    n_shifted = tm + CONV_HALO - SUBLANES
    for s in range(1, SUBLANES):
        hbuf[s, 0:n_shifted] = hbuf[0, pl.ds(s, n_shifted), :]
    for k in range(CONV_WIDTH):
        wbuf[k] = jnp.broadcast_to(w_ref[k:k + 1, :], (SUBLANES, WIDTH))
    rc = 32
    off = CONV_HALO - (CONV_WIDTH - 1)

    def chunk(ci, carry):
        base = pl.multiple_of(ci * rc, rc)
        def tap(k, acc):
            q = off + k
            s = q & (SUBLANES - 1)
            start = pl.multiple_of(base + (q - s), SUBLANES)
            win = hbuf[s, pl.ds(start, rc), :].reshape(rc // SUBLANES, SUBLANES, WIDTH)
            return acc + (wbuf[k] * win).reshape(rc, WIDTH)

        acc = lax.fori_loop(0, CONV_WIDTH, tap, jnp.zeros((rc, WIDTH), F32) + bdw_ref[...],
                            unroll=CONV_TAP_UNROLL)
        ybuf[pl.ds(base, rc), :] = acc
        return carry

    lax.fori_loop(0, tm // rc, chunk, 0)
    acc = ybuf[...]
    mu = jnp.mean(acc, -1, keepdims=True)
    xc = acc - mu
    var = jnp.mean(xc * xc, -1, keepdims=True)
    y = xc * lax.rsqrt(var + EPS) * lng_ref[...] + lnb_ref[...]
    o_ref[...] = (y * jax.nn.sigmoid(y)).astype(o_ref.dtype)


def _conv_module(proj, w_dw, b_dw, ln_g, ln_b):
    b, s, n = proj.shape
    tm = TOKEN_TILE
    hb = tm // CONV_HALO
    row = lambda v: v.reshape(1, WIDTH)
    halo = lambda c: pl.BlockSpec(
        (None, CONV_HALO, WIDTH), lambda i, t: (i, jnp.maximum(t * hb - 1, 0), c))
    return pl.pallas_call(
        _conv_kernel,
        grid=(b, s // tm),
        in_specs=[
            pl.BlockSpec((None, tm, WIDTH), lambda i, t: (i, t, 0)),
            pl.BlockSpec((None, tm, WIDTH), lambda i, t: (i, t, 1)),
            halo(0), halo(1),
            pl.BlockSpec((CONV_WIDTH, WIDTH), lambda i, t: (0, 0)),
            pl.BlockSpec((1, WIDTH), lambda i, t: (0, 0)),
            pl.BlockSpec((1, WIDTH), lambda i, t: (0, 0)),
            pl.BlockSpec((1, WIDTH), lambda i, t: (0, 0)),
        ],
        out_specs=pl.BlockSpec((None, tm, WIDTH), lambda i, t: (i, t, 0)),
        out_shape=jax.ShapeDtypeStruct((b, s, WIDTH), BF16),
        scratch_shapes=[pltpu.VMEM((SUBLANES, CONV_HALO + tm, WIDTH), F32),
                        pltpu.VMEM((CONV_WIDTH, SUBLANES, WIDTH), F32),
                        pltpu.VMEM((tm, WIDTH), F32)],
        compiler_params=_params(("arbitrary", "arbitrary")),
        name="conv_module",
    )(proj, proj, proj, proj, w_dw, row(b_dw), row(ln_g), row(ln_b))


def _stick_kernel(q_ref, k_ref, v_ref, mm_ref, o_ref, q2_ref, carry_ref, acc_ref):
    i = pl.program_id(1)
    npair = WIDTH // PAIR
    lane = lax.broadcasted_iota(jnp.int32, (BLK, PAIR), 1)
    lo_lanes = lane < HEAD_DIM
    row2 = lax.broadcasted_iota(jnp.int32, (2 * BLK, BLK), 0)
    col2 = lax.broadcasted_iota(jnp.int32, (2 * BLK, BLK), 1)
    causal2 = col2 < (row2 & (BLK - 1))
    scale = HEAD_DIM ** -0.5
    for p in range(npair):
        qp = q_ref[:, p * PAIR:(p + 1) * PAIR] * scale
        zero = jnp.zeros_like(qp)
        q2_ref[p] = jnp.concatenate(
            [jnp.where(lo_lanes, qp, zero), jnp.where(lo_lanes, zero, qp)], axis=0)
    carry_ref[...] = jnp.zeros_like(carry_ref)
    acc_ref[...] = jnp.zeros_like(acc_ref)

    def blocks(js, diag_first):
        cols = lambda p: slice(p * PAIR, (p + 1) * PAIR)
        rows = lambda j: pl.ds(pl.multiple_of(j * BLK, BLK), BLK)

        def scores(j, p):
            return _bdot_nt(q2_ref[p], k_ref[rows(j), cols(p)])

        def keep_sums(z, diag):
            soft = jnp.log(1.0 + jnp.exp2(jnp.abs(z) * -LOG2E))
            log_beta = jnp.minimum(z, 0.0) - soft
            log_keep = log_beta - z
            if diag:
                log_keep = jnp.where(causal2, log_keep, 0.0)
            return log_beta, _bdot(log_keep.astype(BF16), mm_ref[...])

        def weighted_values(j, p, diag, log_beta, sums):
            carry = carry_ref[p]
            att = jnp.exp(log_beta + (sums[:, :BLK] + carry))
            if diag:
                att = jnp.where(causal2, att, 0.0)
            carry_ref[p] = carry + sums[:, BLK:]
            att = att.astype(BF16)
            vb = v_ref[rows(j), cols(p)]
            zero = jnp.zeros_like(vb)
            v2 = jnp.concatenate([jnp.where(lo_lanes, vb, zero), jnp.where(lo_lanes, zero, vb)], axis=0)
            return _bdot(jnp.concatenate([att[:BLK], att[BLK:]], axis=1), v2)

        chains = [(j, p, diag_first and n == 0) for n, j in enumerate(js) for p in range(npair)]
        z, ks, pv = {}, {}, {}
        lag = STICK_STAGE_LAG
        for step in range(len(chains) + 3 * lag):
            if step < len(chains):
                z[step] = scores(*chains[step][:2])
            if 0 <= step - lag < len(chains):
                ks[step - lag] = keep_sums(z.pop(step - lag), chains[step - lag][2])
            if 0 <= step - 2 * lag < len(chains):
                pv[step - 2 * lag] = weighted_values(*chains[step - 2 * lag], *ks.pop(step - 2 * lag))
            if 0 <= step - 3 * lag < len(chains):
                acc_ref[chains[step - 3 * lag][1]] += pv.pop(step - 3 * lag)

    @pl.when(i >= 2)
    def _():
        blocks([i, i - 1, i - 2], True)

    @pl.when(i == 1)
    def _():
        blocks([i, i - 1], True)

    @pl.when(i == 0)
    def _():
        blocks([i], True)

    rest = i - jnp.minimum(i, 2)

    def live():
        c = carry_ref[...]
        worst = jnp.max(jnp.max(c, axis=0), axis=0, keepdims=True)
        return (worst[0, 0] >= EXP_IS_ZERO_BELOW).astype(jnp.int32)

    def two_blocks(state):
        t, _ = state
        j = rest - 1 - 2 * t
        blocks([j, j - 1], False)
        return t + 1, live()

    _, alive = lax.while_loop(lambda st: (st[0] < rest // 2) & (st[1] == 1), two_blocks,
                              (jnp.int32(0), live()))

    @pl.when((rest % 2 == 1) & (alive == 1))
    def _():
        blocks([0], False)

    for p in range(npair):
        o_ref[:, p * PAIR:(p + 1) * PAIR] = acc_ref[p].astype(o_ref.dtype)


def _suffix_sum_matrix():
    sp = np.arange(BLK)[:, None]
    sc = np.arange(BLK)[None, :]
    return jnp.asarray(
        np.concatenate([(sp > sc).astype(np.float32), np.ones((BLK, BLK), np.float32)], axis=1), BF16)


def _stick_breaking(proj):
    b, s, n = proj.shape
    npair = WIDTH // PAIR
    return pl.pallas_call(
        _stick_kernel,
        grid=(b, s // BLK),
        in_specs=[
            pl.BlockSpec((None, BLK, WIDTH), lambda i, t: (i, t, 2)),
            pl.BlockSpec((None, s, WIDTH), lambda i, t: (i, 0, 3)),
            pl.BlockSpec((None, s, WIDTH), lambda i, t: (i, 0, 4)),
            pl.BlockSpec((BLK, 2 * BLK), lambda i, t: (0, 0)),
        ],
        out_specs=pl.BlockSpec((None, BLK, WIDTH), lambda i, t: (i, t, 0)),
        out_shape=jax.ShapeDtypeStruct((b, s, WIDTH), BF16),
        scratch_shapes=[pltpu.VMEM((npair, 2 * BLK, PAIR), BF16),
                        pltpu.VMEM((npair, 2 * BLK, BLK), F32),
                        pltpu.VMEM((npair, BLK, PAIR), F32)],
        compiler_params=_params(("arbitrary", "arbitrary")),
        name="stick_breaking",
    )(proj, proj, proj, _suffix_sum_matrix())


def _lane_min_index(mask, lane):
    return jnp.min(jnp.where(mask, lane, float(LANES)), -1, keepdims=True)


def _out_core(left, right, w_ref, x_ref, mod_ref, g_ref, rh_ref, rl_ref, rb_ref, tri_ref,
              xo_ref, h_ref, cls_ref, rank_ref, cnt_ref, seen):
    tm = x_ref.shape[0]

    @pl.when((pl.program_id(0) == 0) & (pl.program_id(1) == 0))
    def _():
        seen[...] = jnp.zeros_like(seen)

    m = mod_ref[...]
    mix = _bdot(left, w_ref[0:WIDTH, :]) + _bdot(right, w_ref[WIDTH:, :])
    x = x_ref[...] + m[2:3] * mix
    xo_ref[...] = x
    h = _rms_mod(x, g_ref[...], m[4:5], m[3:4])
    h_ref[...] = h
    logits = _dot3(h, rh_ref[...], rl_ref[...]) + rb_ref[...]
    lane = lax.broadcasted_iota(jnp.int32, logits.shape, 1).astype(F32)
    is_group = (lane >= N_EXPERTS) & (lane < N_EXPERTS + N_GROUPS)
    glog = jnp.where(is_group, logits, -jnp.inf)
    gmax = jnp.max(glog, -1, keepdims=True)
    g_idx = _lane_min_index(glog == gmax, lane) - N_EXPERTS
    in_group = (lane >= g_idx * EXPERTS_PER_GROUP) & (lane < (g_idx + 1) * EXPERTS_PER_GROUP)
    ev = jnp.where(in_group, logits, -jnp.inf)
    v1 = jnp.max(ev, -1, keepdims=True)
    i1 = _lane_min_index(ev == v1, lane)
    ev2 = jnp.where(lane == i1, -jnp.inf, ev)
    v2 = jnp.max(ev2, -1, keepdims=True)
    i2 = _lane_min_index(ev2 == v2, lane)
    lo = jnp.minimum(i1, i2) - g_idx * EXPERTS_PER_GROUP
    hi = jnp.maximum(i1, i2) - g_idx * EXPERTS_PER_GROUP
    cls = g_idx * PAIRS_PER_GROUP + (lo * (7.0 - lo) * 0.5 + (hi - lo - 1.0))
    cls_t = jnp.transpose(jnp.broadcast_to(cls, (tm, LANES)))
    onehot_t = lax.broadcasted_iota(jnp.int32, (LANES, tm), 0).astype(F32) == cls_t
    ones_t = jnp.where(onehot_t, 1.0, 0.0)
    earlier = _bdot(ones_t.astype(BF16), tri_ref[...]) + seen[...]
    cls_ref[...] = cls_t[0:1, :].astype(jnp.int32)
    rank_ref[...] = jnp.sum(jnp.where(onehot_t, earlier, 0.0), 0, keepdims=True).astype(jnp.int32)
    seen[...] += jnp.sum(ones_t, 1, keepdims=True)
    cnt_ref[...] = seen[...]


def _out_kernel(l_ref, r_ref, *rest):
    _out_core(l_ref[...], r_ref[...], *rest)


def _out_even_kernel(*refs, dilations):
    nb = len(dilations)
    o_refs, m_refs, l_refs = refs[:nb], refs[nb:2 * nb], refs[2 * nb:3 * nb]
    e_ref, u_ref, v_ref, lng_ref, lnb_ref, ws_ref, bs_ref = refs[3 * nb:3 * nb + 7]
    *rest, o_scr, s_scr, left, right = refs[3 * nb + 7:]
    _comb_body(o_refs, m_refs, l_refs, e_ref, left, o_scr, s_scr, dilations)
    _gmlp_body(u_ref, v_ref, lng_ref, lnb_ref, ws_ref, bs_ref, right)
    _out_core(left[...], right[...], *rest)


def _out_proj(mixers, w_out, x, mod, g, r_hi, r_lo, r_b):
    b, s, d = x.shape
    tm = TOKEN_TILE
    nt = s // tm
    tile = lambda w: pl.BlockSpec((None, tm, w), lambda i, t: (i, t, 0))
    const = lambda shp: pl.BlockSpec(shp, lambda i, t: tuple(0 for _ in shp))
    row = pl.BlockSpec((None, 1, tm), lambda i, t: (i * nt + t, 0, 0))
    tri = np.arange(tm)[:, None] < np.arange(tm)[None, :]
    scratch = [pltpu.VMEM((LANES, 1), F32)]
    if mixers[0] == "halves":
        kern = _out_kernel
        specs, args = [tile(WIDTH), tile(WIDTH)], list(mixers[1:])
    else:
        _, outs, dilations, proj, ln_g, ln_b, w_s, b_s = mixers
        nb = len(dilations)
        kern = functools.partial(_out_even_kernel, dilations=tuple(dilations))
        grouped = lambda w: [pl.BlockSpec((None, tm // dil, dil * w), lambda i, t: (i, t, 0))
                             for dil in dilations]
        specs = (grouped(WIDTH) + grouped(LANES) + grouped(LANES) + [const((LANES, WIDTH))]
                 + [pl.BlockSpec((None, tm, WIDTH), lambda i, t: (i, t, 3)),
                    pl.BlockSpec((None, tm, WIDTH), lambda i, t: (i, t, 4)),
                    const((1, WIDTH)), const((1, WIDTH)),
                    const((B_GROUPS, BLK, BLK)), const((B_GROUPS, BLK, 1))])
        args = ([o for o, _, _ in outs] + [m for _, m, _ in outs] + [l for _, _, l in outs]
                + [_head_expander(), proj, proj, ln_g.reshape(1, WIDTH), ln_b.reshape(1, WIDTH),
                   w_s, b_s.reshape(B_GROUPS, BLK, 1)])
        scratch += [pltpu.VMEM((nb, WIDTH // LANES, tm, LANES), F32),
                    pltpu.VMEM((nb, 2, tm, LANES), F32),
                    pltpu.VMEM((tm, WIDTH), BF16), pltpu.VMEM((tm, WIDTH), BF16)]
    return pl.pallas_call(
        kern,
        grid=(b, nt),
        in_specs=specs + [const((2 * WIDTH, d)), tile(d),
                          pl.BlockSpec((None, 6, d), lambda i, t: (i, 0, 0)),
                          const((1, d)), const((d, LANES)), const((d, LANES)), const((1, LANES)),
                          const((tm, tm))],
        out_specs=[tile(d), tile(d), row, row, const((LANES, 1))],
        out_shape=[jax.ShapeDtypeStruct((b, s, d), F32),
                   jax.ShapeDtypeStruct((b, s, d), F32),
                   jax.ShapeDtypeStruct((b * nt, 1, tm), jnp.int32),
                   jax.ShapeDtypeStruct((b * nt, 1, tm), jnp.int32),
                   jax.ShapeDtypeStruct((LANES, 1), F32)],
        scratch_shapes=scratch,
        compiler_params=_params(("arbitrary", "arbitrary")),
        name="out_proj_router",
    )(*args, w_out, x, mod, g.reshape(1, d), r_hi, r_lo, r_b, jnp.asarray(tri, BF16))


def _router_weights(w_group, b_group, w_router, b_router):
    d = w_group.shape[0]
    w = jnp.concatenate([jnp.transpose(w_router, (1, 0, 2)).reshape(d, N_EXPERTS), w_group], axis=1)
    w = jnp.pad(w.astype(F32), ((0, 0), (0, LANES - w.shape[1])))
    bias = jnp.concatenate([b_router.reshape(N_EXPERTS), b_group]).astype(F32)
    bias = jnp.pad(bias, (0, LANES - bias.shape[0])).reshape(1, LANES)
    hi, lo = _split(w)
    return hi, lo, bias


def _to_token_tiles(x):
    n = x.shape[0]
    parts = [x[:, g * LANES:(g + 1) * LANES].reshape(n // SUBLANES, SUBLANES, LANES)
             for g in range(x.shape[1] // LANES)]
    return jnp.swapaxes(jnp.stack(parts, axis=1), 1, 2).reshape(n, len(parts), LANES)


def _from_token_tiles(v):
    n = v.shape[0]
    groups = v.shape[1]
    w = jnp.swapaxes(v.reshape(n // SUBLANES, SUBLANES, groups, LANES), 1, 2)
    return jnp.concatenate([w[:, g].reshape(n, LANES) for g in range(groups)], axis=-1)


def _row_dma_loop(n, row_copy):
    def body(i, carry):
        row_copy(2 * i).start(priority=0)
        row_copy(2 * i + 1).start(priority=1)
        return carry
    lax.fori_loop(0, n // 2, body, 0, unroll=4)


def _dispatch_kernel(pad_ref, dest_ref, h_ref, hs_ref, buf, sem):
    tm = h_ref.shape[0]
    i = pl.program_id(0)
    slot = i % 2

    def wait_rows(which):
        pltpu.make_async_copy(buf.at[which], hs_ref.at[pl.ds(0, tm)], sem.at[which]).wait()

    @pl.when(i == 0)
    def _():
        buf[0] = jnp.zeros((tm, ROW_TILE, LANES), F32)
        for c in range(pad_ref.shape[0]):
            fill = pltpu.make_async_copy(buf.at[0], hs_ref.at[pl.ds(pad_ref[c], tm)], sem.at[0])
            fill.start()
            fill.wait()

    buf[slot] = _to_token_tiles(h_ref[...])
    _row_dma_loop(tm, lambda r: pltpu.make_async_copy(
        buf.at[slot].at[r], hs_ref.at[dest_ref[0, r]], sem.at[slot]))

    @pl.when(i > 0)
    def _():
        wait_rows(1 - slot)

    @pl.when(i == pl.num_programs(0) - 1)
    def _():
        wait_rows(slot)


def _dispatch(h, dest, pad_start, n_slots):
    b, s, d = h.shape
    tm = TOKEN_TILE
    assert tm == SLOT_TILE
    nt = s // tm
    return pl.pallas_call(
        _dispatch_kernel,
        grid_spec=pltpu.PrefetchScalarGridSpec(
            num_scalar_prefetch=1,
            grid=(b * nt,),
            in_specs=[pl.BlockSpec((None, 1, tm), lambda i, pad: (i, 0, 0), memory_space=pltpu.SMEM),
                      pl.BlockSpec((None, tm, d), lambda i, pad: (i // nt, i % nt, 0))],
            out_specs=pl.BlockSpec(memory_space=pl.ANY),
            scratch_shapes=[pltpu.VMEM((2, tm, ROW_TILE, LANES), F32),
                            pltpu.SemaphoreType.DMA((2,))]),
        out_shape=jax.ShapeDtypeStruct((n_slots, ROW_TILE, LANES), F32),
        compiler_params=_params(("arbitrary",)),
        name="moe_dispatch",
    )(pad_start, dest, h)


def _moe_kernel(blk_ref, ea_ref, eb_ref, grp_ref, used_ref, hs_ref, wga_ref, wua_ref, wda_ref,
                wgb_ref, wub_ref, wdb_ref, rh_ref, rb_ref, o_ref):
    n = pl.program_id(0)

    @pl.when(used_ref[n] == 1)
    def _():
        h = _from_token_tiles(hs_ref[...]).astype(BF16)
        logits = _bdot(h, rh_ref[...]) + rb_ref[...]
        lane = lax.broadcasted_iota(jnp.int32, logits.shape, 1)
        pick = lambda idx: jnp.sum(jnp.where(lane == idx, logits, 0.0), -1, keepdims=True)
        is_group = (lane >= N_EXPERTS) & (lane < N_EXPERTS + N_GROUPS)
        g_w = 1.0 / jnp.sum(
            jnp.where(is_group, jnp.exp(logits - pick(N_EXPERTS + grp_ref[n])), 0.0),
            -1, keepdims=True)
        la = pick(ea_ref[n])
        lb = pick(eb_ref[n])

        def expert(wg_ref, wu_ref, wd_ref):
            a = _bdot(h, wg_ref[...])
            hid = (a * jax.nn.sigmoid(a)) * _bdot(h, wu_ref[...])
            return _bdot(hid.astype(BF16), wd_ref[...])

        y = (g_w * (1.0 / (1.0 + jnp.exp(lb - la)))) * expert(wga_ref, wua_ref, wda_ref)
        y = y + (g_w * (1.0 / (1.0 + jnp.exp(la - lb)))) * expert(wgb_ref, wub_ref, wdb_ref)
        o_ref[...] = _to_token_tiles(y)

    @pl.when(used_ref[n] == 0)
    def _():
        o_ref[...] = jnp.zeros_like(o_ref)


def _moe(hs, plan, w_gate, w_up, w_down, r_hi, r_b):
    n_slots = hs.shape[0]
    ne, d, de = w_gate.shape
    tm = SLOT_TILE
    slot = pl.BlockSpec((tm, ROW_TILE, LANES), lambda n, *_: (n, 0, 0))
    slot_in = pl.BlockSpec((tm, ROW_TILE, LANES), lambda n, blk, *_: (blk[n], 0, 0))
    first = lambda shp: pl.BlockSpec((None,) + shp, lambda n, blk, ea, eb, grp, used: (ea[n], 0, 0))
    second = lambda shp: pl.BlockSpec((None,) + shp, lambda n, blk, ea, eb, grp, used: (eb[n], 0, 0))
    const = lambda shp: pl.BlockSpec(shp, lambda n, *_: tuple(0 for _ in shp))
    return pl.pallas_call(
        _moe_kernel,
        grid_spec=pltpu.PrefetchScalarGridSpec(
            num_scalar_prefetch=5,
            grid=(n_slots // tm,),
            in_specs=[slot_in, first((d, de)), first((d, de)), first((de, d)),
                      second((d, de)), second((d, de)), second((de, d)),
                      const((d, LANES)), const((1, LANES))],
            out_specs=slot),
        out_shape=jax.ShapeDtypeStruct((n_slots, ROW_TILE, LANES), F32),
        compiler_params=_params(("arbitrary",)),
        name="moe_experts",
    )(*plan, hs, w_gate, w_up, w_down, w_gate, w_up, w_down, r_hi, r_b)


def _combine_kernel(dest_ref, ys_ref, x_ref, mod_ref, fg_ref, o_ref, buf, sem):
    tm = x_ref.shape[0]
    _row_dma_loop(tm, lambda r: pltpu.make_async_copy(
        ys_ref.at[dest_ref[0, r]], buf.at[r], sem))
    pltpu.make_async_copy(ys_ref.at[pl.ds(0, tm)], buf, sem).wait()
    x = x_ref[...] + mod_ref[5:6, :] * _from_token_tiles(buf[...])
    o_ref[...] = x * lax.rsqrt(jnp.mean(x * x, -1, keepdims=True) + EPS) * fg_ref[...]


def _combine(ys, dest, x, mod, final_g):
    b, s, d = x.shape
    tm = TOKEN_TILE
    nt = s // tm
    return pl.pallas_call(
        _combine_kernel,
        grid=(b * nt,),
        in_specs=[pl.BlockSpec((None, 1, tm), lambda i: (i, 0, 0), memory_space=pltpu.SMEM),
                  pl.BlockSpec(memory_space=pl.ANY),
                  pl.BlockSpec((None, tm, d), lambda i: (i // nt, i % nt, 0)),
                  pl.BlockSpec((None, 6, d), lambda i: (i // nt, 0, 0)),
                  pl.BlockSpec((1, d), lambda i: (0, 0))],
        out_specs=pl.BlockSpec((None, tm, d), lambda i: (i // nt, i % nt, 0)),
        out_shape=jax.ShapeDtypeStruct((b, s, d), F32),
        scratch_shapes=[pltpu.VMEM((tm, ROW_TILE, LANES), F32), pltpu.SemaphoreType.DMA(())],
        compiler_params=_params(("arbitrary",)),
        name="moe_combine",
    )(dest, ys, x, mod, final_g.reshape(1, d))


def _combine_in_kernel(dest_ref, next_ref, ys_ref, x_ref, mod_ref, g_ref, nmod_ref, w_ref,
                       xo_ref, o_ref, *rest, dilations):
    *rest, buf, sem = rest
    tm = x_ref.shape[0]
    i = pl.program_id(0)
    slot = i % 2

    def gather(idx_ref, to):
        _row_dma_loop(tm, lambda r: pltpu.make_async_copy(
            ys_ref.at[idx_ref[0, r]], buf.at[to].at[r], sem.at[to]))

    @pl.when(i == 0)
    def _():
        gather(dest_ref, 0)

    @pl.when(i + 1 < pl.num_programs(0))
    def _():
        gather(next_ref, 1 - slot)

    pltpu.make_async_copy(ys_ref.at[pl.ds(0, tm)], buf.at[slot], sem.at[slot]).wait()
    x = x_ref[...] + mod_ref[5:6, :] * _from_token_tiles(buf[slot])
    xo_ref[...] = x
    _in_body(x, g_ref, nmod_ref, w_ref, o_ref, rest, dilations)


def _combine_in_proj(ys, dest, x, mod, g_next, mod_next, w_in, dilations=()):
    b, s, d = x.shape
    n = w_in.shape[1]
    tm = TOKEN_TILE
    nt = s // tm
    last = b * nt - 1
    tile = lambda i: (i // nt, i % nt, 0)
    specs, shapes, scratch = _in_proj_outputs(b, s, n, tm, dilations, tile)
    outs = pl.pallas_call(
        functools.partial(_combine_in_kernel, dilations=tuple(dilations)),
        grid=(b * nt,),
        in_specs=[pl.BlockSpec((None, 1, tm), lambda i: (i, 0, 0), memory_space=pltpu.SMEM),
                  pl.BlockSpec((None, 1, tm), lambda i: (jnp.minimum(i + 1, last), 0, 0),
                               memory_space=pltpu.SMEM),
                  pl.BlockSpec(memory_space=pl.ANY),
                  pl.BlockSpec((None, tm, d), tile),
                  pl.BlockSpec((None, 6, d), lambda i: (i // nt, 0, 0)),
                  pl.BlockSpec((1, d), lambda i: (0, 0)),
                  pl.BlockSpec((None, 6, d), lambda i: (i // nt, 0, 0)),
                  pl.BlockSpec((d, n), lambda i: (0, 0))],
        out_specs=[pl.BlockSpec((None, tm, d), tile)] + specs,
        out_shape=[jax.ShapeDtypeStruct((b, s, d), F32)] + shapes,
        scratch_shapes=scratch + [pltpu.VMEM((2, tm, ROW_TILE, LANES), F32), pltpu.SemaphoreType.DMA((2,))],
        compiler_params=_params(("arbitrary",)),
        name="moe_combine_in_proj",
    )(dest, dest, ys, x, mod, g_next.reshape(1, d), mod_next, w_in)
    return outs


def _routing_plan(cls, rank, counts, n_tokens):
    tm = SLOT_TILE
    n_tiles = n_tokens // tm + N_CLASSES
    counts = counts[:N_CLASSES, 0].astype(jnp.int32)
    tiles = (counts + tm - 1) // tm
    tile_end = jnp.cumsum(tiles)
    classes = jnp.arange(N_CLASSES, dtype=jnp.int32)
    first_slot = (tile_end - tiles) * tm
    dest = jnp.sum(jnp.where(cls[..., None] == classes, first_slot, 0), -1) + rank
    ids = jnp.arange(n_tiles, dtype=jnp.int32)
    used = ids < tile_end[-1]
    tile_cls = jnp.sum(
        (jnp.minimum(ids, tile_end[-1] - 1)[:, None] >= tile_end).astype(jnp.int32), -1)
    grp = tile_cls // PAIRS_PER_GROUP
    pair = tile_cls % PAIRS_PER_GROUP
    pair_a = jnp.asarray([0, 0, 0, 1, 1, 2], jnp.int32)
    pair_b = jnp.asarray([1, 2, 3, 2, 3, 3], jnp.int32)
    eb = grp * EXPERTS_PER_GROUP + pair_b[pair]
    ea = jnp.where(used, grp * EXPERTS_PER_GROUP + pair_a[pair], eb)
    blk = jnp.where(used, ids, 0)
    tail = jnp.minimum(tile_end[-1] * tm + classes * tm, (n_tiles - 1) * tm)
    zero_blocks = jnp.concatenate([first_slot + counts, tail])
    return dest, zero_blocks, (blk, ea, eb, grp, used.astype(jnp.int32))


@jax.jit
def kernel(x, c, rel_bias, norm1_g, norm2_g, w_ada, b_ada, ev_w_in, ev_w_out, ev_gmlp_ln_g, ev_gmlp_ln_b, ev_w_s, ev_b_s, od_w_in, od_w_out, od_w_dw, od_b_dw, od_conv_ln_g, od_conv_ln_b, moe_w_group, moe_b_group, moe_w_router, moe_b_router, moe_w_gate, moe_w_up, moe_w_down, final_norm_g):
    b, s, d = x.shape
    depth = w_ada.shape[0]
    mods = _ada(c, w_ada, b_ada).reshape(depth, b, 6, d)
    biases = [_branch_bias(rel_bias, dil) for _, dil in DILATED_BRANCHES]
    dils = [dil for _, dil in DILATED_BRANCHES]
    grouped_dils = [dil for dil in dils if dil > 1]

    def in_weights(layer):
        even = layer % 2 == 0
        return (ev_w_in if even else od_w_in)[layer // 2].astype(BF16), grouped_dils if even else []

    w_in, extra = in_weights(0)
    proj, *grouped = _in_proj(x, norm1_g[0], mods[0], w_in, extra)
    for layer in range(depth):
        j = layer // 2
        mod = mods[layer]
        if layer % 2 == 0:
            views = {1: (proj, proj.shape[-1] // WIDTH)}
            views.update({dil: (g, 3) for dil, g in zip(grouped_dils, grouped)})
            outs = [_dilated_branch(views[dil][0], bias, dil, views[dil][1])
                    for bias, dil in zip(biases, dils)]
            mixers = ("even", outs, dils, proj, ev_gmlp_ln_g[j], ev_gmlp_ln_b[j], ev_w_s[j], ev_b_s[j])
            w_out = ev_w_out[j]
        else:
            mixers = ("halves",
                      _conv_module(proj, od_w_dw[j], od_b_dw[j], od_conv_ln_g[j], od_conv_ln_b[j]),
                      _stick_breaking(proj))
            w_out = od_w_out[j]
        r_hi, r_lo, r_b = _router_weights(moe_w_group[layer], moe_b_group[layer],
                                          moe_w_router[layer], moe_b_router[layer])
        x, h, cls, rank, counts = _out_proj(mixers, w_out.astype(BF16), x, mod,
                                            norm2_g[layer], r_hi, r_lo, r_b)
        dest, pad_start, plan = _routing_plan(cls, rank, counts, b * s)
        hs = _dispatch(h, dest, pad_start, (b * s // SLOT_TILE + N_CLASSES) * SLOT_TILE)
        ys = _moe(hs, plan, moe_w_gate[layer].astype(BF16), moe_w_up[layer].astype(BF16),
                  moe_w_down[layer].astype(BF16), r_hi, r_b)
        if layer == depth - 1:
            x = _combine(ys, dest, x, mod, final_norm_g)
        else:
            w_in, extra = in_weights(layer + 1)
            x, proj, *grouped = _combine_in_proj(ys, dest, x, mod, norm1_g[layer + 1],
                                                 mods[layer + 1], w_in, extra)
    return x
```
